```python
import jax, jax.numpy as jnp
from jax import lax
import numpy as np


D_MODEL = 1024
BATCH = 32
SEQ = 2048
DEPTH = 2

GRID_W = 64
CTX_LEN = 256
N_MIXERS = 2
N_ATT_LAYERS = (DEPTH + 1) // 2
N_MLSTM_LAYERS = DEPTH // 2
ATT_HEADS = 8
ATT_KV_HEADS = 2
ATT_GROUP = ATT_HEADS // ATT_KV_HEADS
ATT_HEAD_DIM = D_MODEL // ATT_HEADS
ROPE_PAIRS_PER_AXIS = ATT_HEAD_DIM // 4
ROPE_THETA = 10000.0
Q_BLOCK = 128
MLSTM_HEADS = 4
MLSTM_V_DIM = D_MODEL // MLSTM_HEADS
MLSTM_QK_DIM = MLSTM_V_DIM // 2
MLSTM_CONV_W = 3
MLSTM_CHUNK = 64
D_FF = 2816
N_MOD = 9
DEEPNORM_ALPHA = (2.0 * DEPTH) ** 0.25
DEEPNORM_BETA = (8.0 * DEPTH) ** -0.25
LN_EPS = 1e-5
RMS_EPS = 1e-6

kernel_name = "hybrid_gqa_mlstm_macaron_dit"


def layer_norm(x, g, b):
    xf = x.astype(jnp.float32)
    mu = jnp.mean(xf, axis=-1, keepdims=True)
    var = jnp.mean(jnp.square(xf - mu), axis=-1, keepdims=True)
    return ((xf - mu) * lax.rsqrt(var + LN_EPS) * g + b).astype(x.dtype)


def rms_norm(x, g):
    xf = x.astype(jnp.float32)
    return (xf * lax.rsqrt(jnp.mean(jnp.square(xf), axis=-1, keepdims=True) + RMS_EPS) * g).astype(x.dtype)


def modulate(h, shift, scale):
    return h * (1.0 + scale) + shift


def post_norm(h, delta, g, b):
    return layer_norm(DEEPNORM_ALPHA * h + delta, g, b)


def half_ffn(h, shift, scale, gate, w_in, w_out, g, b):
    xm = modulate(h, shift, scale)
    a, u = jnp.split(xm @ w_in, 2, axis=-1)
    y = (jax.nn.silu(a) * u) @ w_out
    return post_norm(h, 0.5 * gate * y, g, b)


def axial_rope_tables(n_tokens):
    rows = n_tokens // GRID_W
    row = jnp.repeat(jnp.arange(rows, dtype=jnp.int32), GRID_W).astype(jnp.float32)
    col = jnp.tile(jnp.arange(GRID_W, dtype=jnp.int32), rows).astype(jnp.float32)
    inv = ROPE_THETA ** (-jnp.arange(ROPE_PAIRS_PER_AXIS, dtype=jnp.float32) / ROPE_PAIRS_PER_AXIS)
    ang = jnp.stack([row[:, None] * inv, col[:, None] * inv], axis=1)
    return jnp.cos(ang), jnp.sin(ang)


def apply_rope(x, cos, sin):
    B, L, H, Dh = x.shape
    xr = x.reshape(B, L, H, 2, 2, ROPE_PAIRS_PER_AXIS)
    x1, x2 = xr[..., 0, :], xr[..., 1, :]
    c = cos[None, :, None]
    s = sin[None, :, None]
    out = jnp.stack([x1 * c - x2 * s, x2 * c + x1 * s], axis=-2)
    return out.reshape(B, L, H, Dh).astype(x.dtype)


def gqa_softmax(q, k, v):
    s = jnp.einsum('bqhgd,bkhd->bhgqk', q, k).astype(jnp.float32) * (ATT_HEAD_DIM ** -0.5)
    p = jax.nn.softmax(s, axis=-1).astype(v.dtype)
    return jnp.einsum('bhgqk,bkhd->bqhgd', p, v)


def attention_mixer(xl, xc, cos, sin, w_in, q_gain, k_gain, w_out, need_ctx):
    def proj(xs):
        B, L, _ = xs.shape
        p = xs @ w_in
        q, k, v = jnp.split(p, [ATT_HEADS * ATT_HEAD_DIM, (ATT_HEADS + ATT_KV_HEADS) * ATT_HEAD_DIM], axis=-1)
        q = rms_norm(q.reshape(B, L, ATT_HEADS, ATT_HEAD_DIM), q_gain)
        k = rms_norm(k.reshape(B, L, ATT_KV_HEADS, ATT_HEAD_DIM), k_gain)
        v = v.reshape(B, L, ATT_KV_HEADS, ATT_HEAD_DIM)
        return q, k, v

    ql, kl, vl = proj(xl)
    qc, kc, vc = proj(xc)
    ql = apply_rope(ql, cos, sin)
    kl = apply_rope(kl, cos, sin)
    B, L = xl.shape[0], xl.shape[1]
    k_all = jnp.concatenate([kl, kc], axis=1)
    v_all = jnp.concatenate([vl, vc], axis=1)
    nb = L // Q_BLOCK
    qb = jnp.moveaxis(ql.reshape(B, nb, Q_BLOCK, ATT_KV_HEADS, ATT_GROUP, ATT_HEAD_DIM), 1, 0)
    ob = lax.map(lambda qblk: gqa_softmax(qblk, k_all, v_all), qb)
    o_lat = jnp.moveaxis(ob, 0, 1).reshape(B, L, D_MODEL) @ w_out
    o_ctx = None
    if need_ctx:
        Lc = xc.shape[1]
        oc = gqa_softmax(qc.reshape(B, Lc, ATT_KV_HEADS, ATT_GROUP, ATT_HEAD_DIM), kc, vc)
        o_ctx = oc.reshape(B, Lc, D_MODEL) @ w_out
    return o_lat, o_ctx


def centred_conv(a, w, b):
    L = a.shape[1]
    pad = MLSTM_CONV_W // 2
    ap = jnp.pad(a, ((0, 0), (pad, pad), (0, 0)))
    out = ap[:, 0:L] * w[0]
    for j in range(1, MLSTM_CONV_W):
        out = out + ap[:, j:j + L] * w[j]
    return out + b


def mlstm_scan(q, k, v, log_i, log_f, state):
    B, H, L, dk = q.shape
    dv = v.shape[-1]
    Lc = MLSTM_CHUNK
    nc = L // Lc

    def chunks(a):
        return jnp.moveaxis(a.reshape((B, H, nc, Lc) + a.shape[3:]), 2, 0)

    tri = jnp.tril(jnp.ones((Lc, Lc), dtype=bool))

    def step(carry, inp):
        C, n, m = carry
        qc, kc, vc, ic, fc = inp
        bcum = jnp.cumsum(fc, axis=-1)
        d = jnp.where(tri, bcum[..., :, None] - bcum[..., None, :] + ic[..., None, :], -jnp.inf)
        inter = m[..., None] + bcum
        m_t = jnp.maximum(inter, jnp.max(d, axis=-1))
        w_intra = jnp.exp(d - m_t[..., None])
        w_inter = jnp.exp(inter - m_t)
        s = jnp.einsum('bhtd,bhsd->bhts', qc, kc) * w_intra
        num = jnp.einsum('bhts,bhsv->bhtv', s, vc) + w_inter[..., None] * jnp.einsum('bhvd,bhtd->bhtv', C, qc)
        den = jnp.sum(s, axis=-1) + w_inter * jnp.einsum('bhd,bhtd->bht', n, qc)
        h = num / jnp.maximum(jnp.abs(den), jnp.exp(-m_t))[..., None]
        b_last = bcum[..., -1]
        d_last = b_last[..., None] - bcum + ic
        m_new = jnp.maximum(m + b_last, jnp.max(d_last, axis=-1))
        wk = jnp.exp(d_last - m_new[..., None])
        decay = jnp.exp(m + b_last - m_new)
        C_new = decay[..., None, None] * C + jnp.einsum('bhs,bhsv,bhsd->bhvd', wk, vc, kc)
        n_new = decay[..., None] * n + jnp.einsum('bhs,bhsd->bhd', wk, kc)
        return (C_new, n_new, m_new), h

    final, hs = lax.scan(step, state, (chunks(q), chunks(k), chunks(v), chunks(log_i), chunks(log_f)))
    h = jnp.moveaxis(hs, 0, 2).reshape(B, H, L, dv)
    return h, final


def mlstm_mixer(xl, xc, w_in, gate_b, conv_w, conv_b, norm_g, w_out, need_ctx):
    HQK = MLSTM_HEADS * MLSTM_QK_DIM
    HV = MLSTM_HEADS * MLSTM_V_DIM

    def proj(xs):
        B, L, _ = xs.shape
        p = xs @ w_in
        qk, v, o, g = jnp.split(p, [2 * HQK, 2 * HQK + HV, 2 * HQK + HV + D_MODEL], axis=-1)
        qk = jax.nn.silu(centred_conv(qk, conv_w, conv_b))
        q, k = jnp.split(qk, 2, axis=-1)

        def heads(a, dh):
            return jnp.transpose(a.reshape(B, L, MLSTM_HEADS, dh), (0, 2, 1, 3)).astype(jnp.float32)

        q = heads(q, MLSTM_QK_DIM)
        k = heads(k, MLSTM_QK_DIM) * (MLSTM_QK_DIM ** -0.5)
        v = heads(v, MLSTM_V_DIM)
        g = jnp.transpose((g.astype(jnp.float32) + gate_b).reshape(B, L, 4, MLSTM_HEADS), (2, 0, 3, 1))
        return q, k, v, g, o

    def zero_state(B):
        return (jnp.zeros((B, MLSTM_HEADS, MLSTM_V_DIM, MLSTM_QK_DIM), jnp.float32),
                jnp.zeros((B, MLSTM_HEADS, MLSTM_QK_DIM), jnp.float32),
                jnp.zeros((B, MLSTM_HEADS), jnp.float32))

    def flip(a):
        return jnp.flip(a, axis=2)

    def readout(h, o, dtype):
        B, H, L, dv = h.shape
        mu = jnp.mean(h, axis=-1, keepdims=True)
        var = jnp.mean(jnp.square(h - mu), axis=-1, keepdims=True)
        hn = (h - mu) * lax.rsqrt(var + LN_EPS)
        hn = jnp.transpose(hn, (0, 2, 1, 3)).reshape(B, L, D_MODEL) * norm_g
        return (hn * jax.nn.sigmoid(o.astype(jnp.float32))).astype(dtype) @ w_out

    ql, kl, vl, gl, ol = proj(xl)
    qc, kc, vc, gc, oc = proj(xc)
    B = xl.shape[0]
    hc_f, st_f = mlstm_scan(qc, kc, vc, gc[0], jax.nn.log_sigmoid(gc[1]), zero_state(B))
    hl_f, _ = mlstm_scan(ql, kl, vl, gl[0], jax.nn.log_sigmoid(gl[1]), st_f)
    hc_b, st_b = mlstm_scan(flip(qc), flip(kc), flip(vc), flip(gc[2]), jax.nn.log_sigmoid(flip(gc[3])), zero_state(B))
    hl_b, _ = mlstm_scan(flip(ql), flip(kl), flip(vl), flip(gl[2]), jax.nn.log_sigmoid(flip(gl[3])), st_b)
    o_lat = readout(hl_f + flip(hl_b), ol, xl.dtype)
    o_ctx = readout(hc_f + flip(hc_b), oc, xc.dtype) if need_ctx else None
    return o_lat, o_ctx


def setup_inputs(seed: int = 0) -> dict:
    key = jax.random.key(seed)
    ks = jax.random.split(key, 24)
    D = D_MODEL
    HQK = MLSTM_HEADS * MLSTM_QK_DIM
    HV = MLSTM_HEADS * MLSTM_V_DIM
    att_cols = (ATT_HEADS + 2 * ATT_KV_HEADS) * ATT_HEAD_DIM
    ml_cols = 2 * HQK + HV + D + 4 * MLSTM_HEADS

    def nrm(k, shape, scale):
        return jax.random.normal(k, shape, jnp.float32) * scale

    forget_b = jnp.linspace(3.0, 6.0, MLSTM_HEADS, dtype=jnp.float32)
    gnoise = nrm(ks[15], (N_MLSTM_LAYERS, 4, MLSTM_HEADS), 0.1)
    gate_b = gnoise + jnp.stack([jnp.zeros_like(forget_b), forget_b, jnp.zeros_like(forget_b), forget_b], axis=0)[None]
    return {
        "x": nrm(ks[0], (BATCH, SEQ, D), 1.0),
        "c": nrm(ks[1], (BATCH, D), 1.0),
        "ctx": nrm(ks[2], (BATCH, CTX_LEN, D), 1.0),
        "c_ctx": nrm(ks[3], (D,), 1.0),
        "ada_w": nrm(ks[4], (DEPTH, D, N_MOD * D), 0.5 * D ** -0.5),
        "ada_b": nrm(ks[5], (DEPTH, N_MOD * D), 0.02),
        "ln_g": 1.0 + nrm(ks[6], (DEPTH, 3, D), 0.02),
        "ln_b": nrm(ks[7], (DEPTH, 3, D), 0.02),
        "ffn_w_in": nrm(ks[8], (DEPTH, 2, D, 2 * D_FF), D ** -0.5),
        "ffn_w_out": nrm(ks[9], (DEPTH, 2, D_FF, D), DEEPNORM_BETA * D_FF ** -0.5),
        "att_w_in": nrm(ks[10], (N_ATT_LAYERS, D, att_cols), D ** -0.5),
        "att_q_gain": 1.0 + nrm(ks[11], (N_ATT_LAYERS, ATT_HEAD_DIM), 0.02),
        "att_k_gain": 1.0 + nrm(ks[12], (N_ATT_LAYERS, ATT_HEAD_DIM), 0.02),
        "att_w_out": nrm(ks[13], (N_ATT_LAYERS, D, D), DEEPNORM_BETA * D ** -0.5),
        "ml_w_in": nrm(ks[14], (N_MLSTM_LAYERS, D, ml_cols), D ** -0.5),
        "ml_gate_b": gate_b.reshape(N_MLSTM_LAYERS, 4 * MLSTM_HEADS),
        "ml_conv_w": nrm(ks[16], (N_MLSTM_LAYERS, MLSTM_CONV_W, 2 * HQK), MLSTM_CONV_W ** -0.5),
        "ml_conv_b": nrm(ks[17], (N_MLSTM_LAYERS, 2 * HQK), 0.02),
        "ml_norm_g": 1.0 + nrm(ks[18], (N_MLSTM_LAYERS, D), 0.02),
        "ml_w_out": nrm(ks[19], (N_MLSTM_LAYERS, D, D), DEEPNORM_BETA * D ** -0.5),
    }


def reference(x, c, ctx, c_ctx, ada_w, ada_b, ln_g, ln_b, ffn_w_in, ffn_w_out,
              att_w_in, att_q_gain, att_k_gain, att_w_out,
              ml_w_in, ml_gate_b, ml_conv_w, ml_conv_b, ml_norm_g, ml_w_out):
    n_lat = x.shape[1]
    cos, sin = axial_rope_tables(n_lat)
    h_lat, h_ctx = x, ctx
    for i in range(DEPTH):
        need_ctx = i < DEPTH - 1
        mod_lat = jnp.split((jax.nn.silu(c) @ ada_w[i] + ada_b[i])[:, None, :], N_MOD, axis=-1)
        mod_ctx = jnp.split((jax.nn.silu(c_ctx) @ ada_w[i] + ada_b[i])[None, None, :], N_MOD, axis=-1)
        h_lat = half_ffn(h_lat, mod_lat[0], mod_lat[1], mod_lat[2], ffn_w_in[i, 0], ffn_w_out[i, 0], ln_g[i, 0], ln_b[i, 0])
        h_ctx = half_ffn(h_ctx, mod_ctx[0], mod_ctx[1], mod_ctx[2], ffn_w_in[i, 0], ffn_w_out[i, 0], ln_g[i, 0], ln_b[i, 0])
        xm_lat = modulate(h_lat, mod_lat[3], mod_lat[4])
        xm_ctx = modulate(h_ctx, mod_ctx[3], mod_ctx[4])
        j = i // N_MIXERS
        if i % N_MIXERS == 0:
            o_lat, o_ctx = attention_mixer(xm_lat, xm_ctx, cos, sin, att_w_in[j], att_q_gain[j], att_k_gain[j],
                                           att_w_out[j], need_ctx)
        else:
            o_lat, o_ctx = mlstm_mixer(xm_lat, xm_ctx, ml_w_in[j], ml_gate_b[j], ml_conv_w[j], ml_conv_b[j],
                                       ml_norm_g[j], ml_w_out[j], need_ctx)
        h_lat = post_norm(h_lat, mod_lat[5] * o_lat, ln_g[i, 1], ln_b[i, 1])
        h_lat = half_ffn(h_lat, mod_lat[6], mod_lat[7], mod_lat[8], ffn_w_in[i, 1], ffn_w_out[i, 1], ln_g[i, 2], ln_b[i, 2])
        if need_ctx:
            h_ctx = post_norm(h_ctx, mod_ctx[5] * o_ctx, ln_g[i, 1], ln_b[i, 1])
            h_ctx = half_ffn(h_ctx, mod_ctx[6], mod_ctx[7], mod_ctx[8], ffn_w_in[i, 1], ffn_w_out[i, 1], ln_g[i, 2], ln_b[i, 2])
    return h_lat
```

```python
import functools

import jax
import jax.numpy as jnp
from jax import lax
from jax.experimental import pallas as pl
from jax.experimental.pallas import tpu as pltpu

F32 = jnp.float32
BF16 = jnp.bfloat16

N_MOD = 9
GRID_W = 64
ATT_HEADS = 8
ATT_KV_HEADS = 2
ATT_GROUP = ATT_HEADS // ATT_KV_HEADS
ATT_HEAD_DIM = 128
ROPE_PAIRS = ATT_HEAD_DIM // 4
ROPE_THETA = 10000.0
ML_HEADS = 4
ML_QK = 128
ML_V = 256
ML_UNITS = 2 * ML_HEADS
LN_EPS = 1e-5
RMS_EPS = 1e-6

TOKEN_TILE = 256
FF_CHUNK = 256
SCAN_CHUNK = 128
HALO = 8
VMEM_LIMIT = 56 * 1024 * 1024


def _params(*sem):
    return pltpu.CompilerParams(dimension_semantics=sem, vmem_limit_bytes=VMEM_LIMIT)


def _resident(shape):
    n = len(shape)
    return pl.BlockSpec(shape, lambda *_: (0,) * n, pipeline_mode=pl.Buffered(1))


def _layer_norm(x, g, b):
    mu = jnp.mean(x, axis=-1, keepdims=True)
    xc = x - mu
    var = jnp.mean(xc * xc, axis=-1, keepdims=True)
    return xc * lax.rsqrt(var + LN_EPS) * g + b


def _silu(x):
    return x * jax.nn.sigmoid(x)


def _modulated(h, mod_ref, k0):
    return h * (1.0 + mod_ref[k0 + 1:k0 + 2, :]) + mod_ref[k0:k0 + 1, :]


def _ada_kernel(c_ref, w_ref, b_ref, o_ref):
    s = _silu(c_ref[...]).astype(BF16)
    o_ref[...] = jnp.dot(s, w_ref[...].astype(BF16), preferred_element_type=F32) + b_ref[...]


def _ada(cc, ada_w, ada_b):
    depth, d, nd = ada_w.shape
    rows = cc.shape[0]
    return pl.pallas_call(
        _ada_kernel,
        grid=(depth, nd // d),
        in_specs=[
            pl.BlockSpec((rows, d), lambda i, j: (0, 0)),
            pl.BlockSpec((None, d, d), lambda i, j: (i, 0, j)),
            pl.BlockSpec((None, 1, d), lambda i, j: (i, 0, j)),
        ],
        out_specs=pl.BlockSpec((None, rows, d), lambda i, j: (i, 0, j)),
        out_shape=jax.ShapeDtypeStruct((depth, rows, nd), F32),
        compiler_params=_params("parallel", "parallel"),
        name="ada_mod",
    )(cc, ada_w, ada_b.reshape(depth, 1, nd))


def _ffn_kernel(h_ref, mod_ref, win_ref, wout_ref, g_ref, b_ref, o_ref, *, k0, alpha, n_chunks):
    h = h_ref[...]
    xm = _modulated(h, mod_ref, k0).astype(BF16)
    acc = None
    for j in range(n_chunks):
        au = jnp.dot(xm, win_ref[:, 2 * FF_CHUNK * j:2 * FF_CHUNK * (j + 1)], preferred_element_type=F32)
        hid = (_silu(au[:, :FF_CHUNK]) * au[:, FF_CHUNK:]).astype(BF16)
        y = jnp.dot(hid, wout_ref[FF_CHUNK * j:FF_CHUNK * (j + 1), :], preferred_element_type=F32)
        acc = y if acc is None else acc + y
    gate = mod_ref[k0 + 2:k0 + 3, :]
    o_ref[...] = _layer_norm(alpha * h + (0.5 * gate) * acc, g_ref[...], b_ref[...])


def _ffn(h, mods, mod_row, n_tiles, win, wout, g, b, k0, alpha):
    bsz, _, d = h.shape
    dff = wout.shape[0]
    tm = TOKEN_TILE
    kern = functools.partial(_ffn_kernel, k0=k0, alpha=alpha, n_chunks=dff // FF_CHUNK)
    return pl.pallas_call(
        kern,
        grid=(bsz, n_tiles),
        in_specs=[
            pl.BlockSpec((None, tm, d), lambda i, t: (i, t, 0)),
            pl.BlockSpec((None, N_MOD, d), lambda i, t: (mod_row(i, t), 0, 0)),
            _resident((d, 2 * dff)),
            _resident((dff, d)),
            _resident((1, d)),
            _resident((1, d)),
        ],
        out_specs=pl.BlockSpec((None, tm, d), lambda i, t: (i, t, 0)),
        out_shape=jax.ShapeDtypeStruct((bsz, n_tiles * tm, d), F32),
        compiler_params=_params("parallel", "parallel"),
        name="half_ffn",
    )(h, mods, win, wout, g, b)


def _qkv_kernel(h_ref, mod_ref, w_ref, qg_ref, kg_ref, ct_ref, st_ref, q_ref, k_ref, v_ref):
    xm = _modulated(h_ref[...], mod_ref, 3).astype(BF16)
    p = jnp.dot(xm, w_ref[...], preferred_element_type=F32)
    ct = ct_ref[...]
    st = st_ref[...]
    dh = ATT_HEAD_DIM
    lane = lax.broadcasted_iota(jnp.int32, ct.shape, 1)
    first_half = (lane % (2 * ROPE_PAIRS)) < ROPE_PAIRS

    def norm_rope(x, gain):
        xn = x * lax.rsqrt(jnp.mean(x * x, axis=-1, keepdims=True) + RMS_EPS) * gain
        partner = jnp.where(first_half, pltpu.roll(xn, dh - ROPE_PAIRS, 1), pltpu.roll(xn, ROPE_PAIRS, 1))
        return xn * ct + partner * st

    qg = qg_ref[...]
    kg = kg_ref[...]
    for hd in range(ATT_HEADS):
        q = norm_rope(p[:, hd * dh:(hd + 1) * dh], qg) * (dh ** -0.5)
        q_ref[:, hd * dh:(hd + 1) * dh] = q.astype(BF16)
    k0 = ATT_HEADS * dh
    v0 = k0 + ATT_KV_HEADS * dh
    for hd in range(ATT_KV_HEADS):
        k = norm_rope(p[:, k0 + hd * dh:k0 + (hd + 1) * dh], kg)
        k_ref[:, hd * dh:(hd + 1) * dh] = k.astype(BF16)
    v_ref[...] = p[:, v0:].astype(BF16)


def _qkv(h, mods, mod_row, w, qg, kg, ctab, stab):
    bsz, t_all, d = h.shape
    tm = TOKEN_TILE
    dh = ATT_HEAD_DIM
    nq, nk = ATT_HEADS * dh, ATT_KV_HEADS * dh
    tok = lambda width: pl.BlockSpec((None, tm, width), lambda i, t: (i, t, 0))
    return pl.pallas_call(
        _qkv_kernel,
        grid=(bsz, t_all // tm),
        in_specs=[
            tok(d),
            pl.BlockSpec((None, N_MOD, d), lambda i, t: (mod_row(i, t), 0, 0)),
            _resident(w.shape),
            _resident((1, dh)),
            _resident((1, dh)),
            pl.BlockSpec((tm, dh), lambda i, t: (t, 0)),
            pl.BlockSpec((tm, dh), lambda i, t: (t, 0)),
        ],
        out_specs=[tok(nq), tok(nk), tok(nk)],
        out_shape=[
            jax.ShapeDtypeStruct((bsz, t_all, nq), BF16),
            jax.ShapeDtypeStruct((bsz, t_all, nk), BF16),
            jax.ShapeDtypeStruct((bsz, t_all, nk), BF16),
        ],
        compiler_params=_params("parallel", "parallel"),
        name="att_qkv",
    )(h, mods, w, qg, kg, ctab, stab)


def _attn_kernel(q_ref, k_ref, v_ref, o_ref, *, n_lat_tiles, n_lat):
    dh = ATT_HEAD_DIM

    def run(k, v):
        for g in range(ATT_GROUP):
            q = q_ref[:, g * dh:(g + 1) * dh]
            s = lax.dot_general(q, k, (((1,), (1,)), ((), ())), preferred_element_type=F32)
            p = jnp.exp(s - jnp.max(s, axis=-1, keepdims=True))
            denom = jnp.sum(p, axis=-1, keepdims=True)
            o = jnp.dot(p.astype(BF16), v, preferred_element_type=F32)
            o_ref[:, g * dh:(g + 1) * dh] = (o / denom).astype(BF16)

    t = pl.program_id(2)

    @pl.when(t < n_lat_tiles)
    def _():
        run(k_ref[...], v_ref[...])

    @pl.when(t >= n_lat_tiles)
    def _():
        run(k_ref[n_lat:, :], v_ref[n_lat:, :])


def _attention(q, k, v, n_lat):
    bsz, t_all, nq = q.shape
    tm = TOKEN_TILE
    dh = ATT_HEAD_DIM
    gw = ATT_GROUP * dh
    kern = functools.partial(_attn_kernel, n_lat_tiles=n_lat // tm, n_lat=n_lat)
    return pl.pallas_call(
        kern,
        grid=(bsz, ATT_KV_HEADS, t_all // tm),
        in_specs=[
            pl.BlockSpec((None, tm, gw), lambda i, j, t: (i, t, j)),
            pl.BlockSpec((None, t_all, dh), lambda i, j, t: (i, 0, j)),
            pl.BlockSpec((None, t_all, dh), lambda i, j, t: (i, 0, j)),
        ],
        out_specs=pl.BlockSpec((None, tm, gw), lambda i, j, t: (i, t, j)),
        out_shape=jax.ShapeDtypeStruct((bsz, t_all, nq), BF16),
        compiler_params=_params("parallel", "parallel", "parallel"),
        name="att_core",
    )(q, k, v)


def _post_kernel(o_ref, h_ref, mod_ref, w_ref, g_ref, b_ref, out_ref, *, alpha):
    y = jnp.dot(o_ref[...], w_ref[...], preferred_element_type=F32)
    out_ref[...] = _layer_norm(alpha * h_ref[...] + mod_ref[5:6, :] * y, g_ref[...], b_ref[...])


def _post(o, h, mods, mod_row, w, g, b, alpha):
    bsz, t_all, d = h.shape
    tm = TOKEN_TILE
    tok = pl.BlockSpec((None, tm, d), lambda i, t: (i, t, 0))
    return pl.pallas_call(
        functools.partial(_post_kernel, alpha=alpha),
        grid=(bsz, t_all // tm),
        in_specs=[
            tok, tok,
            pl.BlockSpec((None, N_MOD, d), lambda i, t: (mod_row(i, t), 0, 0)),
            _resident(w.shape), _resident((1, d)), _resident((1, d)),
        ],
        out_specs=tok,
        out_shape=jax.ShapeDtypeStruct((bsz, t_all, d), F32),
        compiler_params=_params("parallel", "parallel"),
        name="att_post",
    )(o, h, mods, w, g, b)


def _mlproj_kernel(h_ref, hp_ref, hn_ref, mod_ref, wqk_ref, wv_ref, wo_ref, wg_ref, gb_ref, cw_ref, cb_ref,
                   q_ref, k_ref, v_ref, sig_ref, gc_ref, at_ref, xm_scr, *, n_lat_tiles, n_tiles):
    tm = TOKEN_TILE
    lc = SCAN_CHUNK
    t = pl.program_id(1)
    xm = _modulated(h_ref[...], mod_ref, 3).astype(BF16)
    xm_scr[0:tm, :] = xm
    xm_scr[tm:tm + HALO, :] = _modulated(hp_ref[...], mod_ref, 3).astype(BF16)
    xm_scr[tm + HALO:tm + 2 * HALO, :] = _modulated(hn_ref[...], mod_ref, 3).astype(BF16)
    has_prev = jnp.logical_and(t != 0, t != n_lat_tiles).astype(F32)
    has_next = jnp.logical_and(t != n_lat_tiles - 1, t != n_tiles - 1).astype(F32)

    row = lax.broadcasted_iota(jnp.int32, (tm, FF_CHUNK), 0)
    hq = ML_HEADS * ML_QK
    for c in range(2 * hq // FF_CHUNK):
        cols = slice(c * FF_CHUNK, (c + 1) * FF_CHUNK)
        p = jnp.dot(xm_scr[...], wqk_ref[:, cols], preferred_element_type=F32)
        main = p[0:tm]
        prev_row = p[tm + HALO - 1:tm + HALO] * has_prev
        next_row = p[tm + HALO:tm + HALO + 1] * has_next
        down = jnp.where(row == 0, prev_row, pltpu.roll(main, 1, 0))
        up = jnp.where(row == tm - 1, next_row, pltpu.roll(main, tm - 1, 0))
        conv = down * cw_ref[0:1, cols] + main * cw_ref[1:2, cols] + up * cw_ref[2:3, cols] + cb_ref[:, cols]
        act = _silu(conv)
        if (c + 1) * FF_CHUNK <= hq:
            q_ref[:, cols] = act.astype(BF16)
        else:
            k_ref[:, c * FF_CHUNK - hq:(c + 1) * FF_CHUNK - hq] = (act * (ML_QK ** -0.5)).astype(BF16)

    v_ref[...] = jnp.dot(xm, wv_ref[...], preferred_element_type=F32).astype(BF16)
    sig_ref[...] = jax.nn.sigmoid(jnp.dot(xm, wo_ref[...], preferred_element_type=F32)).astype(BF16)

    g = jnp.dot(xm, wg_ref[...], preferred_element_type=F32) + gb_ref[...]
    logf = jax.nn.log_sigmoid(g)
    r_i = lax.broadcasted_iota(jnp.int32, (lc, lc), 0)
    c_i = lax.broadcasted_iota(jnp.int32, (lc, lc), 1)
    tri_lo = (c_i <= r_i).astype(F32)
    tri_hi = (c_i >= r_i).astype(F32)
    lane = lax.broadcasted_iota(jnp.int32, (lc, 128), 1)
    trow = lax.broadcasted_iota(jnp.int32, (lc, 128), 0)
    nh = ML_HEADS
    for c in range(tm // lc):
        rows = slice(c * lc, (c + 1) * lc)
        lf = logf[rows]
        pre = jnp.dot(tri_lo, lf, preferred_element_type=F32, precision=lax.Precision.HIGHEST)
        suf = jnp.dot(tri_hi, lf, preferred_element_type=F32, precision=lax.Precision.HIGHEST)
        bc = pltpu.roll(jnp.where(lane < 3 * nh, pre, suf), 128 - 2 * nh, 1)
        a = g[rows] - bc
        pf = a
        pb = a
        k = 1
        while k < lc:
            pf = jnp.maximum(pf, jnp.where(trow >= k, pltpu.roll(pf, k, 0), -jnp.inf))
            pb = jnp.maximum(pb, jnp.where(trow < lc - k, pltpu.roll(pb, lc - k, 0), -jnp.inf))
            k *= 2
        pm = jnp.where(lane < nh, pf, pb)
        gc_ref[rows, :] = jnp.where(lane < 2 * nh, bc,
                                    jnp.where(lane < 4 * nh, pltpu.roll(a, 2 * nh, 1), pltpu.roll(pm, 4 * nh, 1)))
        at_ref[c] = jnp.transpose(a)[0:ML_UNITS, :]


def _mlproj(h, mods, mod_row, wqk, wv, wo, wg, gb, cw, cb, n_lat):
    bsz, t_all, d = h.shape
    tm = TOKEN_TILE
    lc = SCAN_CHUNK
    n_tiles = t_all // tm
    n_lat_tiles = n_lat // tm
    hq = ML_HEADS * ML_QK
    blocks_per_tile = tm // HALO
    last_block = t_all // HALO - 1
    tok = lambda width: pl.BlockSpec((None, tm, width), lambda i, t: (i, t, 0))
    kern = functools.partial(_mlproj_kernel, n_lat_tiles=n_lat_tiles, n_tiles=n_tiles)
    return pl.pallas_call(
        kern,
        grid=(bsz, n_tiles),
        in_specs=[
            tok(d),
            pl.BlockSpec((None, HALO, d), lambda i, t: (i, jnp.maximum(t * blocks_per_tile - 1, 0), 0)),
            pl.BlockSpec((None, HALO, d), lambda i, t: (i, jnp.minimum((t + 1) * blocks_per_tile, last_block), 0)),
            pl.BlockSpec((None, N_MOD, d), lambda i, t: (mod_row(i, t), 0, 0)),
            _resident(wqk.shape), _resident(wv.shape), _resident(wo.shape), _resident(wg.shape),
            _resident(gb.shape), _resident(cw.shape), _resident(cb.shape),
        ],
        out_specs=[
            tok(hq), tok(hq), tok(d), tok(d), tok(128),
            pl.BlockSpec((None, tm // lc, ML_UNITS, lc), lambda i, t: (i, t, 0, 0)),
        ],
        out_shape=[
            jax.ShapeDtypeStruct((bsz, t_all, hq), BF16),
            jax.ShapeDtypeStruct((bsz, t_all, hq), BF16),
            jax.ShapeDtypeStruct((bsz, t_all, d), BF16),
            jax.ShapeDtypeStruct((bsz, t_all, d), BF16),
            jax.ShapeDtypeStruct((bsz, t_all, 128), F32),
            jax.ShapeDtypeStruct((bsz, t_all // lc, ML_UNITS, lc), F32),
        ],
        scratch_shapes=[pltpu.VMEM((tm + 2 * HALO, d), BF16)],
        compiler_params=_params("parallel", "parallel"),
        name="ml_proj",
    )(h, h, h, mods, wqk, wv, wo, wg, gb, cw, cb)


def _scan_kernel(q_ref, k_ref, v_ref, gc_ref, at_ref, o_ref, ct_scr, m_scr, *, n_lat_chunks, n_chunks):
    lc = SCAN_CHUNK
    dk, dv = ML_QK, ML_V
    ct_scr[...] = jnp.zeros_like(ct_scr)
    m_scr[...] = jnp.zeros_like(m_scr)
    t_i = lax.broadcasted_iota(jnp.int32, (lc, lc), 0)
    s_i = lax.broadcasted_iota(jnp.int32, (lc, lc), 1)
    masks = (s_i <= t_i, s_i >= t_i)
    ones_col = (lax.broadcasted_iota(jnp.int32, (lc, 128), 1) == 0).astype(BF16)
    n_ctx_chunks = n_chunks - n_lat_chunks

    def step(j, mode):
        chunk = ((j + n_lat_chunks) % n_chunks, n_chunks - 1 - j)
        for direction in range(2):
            c = chunk[direction]
            r0 = pl.multiple_of(c * lc, lc)
            gc = gc_ref[pl.ds(r0, lc), :]
            at = at_ref[c]
            last = lc - 1 if direction == 0 else 0
            for hd in range(ML_HEADS):
                u = direction * ML_HEADS + hd
                qc = q_ref[pl.ds(r0, lc), hd * dk:(hd + 1) * dk]
                kc = k_ref[pl.ds(r0, lc), hd * dk:(hd + 1) * dk]
                vc = v_ref[pl.ds(r0, lc), hd * dv:(hd + 1) * dv]
                v_aug = jnp.concatenate([vc, ones_col], axis=1)
                bcum = gc[:, u:u + 1]
                a_col = gc[:, ML_UNITS + u:ML_UNITS + u + 1]
                pmax = gc[:, 2 * ML_UNITS + u:2 * ML_UNITS + u + 1]
                a_row = at[u:u + 1, :]
                m = m_scr[u, 0:1, 0:1]
                ct = ct_scr[u]
                mu = jnp.maximum(m, pmax)
                if mode != "ctx":
                    w = jnp.where(masks[direction], jnp.exp(a_row - mu), 0.0)
                    s = lax.dot_general(qc, kc, (((1,), (1,)), ((), ())), preferred_element_type=F32)
                    eq = jnp.exp(m - mu) * qc.astype(F32)
                    lhs = jnp.concatenate([(s * w).astype(BF16), eq.astype(BF16)], axis=1)
                    rhs = jnp.concatenate([v_aug, ct.astype(BF16)], axis=0)
                    num = jnp.dot(lhs, rhs, preferred_element_type=F32)
                    den = jnp.maximum(jnp.abs(num[:, dv:dv + 1]), jnp.exp(-(bcum + mu)))
                    hout = num[:, :dv] / den
                    if mode == "store":
                        o_ref[pl.ds(r0, lc), hd * dv:(hd + 1) * dv] = hout
                    else:
                        o_ref[pl.ds(r0, lc), hd * dv:(hd + 1) * dv] += hout
                mu_l = mu[last:last + 1, :]
                wv = (jnp.exp(a_col - mu_l) * v_aug.astype(F32)).astype(BF16)
                upd = lax.dot_general(kc, wv, (((0,), (0,)), ((), ())), preferred_element_type=F32)
                ct_scr[u] = jnp.exp(m - mu_l) * ct + upd
                m_scr[u, 0:1, 0:1] = bcum[last:last + 1, :] + mu_l

    half = n_ctx_chunks + n_lat_chunks // 2
    lax.fori_loop(0, n_ctx_chunks, lambda j, _: step(j, "ctx"), None)
    lax.fori_loop(n_ctx_chunks, half, lambda j, _: step(j, "store"), None)
    lax.fori_loop(half, n_chunks, lambda j, _: step(j, "add"), None)


def _scan(q, k, v, gc, at, n_lat):
    bsz, t_all, d = v.shape
    lc = SCAN_CHUNK
    hq = ML_HEADS * ML_QK
    n_chunks = t_all // lc
    kern = functools.partial(_scan_kernel, n_lat_chunks=n_lat // lc, n_chunks=n_chunks)
    whole = lambda width: pl.BlockSpec((None, t_all, width), lambda i: (i, 0, 0))
    return pl.pallas_call(
        kern,
        grid=(bsz,),
        in_specs=[
            whole(hq), whole(hq), whole(d), whole(128),
            pl.BlockSpec((None, n_chunks, ML_UNITS, lc), lambda i: (i, 0, 0, 0)),
        ],
        out_specs=pl.BlockSpec((None, n_lat, d), lambda i: (i, 0, 0)),
        out_shape=jax.ShapeDtypeStruct((bsz, n_lat, d), F32),
        scratch_shapes=[
            pltpu.VMEM((ML_UNITS, ML_QK, ML_V + 128), F32),
            pltpu.VMEM((ML_UNITS, 8, 128), F32),
        ],
        compiler_params=_params("parallel"),
        name="ml_scan",
    )(q, k, v, gc, at)


def _readout_kernel(hs_ref, sig_ref, h_ref, mod_ref, ng_ref, w_ref, g_ref, b_ref, out_ref, *, alpha):
    dv = ML_V
    parts = []
    for hd in range(ML_HEADS):
        x = hs_ref[:, hd * dv:(hd + 1) * dv]
        mu = jnp.mean(x, axis=-1, keepdims=True)
        xc = x - mu
        var = jnp.mean(xc * xc, axis=-1, keepdims=True)
        parts.append(xc * lax.rsqrt(var + LN_EPS))
    hn = jnp.concatenate(parts, axis=1) * ng_ref[...]
    z = (hn * sig_ref[...].astype(F32)).astype(BF16)
    y = jnp.dot(z, w_ref[...], preferred_element_type=F32)
    out_ref[...] = _layer_norm(alpha * h_ref[...] + mod_ref[5:6, :] * y, g_ref[...], b_ref[...])


def _readout(hs, sig, h, mods, mod_row, ng, w, g, b, alpha):
    bsz, n_lat, d = hs.shape
    tm = TOKEN_TILE
    tok = pl.BlockSpec((None, tm, d), lambda i, t: (i, t, 0))
    return pl.pallas_call(
        functools.partial(_readout_kernel, alpha=alpha),
        grid=(bsz, n_lat // tm),
        in_specs=[
            tok, tok, tok,
            pl.BlockSpec((None, N_MOD, d), lambda i, t: (mod_row(i, t), 0, 0)),
            _resident((1, d)), _resident(w.shape), _resident((1, d)), _resident((1, d)),
        ],
        out_specs=tok,
        out_shape=jax.ShapeDtypeStruct((bsz, n_lat, d), F32),
        compiler_params=_params("parallel", "parallel"),
        name="ml_readout",
    )(hs, sig, h, mods, ng, w, g, b)


def _rope_tables(n_lat, n_all):
    rows = n_lat // GRID_W
    row = jnp.repeat(jnp.arange(rows, dtype=jnp.int32), GRID_W).astype(F32)
    col = jnp.tile(jnp.arange(GRID_W, dtype=jnp.int32), rows).astype(F32)
    inv = ROPE_THETA ** (-jnp.arange(ROPE_PAIRS, dtype=F32) / ROPE_PAIRS)
    ar, ac = row[:, None] * inv, col[:, None] * inv
    cos = jnp.concatenate([jnp.cos(ar), jnp.cos(ar), jnp.cos(ac), jnp.cos(ac)], axis=1)
    sin = jnp.concatenate([-jnp.sin(ar), jnp.sin(ar), -jnp.sin(ac), jnp.sin(ac)], axis=1)
    n_ctx = n_all - n_lat
    cos = jnp.concatenate([cos, jnp.ones((n_ctx, ATT_HEAD_DIM), F32)], axis=0)
    sin = jnp.concatenate([sin, jnp.zeros((n_ctx, ATT_HEAD_DIM), F32)], axis=0)
    return cos, sin


def kernel(x, c, ctx, c_ctx, ada_w, ada_b, ln_g, ln_b, ffn_w_in, ffn_w_out, att_w_in, att_q_gain, att_k_gain,
           att_w_out, ml_w_in, ml_gate_b, ml_conv_w, ml_conv_b, ml_norm_g, ml_w_out):
    bsz, n_lat, d = x.shape
    n_ctx = ctx.shape[1]
    n_all = n_lat + n_ctx
    depth = ada_w.shape[0]
    dff = ffn_w_out.shape[2]
    tm = TOKEN_TILE
    assert depth == 2, "layer 0 attention, layer 1 (last) mLSTM"
    assert n_lat % tm == 0 and n_ctx % tm == 0 and dff % FF_CHUNK == 0 and tm % SCAN_CHUNK == 0
    assert (n_lat // SCAN_CHUNK) % 2 == 0
    n_lat_tiles = n_lat // tm
    n_all_tiles = n_all // tm
    alpha = (2.0 * depth) ** 0.25

    mod_rows = -(-(bsz + 1) // 8) * 8
    cc = jnp.concatenate([c, c_ctx[None, :], jnp.zeros((mod_rows - bsz - 1, d), F32)], axis=0)
    mods = _ada(cc, ada_w, ada_b).reshape(depth, mod_rows, N_MOD, d)
    mod_row = lambda i, t: jnp.where(t < n_lat_tiles, i, bsz)

    n_ff = dff // FF_CHUNK
    win = ffn_w_in.reshape(depth, 2, d, 2, n_ff, FF_CHUNK).transpose(0, 1, 2, 4, 3, 5)
    win = win.reshape(depth, 2, d, 2 * dff).astype(BF16)
    wout = ffn_w_out.astype(BF16)
    row2 = lambda a: a.reshape(1, -1)

    h = jnp.concatenate([x, ctx], axis=1)
    cos, sin = _rope_tables(n_lat, n_all)

    m_i = mods[0]
    h = _ffn(h, m_i, mod_row, n_all_tiles, win[0, 0], wout[0, 0], row2(ln_g[0, 0]), row2(ln_b[0, 0]), 0, alpha)
    q, k, v = _qkv(h, m_i, mod_row, att_w_in[0].astype(BF16), row2(att_q_gain[0]), row2(att_k_gain[0]), cos, sin)
    o = _attention(q, k, v, n_lat)
    h = _post(o, h, m_i, mod_row, att_w_out[0].astype(BF16), row2(ln_g[0, 1]), row2(ln_b[0, 1]), alpha)
    h = _ffn(h, m_i, mod_row, n_all_tiles, win[0, 1], wout[0, 1], row2(ln_g[0, 2]), row2(ln_b[0, 2]), 6, alpha)

    m_i = mods[1]
    h = _ffn(h, m_i, mod_row, n_all_tiles, win[1, 0], wout[1, 0], row2(ln_g[1, 0]), row2(ln_b[1, 0]), 0, alpha)
    hq = ML_HEADS * ML_QK
    w = ml_w_in[0]
    wqk = w[:, :2 * hq].astype(BF16)
    wv = w[:, 2 * hq:2 * hq + d].astype(BF16)
    wo = w[:, 2 * hq + d:2 * hq + 2 * d].astype(BF16)
    perm = jnp.array([0, 2, 1, 3])
    wg = w[:, 2 * hq + 2 * d:].reshape(d, 4, ML_HEADS)[:, perm].reshape(d, 4 * ML_HEADS)
    wg = jnp.pad(wg, ((0, 0), (0, 128 - 4 * ML_HEADS))).astype(BF16)
    gb = ml_gate_b[0].reshape(4, ML_HEADS)[perm].reshape(1, 4 * ML_HEADS)
    gb = jnp.pad(gb, ((0, 0), (0, 128 - 4 * ML_HEADS)))
    q, k, v, sig, gc, at = _mlproj(h, m_i, mod_row, wqk, wv, wo, wg, gb, ml_conv_w[0], row2(ml_conv_b[0]), n_lat)
    hs = _scan(q, k, v, gc, at, n_lat)
    h = _readout(hs, sig, h, m_i, mod_row, row2(ml_norm_g[0]), ml_w_out[0].astype(BF16),
                 row2(ln_g[1, 1]), row2(ln_b[1, 1]), alpha)
    return _ffn(h, m_i, mod_row, n_lat_tiles, win[1, 1], wout[1, 1], row2(ln_g[1, 2]), row2(ln_b[1, 2]), 6, alpha)
```

```python
import functools

import jax
import jax.numpy as jnp
from jax import lax
from jax.experimental import pallas as pl
from jax.experimental.pallas import tpu as pltpu

F32 = jnp.float32
BF16 = jnp.bfloat16

N_MOD = 9
GRID_W = 64
ATT_HEADS = 8
ATT_KV_HEADS = 2
ATT_GROUP = ATT_HEADS // ATT_KV_HEADS
ATT_HEAD_DIM = 128
ROPE_PAIRS = ATT_HEAD_DIM // 4
ROPE_THETA = 10000.0
ML_HEADS = 4
ML_QK = 128
ML_V = 256
ML_UNITS = 2 * ML_HEADS
LN_EPS = 1e-5
RMS_EPS = 1e-6

TOKEN_TILE = 512
ATT_Q_TILE = 256
FF_CHUNK = 256
SCAN_CHUNK = 128
HALO = 8
VMEM_LIMIT = 56 * 1024 * 1024


def _params(*sem):
    return pltpu.CompilerParams(dimension_semantics=sem, vmem_limit_bytes=VMEM_LIMIT)


def _resident(shape):
    n = len(shape)
    return pl.BlockSpec(shape, lambda *_: (0,) * n, pipeline_mode=pl.Buffered(1))


def _layer_norm(x, g, b):
    mu = jnp.mean(x, axis=-1, keepdims=True)
    xc = x - mu
    var = jnp.mean(xc * xc, axis=-1, keepdims=True)
    return xc * lax.rsqrt(var + LN_EPS) * g + b


def _silu(x):
    return x * jax.nn.sigmoid(x)


def _modulated(h, mod_ref, k0):
    return h * (1.0 + mod_ref[k0 + 1:k0 + 2, :]) + mod_ref[k0:k0 + 1, :]


class _Tokens:
    def __init__(self, bsz, n_lat, n_ctx):
        tm = TOKEN_TILE
        assert n_lat % tm == 0 and (bsz * n_ctx) % tm == 0 and tm % n_ctx == 0
        self.bsz, self.n_lat, self.n_ctx = bsz, n_lat, n_ctx
        self.lat_rows = bsz * n_lat
        self.rows = bsz * (n_lat + n_ctx)
        self.tiles_per_seq = n_lat // tm
        self.lat_tiles = self.lat_rows // tm
        self.tiles = self.rows // tm

    def is_lat(self, t):
        return t < self.lat_tiles

    def mod_row(self, t):
        return jnp.where(t < self.lat_tiles, t // self.tiles_per_seq, self.bsz)

    def mod_spec(self, d):
        return pl.BlockSpec((None, N_MOD, d), lambda t: (self.mod_row(t), 0, 0))


def _tok_spec(width):
    return pl.BlockSpec((TOKEN_TILE, width), lambda t: (t, 0))


def _ada_kernel(c_ref, w_ref, b_ref, o_ref):
    s = _silu(c_ref[...]).astype(BF16)
    o_ref[...] = jnp.dot(s, w_ref[...].astype(BF16), preferred_element_type=F32) + b_ref[...]


def _ada(cc, ada_w, ada_b):
    depth, d, nd = ada_w.shape
    rows = cc.shape[0]
    return pl.pallas_call(
        _ada_kernel,
        grid=(depth, nd // d),
        in_specs=[
            pl.BlockSpec((rows, d), lambda i, j: (0, 0)),
            pl.BlockSpec((None, d, d), lambda i, j: (i, 0, j)),
            pl.BlockSpec((None, 1, d), lambda i, j: (i, 0, j)),
        ],
        out_specs=pl.BlockSpec((None, rows, d), lambda i, j: (i, 0, j)),
        out_shape=jax.ShapeDtypeStruct((depth, rows, nd), F32),
        compiler_params=_params("parallel", "parallel"),
        name="ada_mod",
    )(cc, ada_w, ada_b.reshape(depth, 1, nd))


def _ffn_kernel(*refs, k0, alpha, dff, lat_tiles, split_input):
    if split_input:
        hx_ref, hc_ref, mod_ref, win_ref, wout_ref, g_ref, b_ref, o_ref = refs
        h = jnp.where(pl.program_id(0) < lat_tiles, hx_ref[...], hc_ref[...])
    else:
        h_ref, mod_ref, win_ref, wout_ref, g_ref, b_ref, o_ref = refs
        h = h_ref[...]
    xm = _modulated(h, mod_ref, k0).astype(BF16)
    acc = None
    for j in range(dff // FF_CHUNK):
        cols = slice(FF_CHUNK * j, FF_CHUNK * (j + 1))
        a = jnp.dot(xm, win_ref[:, cols], preferred_element_type=F32)
        u = jnp.dot(xm, win_ref[:, dff + FF_CHUNK * j:dff + FF_CHUNK * (j + 1)], preferred_element_type=F32)
        hid = (_silu(a) * u).astype(BF16)
        y = jnp.dot(hid, wout_ref[cols, :], preferred_element_type=F32)
        acc = y if acc is None else acc + y
    gate = mod_ref[k0 + 2:k0 + 3, :]
    o_ref[...] = _layer_norm(alpha * h + (0.5 * gate) * acc, g_ref[...], b_ref[...])


def _ffn(tk, hs, mods, n_tiles, win, wout, g, b, k0, alpha):
    split = isinstance(hs, tuple)
    d, dff = wout.shape[1], wout.shape[0]
    tm = TOKEN_TILE
    if split:
        last_lat = tk.lat_tiles - 1
        h_specs = [pl.BlockSpec((tm, d), lambda t: (jnp.minimum(t, last_lat), 0)),
                   pl.BlockSpec((tm, d), lambda t: (jnp.maximum(t - tk.lat_tiles, 0), 0))]
        hs = list(hs)
    else:
        h_specs = [_tok_spec(d)]
        hs = [hs]
    kern = functools.partial(_ffn_kernel, k0=k0, alpha=alpha, dff=dff, lat_tiles=tk.lat_tiles, split_input=split)
    return pl.pallas_call(
        kern,
        grid=(n_tiles,),
        in_specs=h_specs + [tk.mod_spec(d), _resident(win.shape), _resident(wout.shape),
                            _resident((1, d)), _resident((1, d))],
        out_specs=_tok_spec(d),
        out_shape=jax.ShapeDtypeStruct((n_tiles * tm, d), F32),
        compiler_params=_params("parallel"),
        name="half_ffn",
    )(*hs, mods, win, wout, g, b)


def _qkv_kernel(h_ref, mod_ref, w_ref, qg_ref, kg_ref, ct_ref, st_ref, q_ref, k_ref, v_ref):
    xm = _modulated(h_ref[...], mod_ref, 3).astype(BF16)
    p = jnp.dot(xm, w_ref[...], preferred_element_type=F32)
    ct = ct_ref[...]
    st = st_ref[...]
    dh = ATT_HEAD_DIM
    lane = lax.broadcasted_iota(jnp.int32, ct.shape, 1)
    first_half = (lane % (2 * ROPE_PAIRS)) < ROPE_PAIRS

    def norm_rope(x, gain):
        xn = x * lax.rsqrt(jnp.mean(x * x, axis=-1, keepdims=True) + RMS_EPS) * gain
        partner = jnp.where(first_half, pltpu.roll(xn, dh - ROPE_PAIRS, 1), pltpu.roll(xn, ROPE_PAIRS, 1))
        return xn * ct + partner * st

    qg = qg_ref[...]
    kg = kg_ref[...]
    for hd in range(ATT_HEADS):
        q = norm_rope(p[:, hd * dh:(hd + 1) * dh], qg) * (dh ** -0.5)
        q_ref[:, hd * dh:(hd + 1) * dh] = q.astype(BF16)
    k0 = ATT_HEADS * dh
    v0 = k0 + ATT_KV_HEADS * dh
    for hd in range(ATT_KV_HEADS):
        k = norm_rope(p[:, k0 + hd * dh:k0 + (hd + 1) * dh], kg)
        k_ref[:, hd * dh:(hd + 1) * dh] = k.astype(BF16)
    v_ref[...] = p[:, v0:].astype(BF16)


def _qkv(tk, h, mods, w, qg, kg, ctab, stab):
    d = h.shape[1]
    tm = TOKEN_TILE
    dh = ATT_HEAD_DIM
    nq, nk = ATT_HEADS * dh, ATT_KV_HEADS * dh
    tab_spec = pl.BlockSpec((tm, dh), lambda t: (jnp.where(tk.is_lat(t), t % tk.tiles_per_seq, tk.tiles_per_seq), 0))
    return pl.pallas_call(
        _qkv_kernel,
        grid=(tk.tiles,),
        in_specs=[_tok_spec(d), tk.mod_spec(d), _resident(w.shape), _resident((1, dh)), _resident((1, dh)),
                  tab_spec, tab_spec],
        out_specs=[_tok_spec(nq), _tok_spec(nk), _tok_spec(nk)],
        out_shape=[jax.ShapeDtypeStruct((tk.rows, nq), BF16),
                   jax.ShapeDtypeStruct((tk.rows, nk), BF16),
                   jax.ShapeDtypeStruct((tk.rows, nk), BF16)],
        compiler_params=_params("parallel"),
        name="att_qkv",
    )(h, mods, w, qg, kg, ctab, stab)


def _attn_kernel(q_ref, kl_ref, kc_ref, vl_ref, vc_ref, o_ref, *, lat_q_tiles):
    dh = ATT_HEAD_DIM
    nt = (((1,), (1,)), ((), ()))

    def finish(g, o, denom):
        o_ref[:, g * dh:(g + 1) * dh] = (o / denom).astype(BF16)

    t = pl.program_id(2)

    @pl.when(t < lat_q_tiles)
    def _():
        for g in range(ATT_GROUP):
            q = q_ref[:, g * dh:(g + 1) * dh]
            sl = lax.dot_general(q, kl_ref[...], nt, preferred_element_type=F32)
            sc = lax.dot_general(q, kc_ref[...], nt, preferred_element_type=F32)
            m = jnp.maximum(jnp.max(sl, axis=-1, keepdims=True), jnp.max(sc, axis=-1, keepdims=True))
            pl_ = jnp.exp(sl - m)
            pc = jnp.exp(sc - m)
            denom = jnp.sum(pl_, axis=-1, keepdims=True) + jnp.sum(pc, axis=-1, keepdims=True)
            o = (jnp.dot(pl_.astype(BF16), vl_ref[...], preferred_element_type=F32)
                 + jnp.dot(pc.astype(BF16), vc_ref[...], preferred_element_type=F32))
            finish(g, o, denom)

    @pl.when(t >= lat_q_tiles)
    def _():
        for g in range(ATT_GROUP):
            q = q_ref[:, g * dh:(g + 1) * dh]
            sc = lax.dot_general(q, kc_ref[...], nt, preferred_element_type=F32)
            pc = jnp.exp(sc - jnp.max(sc, axis=-1, keepdims=True))
            finish(g, jnp.dot(pc.astype(BF16), vc_ref[...], preferred_element_type=F32),
                   jnp.sum(pc, axis=-1, keepdims=True))


def _attention(tk, q, k, v):
    tq = ATT_Q_TILE
    dh = ATT_HEAD_DIM
    gw = ATT_GROUP * dh
    n_lat, n_ctx = tk.n_lat, tk.n_ctx
    assert n_lat % tq == 0 and n_ctx % tq == 0
    lat_q_tiles = n_lat // tq
    q_tiles = lat_q_tiles + n_ctx // tq
    ctx_block0 = tk.lat_rows // n_ctx

    def q_row(i, t):
        return jnp.where(t < lat_q_tiles, i * lat_q_tiles + t,
                         tk.lat_rows // tq + i * (n_ctx // tq) + (t - lat_q_tiles))

    q_spec = pl.BlockSpec((tq, gw), lambda i, j, t: (q_row(i, t), j))
    lat_spec = pl.BlockSpec((n_lat, dh), lambda i, j, t: (i, j))
    ctx_spec = pl.BlockSpec((n_ctx, dh), lambda i, j, t: (ctx_block0 + i, j))
    return pl.pallas_call(
        functools.partial(_attn_kernel, lat_q_tiles=lat_q_tiles),
        grid=(tk.bsz, ATT_KV_HEADS, q_tiles),
        in_specs=[q_spec, lat_spec, ctx_spec, lat_spec, ctx_spec],
        out_specs=q_spec,
        out_shape=jax.ShapeDtypeStruct(q.shape, BF16),
        compiler_params=_params("parallel", "parallel", "parallel"),
        name="att_core",
    )(q, k, k, v, v)


def _post_kernel(o_ref, h_ref, mod_ref, w_ref, g_ref, b_ref, out_ref, *, alpha):
    y = jnp.dot(o_ref[...], w_ref[...], preferred_element_type=F32)
    out_ref[...] = _layer_norm(alpha * h_ref[...] + mod_ref[5:6, :] * y, g_ref[...], b_ref[...])


def _post(tk, o, h, mods, w, g, b, alpha):
    d = h.shape[1]
    return pl.pallas_call(
        functools.partial(_post_kernel, alpha=alpha),
        grid=(tk.tiles,),
        in_specs=[_tok_spec(d), _tok_spec(d), tk.mod_spec(d), _resident(w.shape), _resident((1, d)), _resident((1, d))],
        out_specs=_tok_spec(d),
        out_shape=jax.ShapeDtypeStruct(h.shape, F32),
        compiler_params=_params("parallel"),
        name="att_post",
    )(o, h, mods, w, g, b)


def _mlproj_kernel(h_ref, hp_ref, hn_ref, mod_ref, wqk_ref, wv_ref, wo_ref, wg_ref, gb_ref, cw_ref, cb_ref,
                   q_ref, k_ref, v_ref, sig_ref, gc_ref, at_ref, xm_scr, *, tk):
    tm = TOKEN_TILE
    lc = SCAN_CHUNK
    t = pl.program_id(0)
    xm = _modulated(h_ref[...], mod_ref, 3).astype(BF16)
    xm_scr[0:tm, :] = xm
    xm_scr[tm:tm + HALO, :] = _modulated(hp_ref[...], mod_ref, 3).astype(BF16)
    xm_scr[tm + HALO:tm + 2 * HALO, :] = _modulated(hn_ref[...], mod_ref, 3).astype(BF16)

    row = lax.broadcasted_iota(jnp.int32, (tm, FF_CHUNK), 0)
    pos_in_seq = t % tk.tiles_per_seq
    lat_first = jnp.logical_and(tk.is_lat(t), pos_in_seq == 0)
    lat_last = jnp.logical_and(tk.is_lat(t), pos_in_seq == tk.tiles_per_seq - 1)
    is_ctx = jnp.logical_not(tk.is_lat(t))
    seq_first = jnp.logical_or(jnp.logical_and(row == 0, lat_first),
                               jnp.logical_and(row % tk.n_ctx == 0, is_ctx))
    seq_last = jnp.logical_or(jnp.logical_and(row == tm - 1, lat_last),
                              jnp.logical_and(row % tk.n_ctx == tk.n_ctx - 1, is_ctx))

    hq = ML_HEADS * ML_QK
    for c in range(2 * hq // FF_CHUNK):
        cols = slice(c * FF_CHUNK, (c + 1) * FF_CHUNK)
        p = jnp.dot(xm_scr[...], wqk_ref[:, cols], preferred_element_type=F32)
        main = p[0:tm]
        prev_row = p[tm + HALO - 1:tm + HALO]
        next_row = p[tm + HALO:tm + HALO + 1]
        down = jnp.where(row == 0, prev_row, pltpu.roll(main, 1, 0))
        up = jnp.where(row == tm - 1, next_row, pltpu.roll(main, tm - 1, 0))
        down = jnp.where(seq_first, 0.0, down)
        up = jnp.where(seq_last, 0.0, up)
        conv = down * cw_ref[0:1, cols] + main * cw_ref[1:2, cols] + up * cw_ref[2:3, cols] + cb_ref[:, cols]
        act = _silu(conv)
        if (c + 1) * FF_CHUNK <= hq:
            q_ref[:, cols] = act.astype(BF16)
        else:
            k_ref[:, c * FF_CHUNK - hq:(c + 1) * FF_CHUNK - hq] = (act * (ML_QK ** -0.5)).astype(BF16)

    v_ref[...] = jnp.dot(xm, wv_ref[...], preferred_element_type=F32).astype(BF16)
    sig_ref[...] = jax.nn.sigmoid(jnp.dot(xm, wo_ref[...], preferred_element_type=F32)).astype(BF16)

    g = jnp.dot(xm, wg_ref[...], preferred_element_type=F32) + gb_ref[...]
    logf = jax.nn.log_sigmoid(g)
    r_i = lax.broadcasted_iota(jnp.int32, (lc, lc), 0)
    c_i = lax.broadcasted_iota(jnp.int32, (lc, lc), 1)
    tri_lo = (c_i <= r_i).astype(F32)
    tri_hi = (c_i >= r_i).astype(F32)
    lane = lax.broadcasted_iota(jnp.int32, (lc, 128), 1)
    trow = lax.broadcasted_iota(jnp.int32, (lc, 128), 0)
    nh = ML_HEADS
    for c in range(tm // lc):
        rows = slice(c * lc, (c + 1) * lc)
        lf = logf[rows]
        pre = jnp.dot(tri_lo, lf, preferred_element_type=F32, precision=lax.Precision.HIGHEST)
        suf = jnp.dot(tri_hi, lf, preferred_element_type=F32, precision=lax.Precision.HIGHEST)
        bc = pltpu.roll(jnp.where(lane < 3 * nh, pre, suf), 128 - 2 * nh, 1)
        a = g[rows] - bc
        pf = a
        pb = a
        k = 1
        while k < lc:
            pf = jnp.maximum(pf, jnp.where(trow >= k, pltpu.roll(pf, k, 0), -jnp.inf))
            pb = jnp.maximum(pb, jnp.where(trow < lc - k, pltpu.roll(pb, lc - k, 0), -jnp.inf))
            k *= 2
        pm = jnp.where(lane < nh, pf, pb)
        gc_ref[rows, :] = jnp.where(lane < 2 * nh, bc,
                                    jnp.where(lane < 4 * nh, pltpu.roll(a, 2 * nh, 1), pltpu.roll(pm, 4 * nh, 1)))
        at_ref[c] = jnp.transpose(a)[0:ML_UNITS, :]


def _mlproj(tk, h, mods, wqk, wv, wo, wg, gb, cw, cb):
    d = h.shape[1]
    tm = TOKEN_TILE
    lc = SCAN_CHUNK
    hq = ML_HEADS * ML_QK
    blocks_per_tile = tm // HALO
    last_block = tk.rows // HALO - 1
    return pl.pallas_call(
        functools.partial(_mlproj_kernel, tk=tk),
        grid=(tk.tiles,),
        in_specs=[
            _tok_spec(d),
            pl.BlockSpec((HALO, d), lambda t: (jnp.maximum(t * blocks_per_tile - 1, 0), 0)),
            pl.BlockSpec((HALO, d), lambda t: (jnp.minimum((t + 1) * blocks_per_tile, last_block), 0)),
            tk.mod_spec(d),
            _resident(wqk.shape), _resident(wv.shape), _resident(wo.shape), _resident(wg.shape),
            _resident(gb.shape), _resident(cw.shape), _resident(cb.shape),
        ],
        out_specs=[
            _tok_spec(hq), _tok_spec(hq), _tok_spec(d), _tok_spec(d), _tok_spec(128),
            pl.BlockSpec((tm // lc, ML_UNITS, lc), lambda t: (t, 0, 0)),
        ],
        out_shape=[
            jax.ShapeDtypeStruct((tk.rows, hq), BF16),
            jax.ShapeDtypeStruct((tk.rows, hq), BF16),
            jax.ShapeDtypeStruct((tk.rows, d), BF16),
            jax.ShapeDtypeStruct((tk.rows, d), BF16),
            jax.ShapeDtypeStruct((tk.rows, 128), F32),
            jax.ShapeDtypeStruct((tk.rows // lc, ML_UNITS, lc), F32),
        ],
        scratch_shapes=[pltpu.VMEM((tm + 2 * HALO, d), BF16)],
        compiler_params=_params("parallel"),
        name="ml_proj",
    )(h, h, h, mods, wqk, wv, wo, wg, gb, cw, cb)


def _scan_kernel(ql_ref, qc_ref, kl_ref, kc_ref, vl_ref, vc_ref, gl_ref, gcx_ref, al_ref, ac_ref, o_ref,
                 ct_scr, m_scr, *, n_lat_chunks, n_ctx_chunks):
    lc = SCAN_CHUNK
    dk, dv = ML_QK, ML_V
    ct_scr[...] = jnp.zeros_like(ct_scr)
    m_scr[...] = jnp.zeros_like(m_scr)
    t_i = lax.broadcasted_iota(jnp.int32, (lc, lc), 0)
    s_i = lax.broadcasted_iota(jnp.int32, (lc, lc), 1)
    masks = (s_i <= t_i, s_i >= t_i)
    ones_col = (lax.broadcasted_iota(jnp.int32, (lc, 128), 1) == 0).astype(BF16)

    def step(j, mode):
        if mode == "ctx":
            q_ref, k_ref, v_ref, g_ref, a_ref, n_seq = qc_ref, kc_ref, vc_ref, gcx_ref, ac_ref, n_ctx_chunks
        else:
            q_ref, k_ref, v_ref, g_ref, a_ref, n_seq = ql_ref, kl_ref, vl_ref, gl_ref, al_ref, n_lat_chunks
        chunk = (j, n_seq - 1 - j)
        for direction in range(2):
            c = chunk[direction]
            r0 = pl.multiple_of(c * lc, lc)
            gc = g_ref[pl.ds(r0, lc), :]
            at = a_ref[c]
            last = lc - 1 if direction == 0 else 0
            for hd in range(ML_HEADS):
                u = direction * ML_HEADS + hd
                qc = q_ref[pl.ds(r0, lc), hd * dk:(hd + 1) * dk]
                kc = k_ref[pl.ds(r0, lc), hd * dk:(hd + 1) * dk]
                vc = v_ref[pl.ds(r0, lc), hd * dv:(hd + 1) * dv]
                v_aug = jnp.concatenate([vc, ones_col], axis=1)
                bcum = gc[:, u:u + 1]
                a_col = gc[:, ML_UNITS + u:ML_UNITS + u + 1]
                pmax = gc[:, 2 * ML_UNITS + u:2 * ML_UNITS + u + 1]
                a_row = at[u:u + 1, :]
                m = m_scr[u, 0:1, 0:1]
                ct = ct_scr[u]
                mu = jnp.maximum(m, pmax)
                if mode != "ctx":
                    w = jnp.where(masks[direction], jnp.exp(a_row - mu), 0.0)
                    s = lax.dot_general(qc, kc, (((1,), (1,)), ((), ())), preferred_element_type=F32)
                    eq = jnp.exp(m - mu) * qc.astype(F32)
                    lhs = jnp.concatenate([(s * w).astype(BF16), eq.astype(BF16)], axis=1)
                    rhs = jnp.concatenate([v_aug, ct.astype(BF16)], axis=0)
                    num = jnp.dot(lhs, rhs, preferred_element_type=F32)
                    den = jnp.maximum(jnp.abs(num[:, dv:dv + 1]), jnp.exp(-(bcum + mu)))
                    hout = num[:, :dv] / den
                    if mode == "store":
                        o_ref[pl.ds(r0, lc), hd * dv:(hd + 1) * dv] = hout
                    else:
                        o_ref[pl.ds(r0, lc), hd * dv:(hd + 1) * dv] += hout
                mu_l = mu[last:last + 1, :]
                wv = (jnp.exp(a_col - mu_l) * v_aug.astype(F32)).astype(BF16)
                upd = lax.dot_general(kc, wv, (((0,), (0,)), ((), ())), preferred_element_type=F32)
                ct_scr[u] = jnp.exp(m - mu_l) * ct + upd
                m_scr[u, 0:1, 0:1] = bcum[last:last + 1, :] + mu_l

    half = n_lat_chunks // 2
    lax.fori_loop(0, n_ctx_chunks, lambda j, _: step(j, "ctx"), None)
    lax.fori_loop(0, half, lambda j, _: step(j, "store"), None)
    lax.fori_loop(half, n_lat_chunks, lambda j, _: step(j, "add"), None)


def _scan(tk, q, k, v, gc, at):
    d = v.shape[1]
    lc = SCAN_CHUNK
    hq = ML_HEADS * ML_QK
    n_lat, n_ctx = tk.n_lat, tk.n_ctx
    n_lat_chunks, n_ctx_chunks = n_lat // lc, n_ctx // lc
    assert n_lat_chunks % 2 == 0
    ctx0 = tk.lat_rows // n_ctx
    ctx_chunk0 = tk.lat_rows // lc // n_ctx_chunks
    lat = lambda width: pl.BlockSpec((n_lat, width), lambda i: (i, 0))
    ctx = lambda width: pl.BlockSpec((n_ctx, width), lambda i: (ctx0 + i, 0))
    kern = functools.partial(_scan_kernel, n_lat_chunks=n_lat_chunks, n_ctx_chunks=n_ctx_chunks)
    return pl.pallas_call(
        kern,
        grid=(tk.bsz,),
        in_specs=[
            lat(hq), ctx(hq), lat(hq), ctx(hq), lat(d), ctx(d), lat(128), ctx(128),
            pl.BlockSpec((n_lat_chunks, ML_UNITS, lc), lambda i: (i, 0, 0)),
            pl.BlockSpec((n_ctx_chunks, ML_UNITS, lc), lambda i: (ctx_chunk0 + i, 0, 0)),
        ],
        out_specs=lat(d),
        out_shape=jax.ShapeDtypeStruct((tk.lat_rows, d), F32),
        scratch_shapes=[
            pltpu.VMEM((ML_UNITS, ML_QK, ML_V + 128), F32),
            pltpu.VMEM((ML_UNITS, 8, 128), F32),
        ],
        compiler_params=_params("parallel"),
        name="ml_scan",
    )(q, q, k, k, v, v, gc, gc, at, at)


def _readout_kernel(hs_ref, sig_ref, h_ref, mod_ref, ng_ref, w_ref, g_ref, b_ref, out_ref, *, alpha):
    dv = ML_V
    parts = []
    for hd in range(ML_HEADS):
        x = hs_ref[:, hd * dv:(hd + 1) * dv]
        mu = jnp.mean(x, axis=-1, keepdims=True)
        xc = x - mu
        var = jnp.mean(xc * xc, axis=-1, keepdims=True)
        parts.append(xc * lax.rsqrt(var + LN_EPS))
    hn = jnp.concatenate(parts, axis=1) * ng_ref[...]
    z = (hn * sig_ref[...].astype(F32)).astype(BF16)
    y = jnp.dot(z, w_ref[...], preferred_element_type=F32)
    out_ref[...] = _layer_norm(alpha * h_ref[...] + mod_ref[5:6, :] * y, g_ref[...], b_ref[...])


def _readout(tk, hs, sig, h, mods, ng, w, g, b, alpha):
    d = hs.shape[1]
    return pl.pallas_call(
        functools.partial(_readout_kernel, alpha=alpha),
        grid=(tk.lat_tiles,),
        in_specs=[_tok_spec(d), _tok_spec(d), _tok_spec(d), tk.mod_spec(d),
                  _resident((1, d)), _resident(w.shape), _resident((1, d)), _resident((1, d))],
        out_specs=_tok_spec(d),
        out_shape=jax.ShapeDtypeStruct(hs.shape, F32),
        compiler_params=_params("parallel"),
        name="ml_readout",
    )(hs, sig, h, mods, ng, w, g, b)


def _rope_tables(n_lat):
    rows = n_lat // GRID_W
    row = jnp.repeat(jnp.arange(rows, dtype=jnp.int32), GRID_W).astype(F32)
    col = jnp.tile(jnp.arange(GRID_W, dtype=jnp.int32), rows).astype(F32)
    inv = ROPE_THETA ** (-jnp.arange(ROPE_PAIRS, dtype=F32) / ROPE_PAIRS)
    ar, ac = row[:, None] * inv, col[:, None] * inv
    cos = jnp.concatenate([jnp.cos(ar), jnp.cos(ar), jnp.cos(ac), jnp.cos(ac)], axis=1)
    sin = jnp.concatenate([-jnp.sin(ar), jnp.sin(ar), -jnp.sin(ac), jnp.sin(ac)], axis=1)
    cos = jnp.concatenate([cos, jnp.ones((TOKEN_TILE, ATT_HEAD_DIM), F32)], axis=0)
    sin = jnp.concatenate([sin, jnp.zeros((TOKEN_TILE, ATT_HEAD_DIM), F32)], axis=0)
    return cos, sin


def kernel(x, c, ctx, c_ctx, ada_w, ada_b, ln_g, ln_b, ffn_w_in, ffn_w_out, att_w_in, att_q_gain, att_k_gain,
           att_w_out, ml_w_in, ml_gate_b, ml_conv_w, ml_conv_b, ml_norm_g, ml_w_out):
    bsz, n_lat, d = x.shape
    n_ctx = ctx.shape[1]
    depth = ada_w.shape[0]
    dff = ffn_w_out.shape[2]
    assert depth == 2, "layer 0 attention, layer 1 (last) mLSTM"
    assert dff % FF_CHUNK == 0 and TOKEN_TILE % SCAN_CHUNK == 0
    tk = _Tokens(bsz, n_lat, n_ctx)
    alpha = (2.0 * depth) ** 0.25

    mod_rows = -(-(bsz + 1) // 8) * 8
    cc = jnp.concatenate([c, c_ctx[None, :], jnp.zeros((mod_rows - bsz - 1, d), F32)], axis=0)
    mods = _ada(cc, ada_w, ada_b).reshape(depth, mod_rows, N_MOD, d)

    win = ffn_w_in.astype(BF16)
    wout = ffn_w_out.astype(BF16)
    row2 = lambda a: a.reshape(1, -1)
    cos, sin = _rope_tables(n_lat)

    m_i = mods[0]
    h = _ffn(tk, (x.reshape(bsz * n_lat, d), ctx.reshape(bsz * n_ctx, d)), m_i, tk.tiles, win[0, 0], wout[0, 0],
             row2(ln_g[0, 0]), row2(ln_b[0, 0]), 0, alpha)
    q, k, v = _qkv(tk, h, m_i, att_w_in[0].astype(BF16), row2(att_q_gain[0]), row2(att_k_gain[0]), cos, sin)
    o = _attention(tk, q, k, v)
    h = _post(tk, o, h, m_i, att_w_out[0].astype(BF16), row2(ln_g[0, 1]), row2(ln_b[0, 1]), alpha)
    h = _ffn(tk, h, m_i, tk.tiles, win[0, 1], wout[0, 1], row2(ln_g[0, 2]), row2(ln_b[0, 2]), 6, alpha)

    m_i = mods[1]
    h = _ffn(tk, h, m_i, tk.tiles, win[1, 0], wout[1, 0], row2(ln_g[1, 0]), row2(ln_b[1, 0]), 0, alpha)
    hq = ML_HEADS * ML_QK
    w = ml_w_in[0]
    wqk = w[:, :2 * hq].astype(BF16)
    wv = w[:, 2 * hq:2 * hq + d].astype(BF16)
    wo = w[:, 2 * hq + d:2 * hq + 2 * d].astype(BF16)
    perm = jnp.array([0, 2, 1, 3])
    wg = w[:, 2 * hq + 2 * d:].reshape(d, 4, ML_HEADS)[:, perm].reshape(d, 4 * ML_HEADS)
    wg = jnp.pad(wg, ((0, 0), (0, 128 - 4 * ML_HEADS))).astype(BF16)
    gb = ml_gate_b[0].reshape(4, ML_HEADS)[perm].reshape(1, 4 * ML_HEADS)
    gb = jnp.pad(gb, ((0, 0), (0, 128 - 4 * ML_HEADS)))
    q, k, v, sig, gc, at = _mlproj(tk, h, m_i, wqk, wv, wo, wg, gb, ml_conv_w[0], row2(ml_conv_b[0]))
    hs = _scan(tk, q, k, v, gc, at)
    h = _readout(tk, hs, sig, h, m_i, row2(ml_norm_g[0]), ml_w_out[0].astype(BF16),
                 row2(ln_g[1, 1]), row2(ln_b[1, 1]), alpha)
    h = _ffn(tk, h, m_i, tk.lat_tiles, win[1, 1], wout[1, 1], row2(ln_g[1, 2]), row2(ln_b[1, 2]), 6, alpha)
    return h.reshape(bsz, n_lat, d)
```

```python
import functools

import jax
import jax.numpy as jnp
from jax import lax
from jax.experimental import pallas as pl
from jax.experimental.pallas import tpu as pltpu

F32 = jnp.float32
BF16 = jnp.bfloat16

N_MOD = 9
GRID_W = 64
ATT_HEADS = 8
ATT_KV_HEADS = 2
ATT_GROUP = ATT_HEADS // ATT_KV_HEADS
ATT_HEAD_DIM = 128
ROPE_PAIRS = ATT_HEAD_DIM // 4
ROPE_THETA = 10000.0
ML_HEADS = 4
ML_QK = 128
ML_V = 256
ML_UNITS = 2 * ML_HEADS
LN_EPS = 1e-5
RMS_EPS = 1e-6
LOG2_E = 1.4426950408889634

TOKEN_TILE = 512
ATT_Q_TILE = 256
ATT_KEY_BLOCK = 256
FF_CHUNK = 256
SCAN_CHUNK = 128
HALO = 8
VMEM_LIMIT = 56 * 1024 * 1024


def _params(*sem):
    return pltpu.CompilerParams(dimension_semantics=sem, vmem_limit_bytes=VMEM_LIMIT)


def _resident(shape):
    n = len(shape)
    return pl.BlockSpec(shape, lambda *_: (0,) * n, pipeline_mode=pl.Buffered(1))


def _layer_norm(x, g, b):
    mu = jnp.mean(x, axis=-1, keepdims=True)
    xc = x - mu
    var = jnp.mean(xc * xc, axis=-1, keepdims=True)
    return xc * lax.rsqrt(var + LN_EPS) * g + b


def _silu(x):
    return x * jax.nn.sigmoid(x)


def _modulated(h, mod_ref, k0):
    return h * (1.0 + mod_ref[k0 + 1:k0 + 2, :]) + mod_ref[k0:k0 + 1, :]


class _Tokens:
    def __init__(self, bsz, n_lat, n_ctx):
        tm = TOKEN_TILE
        assert n_lat % tm == 0 and (bsz * n_ctx) % tm == 0 and tm % n_ctx == 0
        self.bsz, self.n_lat, self.n_ctx = bsz, n_lat, n_ctx
        self.lat_rows = bsz * n_lat
        self.rows = bsz * (n_lat + n_ctx)
        self.tiles_per_seq = n_lat // tm
        self.lat_tiles = self.lat_rows // tm
        self.tiles = self.rows // tm

    def is_lat(self, t):
        return t < self.lat_tiles

    def mod_row(self, t):
        return jnp.where(t < self.lat_tiles, t // self.tiles_per_seq, self.bsz)

    def mod_spec(self, d):
        return pl.BlockSpec((None, N_MOD, d), lambda t: (self.mod_row(t), 0, 0))


def _tok_spec(width):
    return pl.BlockSpec((TOKEN_TILE, width), lambda t: (t, 0))


def _ada_kernel(c_ref, w_ref, b_ref, o_ref):
    s = _silu(c_ref[...]).astype(BF16)
    o_ref[...] = jnp.dot(s, w_ref[...].astype(BF16), preferred_element_type=F32) + b_ref[...]


def _ada(cc, ada_w, ada_b):
    depth, d, nd = ada_w.shape
    rows = cc.shape[0]
    return pl.pallas_call(
        _ada_kernel,
        grid=(depth, nd // d),
        in_specs=[
            pl.BlockSpec((rows, d), lambda i, j: (0, 0)),
            pl.BlockSpec((None, d, d), lambda i, j: (i, 0, j)),
            pl.BlockSpec((None, 1, d), lambda i, j: (i, 0, j)),
        ],
        out_specs=pl.BlockSpec((None, rows, d), lambda i, j: (i, 0, j)),
        out_shape=jax.ShapeDtypeStruct((depth, rows, nd), F32),
        compiler_params=_params("parallel", "parallel"),
        name="ada_mod",
    )(cc, ada_w, ada_b.reshape(depth, 1, nd))


def _ffn_kernel(*refs, k0, alpha, dff, lat_tiles, split_input):
    if split_input:
        hx_ref, hc_ref, mod_ref, win_ref, wout_ref, g_ref, b_ref, o_ref = refs
        h = jnp.where(pl.program_id(0) < lat_tiles, hx_ref[...], hc_ref[...])
    else:
        h_ref, mod_ref, win_ref, wout_ref, g_ref, b_ref, o_ref = refs
        h = h_ref[...]
    xm = _modulated(h, mod_ref, k0).astype(BF16)
    acc = None
    for j in range(dff // FF_CHUNK):
        cols = slice(FF_CHUNK * j, FF_CHUNK * (j + 1))
        a = jnp.dot(xm, win_ref[:, cols], preferred_element_type=F32)
        u = jnp.dot(xm, win_ref[:, dff + FF_CHUNK * j:dff + FF_CHUNK * (j + 1)], preferred_element_type=F32)
        hid = (_silu(a) * u).astype(BF16)
        y = jnp.dot(hid, wout_ref[cols, :], preferred_element_type=F32)
        acc = y if acc is None else acc + y
    gate = mod_ref[k0 + 2:k0 + 3, :]
    o_ref[...] = _layer_norm(alpha * h + (0.5 * gate) * acc, g_ref[...], b_ref[...])


def _ffn(tk, hs, mods, n_tiles, win, wout, g, b, k0, alpha):
    split = isinstance(hs, tuple)
    d, dff = wout.shape[1], wout.shape[0]
    tm = TOKEN_TILE
    if split:
        last_lat = tk.lat_tiles - 1
        h_specs = [pl.BlockSpec((tm, d), lambda t: (jnp.minimum(t, last_lat), 0)),
                   pl.BlockSpec((tm, d), lambda t: (jnp.maximum(t - tk.lat_tiles, 0), 0))]
        hs = list(hs)
    else:
        h_specs = [_tok_spec(d)]
        hs = [hs]
    kern = functools.partial(_ffn_kernel, k0=k0, alpha=alpha, dff=dff, lat_tiles=tk.lat_tiles, split_input=split)
    return pl.pallas_call(
        kern,
        grid=(n_tiles,),
        in_specs=h_specs + [tk.mod_spec(d), _resident(win.shape), _resident(wout.shape),
                            _resident((1, d)), _resident((1, d))],
        out_specs=_tok_spec(d),
        out_shape=jax.ShapeDtypeStruct((n_tiles * tm, d), F32),
        compiler_params=_params("parallel"),
        name="half_ffn",
    )(*hs, mods, win, wout, g, b)


def _qkv_kernel(h_ref, mod_ref, wqk_ref, wvt_ref, qg_ref, kg_ref, ct_ref, st_ref, q_ref, k_ref, vt_ref):
    xm = _modulated(h_ref[...], mod_ref, 3).astype(BF16)
    ct = ct_ref[...]
    st = st_ref[...]
    dh = ATT_HEAD_DIM

    def norm_rope(x, gain):
        xn = x * lax.rsqrt(jnp.mean(x * x, axis=-1, keepdims=True) + RMS_EPS) * gain
        return xn * ct + pltpu.roll(xn, dh // 2, 1) * st

    qg = qg_ref[...] * (dh ** -0.5 * LOG2_E)
    kg = kg_ref[...]
    for pair in range((ATT_HEADS + ATT_KV_HEADS) // 2):
        p = jnp.dot(xm, wqk_ref[:, 2 * pair * dh:2 * (pair + 1) * dh], preferred_element_type=F32)
        for half in range(2):
            hd = 2 * pair + half
            x = p[:, half * dh:(half + 1) * dh]
            if hd < ATT_HEADS:
                q_ref[:, hd * dh:(hd + 1) * dh] = norm_rope(x, qg).astype(BF16)
            else:
                hk = hd - ATT_HEADS
                k_ref[:, hk * dh:(hk + 1) * dh] = norm_rope(x, kg).astype(BF16)
    vt = lax.dot_general(wvt_ref[...], xm, (((1,), (1,)), ((), ())), preferred_element_type=F32)
    vt_ref[...] = vt.astype(BF16)


def _qkv(tk, h, mods, wqk, wvt, qg, kg, ctab, stab):
    d = h.shape[1]
    tm = TOKEN_TILE
    dh = ATT_HEAD_DIM
    nq, nk = ATT_HEADS * dh, ATT_KV_HEADS * dh
    tab_spec = pl.BlockSpec((tm, dh), lambda t: (jnp.where(tk.is_lat(t), t % tk.tiles_per_seq, tk.tiles_per_seq), 0))
    return pl.pallas_call(
        _qkv_kernel,
        grid=(tk.tiles,),
        in_specs=[_tok_spec(d), tk.mod_spec(d), _resident(wqk.shape), _resident(wvt.shape),
                  _resident((1, dh)), _resident((1, dh)), tab_spec, tab_spec],
        out_specs=[_tok_spec(nq), _tok_spec(nk), pl.BlockSpec((nk, tm), lambda t: (0, t))],
        out_shape=[jax.ShapeDtypeStruct((tk.rows, nq), BF16),
                   jax.ShapeDtypeStruct((tk.rows, nk), BF16),
                   jax.ShapeDtypeStruct((nk, tk.rows), BF16)],
        compiler_params=_params("parallel"),
        name="att_qkv",
    )(h, mods, wqk, wvt, qg, kg, ctab, stab)


def _attn_kernel(q_ref, kl_ref, kc_ref, vlt_ref, vct_ref, o_ref, s_scr, *, lat_q_tiles):
    dh = ATT_HEAD_DIM
    kb = ATT_KEY_BLOCK
    tq = q_ref.shape[0]
    n_lat, n_ctx = kl_ref.shape[0], kc_ref.shape[0]
    nt = (((1,), (1,)), ((), ()))

    def key_blocks(with_lat):
        blocks = []
        if with_lat:
            blocks += [(kl_ref, vlt_ref, b * kb, b * kb) for b in range(n_lat // kb)]
        base = n_lat if with_lat else 0
        blocks += [(kc_ref, vct_ref, b * kb, base + b * kb) for b in range(n_ctx // kb)]
        return blocks

    def scores(g, blocks):
        q = q_ref[:, g * dh:(g + 1) * dh]
        m8 = None
        for k_ref, _, r0, s0 in blocks:
            s = lax.dot_general(k_ref[r0:r0 + kb, :], q, nt, preferred_element_type=F32)
            s_scr[g % 2, s0:s0 + kb, :] = s
            bm = jnp.max(s.reshape(kb // 8, 8, tq), axis=0)
            m8 = bm if m8 is None else jnp.maximum(m8, bm)
        return jnp.max(m8, axis=0, keepdims=True)

    def weighted_values(g, blocks, m):
        l8 = None
        ot = None
        for _, vt_ref, r0, s0 in blocks:
            p = jnp.exp2(s_scr[g % 2, s0:s0 + kb, :] - m)
            ps = jnp.sum(p.reshape(kb // 8, 8, tq), axis=0)
            l8 = ps if l8 is None else l8 + ps
            o_blk = jnp.dot(vt_ref[:, r0:r0 + kb], p.astype(BF16), preferred_element_type=F32)
            ot = o_blk if ot is None else ot + o_blk
        denom = jnp.sum(l8, axis=0, keepdims=True)
        o_ref[:, g * dh:(g + 1) * dh] = jnp.transpose(ot / denom).astype(BF16)

    def run(blocks):
        m = scores(0, blocks)
        for g in range(ATT_GROUP):
            m_next = scores(g + 1, blocks) if g + 1 < ATT_GROUP else None
            weighted_values(g, blocks, m)
            m = m_next

    t = pl.program_id(2)

    @pl.when(t < lat_q_tiles)
    def _():
        run(key_blocks(True))

    @pl.when(t >= lat_q_tiles)
    def _():
        run(key_blocks(False))


def _attention(tk, q, k, vt):
    tq = ATT_Q_TILE
    dh = ATT_HEAD_DIM
    gw = ATT_GROUP * dh
    n_lat, n_ctx = tk.n_lat, tk.n_ctx
    assert n_lat % tq == 0 and n_ctx % tq == 0
    lat_q_tiles = n_lat // tq
    q_tiles = lat_q_tiles + n_ctx // tq
    ctx_block0 = tk.lat_rows // n_ctx

    def q_row(i, t):
        return jnp.where(t < lat_q_tiles, i * lat_q_tiles + t,
                         tk.lat_rows // tq + i * (n_ctx // tq) + (t - lat_q_tiles))

    q_spec = pl.BlockSpec((tq, gw), lambda i, j, t: (q_row(i, t), j))
    return pl.pallas_call(
        functools.partial(_attn_kernel, lat_q_tiles=lat_q_tiles),
        grid=(tk.bsz, ATT_KV_HEADS, q_tiles),
        in_specs=[q_spec,
                  pl.BlockSpec((n_lat, dh), lambda i, j, t: (i, j)),
                  pl.BlockSpec((n_ctx, dh), lambda i, j, t: (ctx_block0 + i, j)),
                  pl.BlockSpec((dh, n_lat), lambda i, j, t: (j, i)),
                  pl.BlockSpec((dh, n_ctx), lambda i, j, t: (j, ctx_block0 + i))],
        out_specs=q_spec,
        out_shape=jax.ShapeDtypeStruct(q.shape, BF16),
        scratch_shapes=[pltpu.VMEM((2, n_lat + n_ctx, tq), F32)],
        compiler_params=_params("parallel", "parallel", "parallel"),
        name="att_core",
    )(q, k, k, vt, vt)


def _post_kernel(o_ref, h_ref, mod_ref, w_ref, g_ref, b_ref, out_ref, *, alpha):
    y = jnp.dot(o_ref[...], w_ref[...], preferred_element_type=F32)
    out_ref[...] = _layer_norm(alpha * h_ref[...] + mod_ref[5:6, :] * y, g_ref[...], b_ref[...])


def _post(tk, o, h, mods, w, g, b, alpha):
    d = h.shape[1]
    return pl.pallas_call(
        functools.partial(_post_kernel, alpha=alpha),
        grid=(tk.tiles,),
        in_specs=[_tok_spec(d), _tok_spec(d), tk.mod_spec(d), _resident(w.shape), _resident((1, d)), _resident((1, d))],
        out_specs=_tok_spec(d),
        out_shape=jax.ShapeDtypeStruct(h.shape, F32),
        compiler_params=_params("parallel"),
        name="att_post",
    )(o, h, mods, w, g, b)


def _mlproj_kernel(h_ref, hp_ref, hn_ref, mod_ref, wqk_ref, wv_ref, wo_ref, wg_ref, gb_ref, cw_ref, cb_ref,
                   q_ref, k_ref, v_ref, sig_ref, gc_ref, at_ref, xm_scr, *, tk):
    tm = TOKEN_TILE
    lc = SCAN_CHUNK
    t = pl.program_id(0)
    xm = _modulated(h_ref[...], mod_ref, 3).astype(BF16)
    xm_scr[0:tm, :] = xm
    xm_scr[tm:tm + HALO, :] = _modulated(hp_ref[...], mod_ref, 3).astype(BF16)
    xm_scr[tm + HALO:tm + 2 * HALO, :] = _modulated(hn_ref[...], mod_ref, 3).astype(BF16)

    row = lax.broadcasted_iota(jnp.int32, (tm, FF_CHUNK), 0)
    pos_in_seq = t % tk.tiles_per_seq
    lat_first = jnp.logical_and(tk.is_lat(t), pos_in_seq == 0)
    lat_last = jnp.logical_and(tk.is_lat(t), pos_in_seq == tk.tiles_per_seq - 1)
    is_ctx = jnp.logical_not(tk.is_lat(t))
    seq_first = jnp.logical_or(jnp.logical_and(row == 0, lat_first),
                               jnp.logical_and(row % tk.n_ctx == 0, is_ctx))
    seq_last = jnp.logical_or(jnp.logical_and(row == tm - 1, lat_last),
                              jnp.logical_and(row % tk.n_ctx == tk.n_ctx - 1, is_ctx))

    hq = ML_HEADS * ML_QK
    for c in range(2 * hq // FF_CHUNK):
        cols = slice(c * FF_CHUNK, (c + 1) * FF_CHUNK)
        p = jnp.dot(xm_scr[...], wqk_ref[:, cols], preferred_element_type=F32)
        main = p[0:tm]
        prev_row = p[tm + HALO - 1:tm + HALO]
        next_row = p[tm + HALO:tm + HALO + 1]
        down = jnp.where(row == 0, prev_row, pltpu.roll(main, 1, 0))
        up = jnp.where(row == tm - 1, next_row, pltpu.roll(main, tm - 1, 0))
        down = jnp.where(seq_first, 0.0, down)
        up = jnp.where(seq_last, 0.0, up)
        conv = down * cw_ref[0:1, cols] + main * cw_ref[1:2, cols] + up * cw_ref[2:3, cols] + cb_ref[:, cols]
        act = _silu(conv)
        if (c + 1) * FF_CHUNK <= hq:
            q_ref[:, cols] = act.astype(BF16)
        else:
            k_ref[:, c * FF_CHUNK - hq:(c + 1) * FF_CHUNK - hq] = (act * (ML_QK ** -0.5)).astype(BF16)

    v_ref[...] = jnp.dot(xm, wv_ref[...], preferred_element_type=F32).astype(BF16)
    sig_ref[...] = jax.nn.sigmoid(jnp.dot(xm, wo_ref[...], preferred_element_type=F32)).astype(BF16)

    g = jnp.dot(xm, wg_ref[...], preferred_element_type=F32) + gb_ref[...]
    logf = jax.nn.log_sigmoid(g)
    r_i = lax.broadcasted_iota(jnp.int32, (lc, lc), 0)
    c_i = lax.broadcasted_iota(jnp.int32, (lc, lc), 1)
    tri_lo = (c_i <= r_i).astype(F32)
    tri_hi = (c_i >= r_i).astype(F32)
    lane = lax.broadcasted_iota(jnp.int32, (lc, 128), 1)
    trow = lax.broadcasted_iota(jnp.int32, (lc, 128), 0)
    nh = ML_HEADS
    for c in range(tm // lc):
        rows = slice(c * lc, (c + 1) * lc)
        lf = logf[rows]
        pre = jnp.dot(tri_lo, lf, preferred_element_type=F32, precision=lax.Precision.HIGHEST)
        suf = jnp.dot(tri_hi, lf, preferred_element_type=F32, precision=lax.Precision.HIGHEST)
        bc = pltpu.roll(jnp.where(lane < 3 * nh, pre, suf), 128 - 2 * nh, 1)
        a = g[rows] - bc
        pf = a
        pb = a
        k = 1
        while k < lc:
            pf = jnp.maximum(pf, jnp.where(trow >= k, pltpu.roll(pf, k, 0), -jnp.inf))
            pb = jnp.maximum(pb, jnp.where(trow < lc - k, pltpu.roll(pb, lc - k, 0), -jnp.inf))
            k *= 2
        pm = jnp.where(lane < nh, pf, pb)
        gc_ref[rows, :] = jnp.where(lane < 2 * nh, bc,
                                    jnp.where(lane < 4 * nh, pltpu.roll(a, 2 * nh, 1), pltpu.roll(pm, 4 * nh, 1)))
        at_ref[c] = jnp.transpose(a)[0:ML_UNITS, :]


def _mlproj(tk, h, mods, wqk, wv, wo, wg, gb, cw, cb):
    d = h.shape[1]
    tm = TOKEN_TILE
    lc = SCAN_CHUNK
    hq = ML_HEADS * ML_QK
    blocks_per_tile = tm // HALO
    last_block = tk.rows // HALO - 1
    return pl.pallas_call(
        functools.partial(_mlproj_kernel, tk=tk),
        grid=(tk.tiles,),
        in_specs=[
            _tok_spec(d),
            pl.BlockSpec((HALO, d), lambda t: (jnp.maximum(t * blocks_per_tile - 1, 0), 0)),
            pl.BlockSpec((HALO, d), lambda t: (jnp.minimum((t + 1) * blocks_per_tile, last_block), 0)),
            tk.mod_spec(d),
            _resident(wqk.shape), _resident(wv.shape), _resident(wo.shape), _resident(wg.shape),
            _resident(gb.shape), _resident(cw.shape), _resident(cb.shape),
        ],
        out_specs=[
            _tok_spec(hq), _tok_spec(hq), _tok_spec(d), _tok_spec(d), _tok_spec(128),
            pl.BlockSpec((tm // lc, ML_UNITS, lc), lambda t: (t, 0, 0)),
        ],
        out_shape=[
            jax.ShapeDtypeStruct((tk.rows, hq), BF16),
            jax.ShapeDtypeStruct((tk.rows, hq), BF16),
            jax.ShapeDtypeStruct((tk.rows, d), BF16),
            jax.ShapeDtypeStruct((tk.rows, d), BF16),
            jax.ShapeDtypeStruct((tk.rows, 128), F32),
            jax.ShapeDtypeStruct((tk.rows // lc, ML_UNITS, lc), F32),
        ],
        scratch_shapes=[pltpu.VMEM((tm + 2 * HALO, d), BF16)],
        compiler_params=_params("parallel"),
        name="ml_proj",
    )(h, h, h, mods, wqk, wv, wo, wg, gb, cw, cb)


def _scan_kernel(ql_ref, qc_ref, kl_ref, kc_ref, vl_ref, vc_ref, gl_ref, gcx_ref, al_ref, ac_ref, o_ref,
                 ct_scr, m_scr, *, n_lat_chunks, n_ctx_chunks):
    lc = SCAN_CHUNK
    dk, dv = ML_QK, ML_V
    ct_scr[...] = jnp.zeros_like(ct_scr)
    m_scr[...] = jnp.zeros_like(m_scr)
    t_i = lax.broadcasted_iota(jnp.int32, (lc, lc), 0)
    s_i = lax.broadcasted_iota(jnp.int32, (lc, lc), 1)
    masks = (s_i <= t_i, s_i >= t_i)
    ones_col = (lax.broadcasted_iota(jnp.int32, (lc, 128), 1) == 0).astype(BF16)

    def step(j, mode):
        if mode == "ctx":
            q_ref, k_ref, v_ref, g_ref, a_ref, n_seq = qc_ref, kc_ref, vc_ref, gcx_ref, ac_ref, n_ctx_chunks
        else:
            q_ref, k_ref, v_ref, g_ref, a_ref, n_seq = ql_ref, kl_ref, vl_ref, gl_ref, al_ref, n_lat_chunks
        chunk = (j, n_seq - 1 - j)
        for direction in range(2):
            c = chunk[direction]
            r0 = pl.multiple_of(c * lc, lc)
            gc = g_ref[pl.ds(r0, lc), :]
            at = a_ref[c]
            last = lc - 1 if direction == 0 else 0
            for hd in range(ML_HEADS):
                u = direction * ML_HEADS + hd
                qc = q_ref[pl.ds(r0, lc), hd * dk:(hd + 1) * dk]
                kc = k_ref[pl.ds(r0, lc), hd * dk:(hd + 1) * dk]
                vc = v_ref[pl.ds(r0, lc), hd * dv:(hd + 1) * dv]
                v_aug = jnp.concatenate([vc, ones_col], axis=1)
                bcum = gc[:, u:u + 1]
                a_col = gc[:, ML_UNITS + u:ML_UNITS + u + 1]
                pmax = gc[:, 2 * ML_UNITS + u:2 * ML_UNITS + u + 1]
                a_row = at[u:u + 1, :]
                m = m_scr[u, 0:1, 0:1]
                ct = ct_scr[u]
                mu = jnp.maximum(m, pmax)
                if mode != "ctx":
                    w = jnp.where(masks[direction], jnp.exp(a_row - mu), 0.0)
                    s = lax.dot_general(qc, kc, (((1,), (1,)), ((), ())), preferred_element_type=F32)
                    eq = jnp.exp(m - mu) * qc.astype(F32)
                    lhs = jnp.concatenate([(s * w).astype(BF16), eq.astype(BF16)], axis=1)
                    rhs = jnp.concatenate([v_aug, ct.astype(BF16)], axis=0)
                    num = jnp.dot(lhs, rhs, preferred_element_type=F32)
                    den = jnp.maximum(jnp.abs(num[:, dv:dv + 1]), jnp.exp(-(bcum + mu)))
                    hout = num[:, :dv] / den
                    if mode == "store":
                        o_ref[pl.ds(r0, lc), hd * dv:(hd + 1) * dv] = hout
                    else:
                        o_ref[pl.ds(r0, lc), hd * dv:(hd + 1) * dv] += hout
                mu_l = mu[last:last + 1, :]
                wv = (jnp.exp(a_col - mu_l) * v_aug.astype(F32)).astype(BF16)
                upd = lax.dot_general(kc, wv, (((0,), (0,)), ((), ())), preferred_element_type=F32)
                ct_scr[u] = jnp.exp(m - mu_l) * ct + upd
                m_scr[u, 0:1, 0:1] = bcum[last:last + 1, :] + mu_l

    half = n_lat_chunks // 2
    lax.fori_loop(0, n_ctx_chunks, lambda j, _: step(j, "ctx"), None)
    lax.fori_loop(0, half, lambda j, _: step(j, "store"), None)
    lax.fori_loop(half, n_lat_chunks, lambda j, _: step(j, "add"), None)


def _scan(tk, q, k, v, gc, at):
    d = v.shape[1]
    lc = SCAN_CHUNK
    hq = ML_HEADS * ML_QK
    n_lat, n_ctx = tk.n_lat, tk.n_ctx
    n_lat_chunks, n_ctx_chunks = n_lat // lc, n_ctx // lc
    assert n_lat_chunks % 2 == 0
    ctx0 = tk.lat_rows // n_ctx
    ctx_chunk0 = tk.lat_rows // lc // n_ctx_chunks
    lat = lambda width: pl.BlockSpec((n_lat, width), lambda i: (i, 0))
    ctx = lambda width: pl.BlockSpec((n_ctx, width), lambda i: (ctx0 + i, 0))
    kern = functools.partial(_scan_kernel, n_lat_chunks=n_lat_chunks, n_ctx_chunks=n_ctx_chunks)
    return pl.pallas_call(
        kern,
        grid=(tk.bsz,),
        in_specs=[
            lat(hq), ctx(hq), lat(hq), ctx(hq), lat(d), ctx(d), lat(128), ctx(128),
            pl.BlockSpec((n_lat_chunks, ML_UNITS, lc), lambda i: (i, 0, 0)),
            pl.BlockSpec((n_ctx_chunks, ML_UNITS, lc), lambda i: (ctx_chunk0 + i, 0, 0)),
        ],
        out_specs=lat(d),
        out_shape=jax.ShapeDtypeStruct((tk.lat_rows, d), F32),
        scratch_shapes=[
            pltpu.VMEM((ML_UNITS, ML_QK, ML_V + 128), F32),
            pltpu.VMEM((ML_UNITS, 8, 128), F32),
        ],
        compiler_params=_params("parallel"),
        name="ml_scan",
    )(q, q, k, k, v, v, gc, gc, at, at)


def _readout_kernel(hs_ref, sig_ref, h_ref, mod_ref, ng_ref, w_ref, g_ref, b_ref, out_ref, *, alpha):
    dv = ML_V
    parts = []
    for hd in range(ML_HEADS):
        x = hs_ref[:, hd * dv:(hd + 1) * dv]
        mu = jnp.mean(x, axis=-1, keepdims=True)
        xc = x - mu
        var = jnp.mean(xc * xc, axis=-1, keepdims=True)
        parts.append(xc * lax.rsqrt(var + LN_EPS))
    hn = jnp.concatenate(parts, axis=1) * ng_ref[...]
    z = (hn * sig_ref[...].astype(F32)).astype(BF16)
    y = jnp.dot(z, w_ref[...], preferred_element_type=F32)
    out_ref[...] = _layer_norm(alpha * h_ref[...] + mod_ref[5:6, :] * y, g_ref[...], b_ref[...])


def _readout(tk, hs, sig, h, mods, ng, w, g, b, alpha):
    d = hs.shape[1]
    return pl.pallas_call(
        functools.partial(_readout_kernel, alpha=alpha),
        grid=(tk.lat_tiles,),
        in_specs=[_tok_spec(d), _tok_spec(d), _tok_spec(d), tk.mod_spec(d),
                  _resident((1, d)), _resident(w.shape), _resident((1, d)), _resident((1, d))],
        out_specs=_tok_spec(d),
        out_shape=jax.ShapeDtypeStruct(hs.shape, F32),
        compiler_params=_params("parallel"),
        name="ml_readout",
    )(hs, sig, h, mods, ng, w, g, b)


def _rope_tables(n_lat):
    rows = n_lat // GRID_W
    row = jnp.repeat(jnp.arange(rows, dtype=jnp.int32), GRID_W).astype(F32)
    col = jnp.tile(jnp.arange(GRID_W, dtype=jnp.int32), rows).astype(F32)
    inv = ROPE_THETA ** (-jnp.arange(ROPE_PAIRS, dtype=F32) / ROPE_PAIRS)
    ar, ac = row[:, None] * inv, col[:, None] * inv
    cos = jnp.concatenate([jnp.cos(ar), jnp.cos(ac), jnp.cos(ar), jnp.cos(ac)], axis=1)
    sin = jnp.concatenate([-jnp.sin(ar), -jnp.sin(ac), jnp.sin(ar), jnp.sin(ac)], axis=1)
    cos = jnp.concatenate([cos, jnp.ones((TOKEN_TILE, ATT_HEAD_DIM), F32)], axis=0)
    sin = jnp.concatenate([sin, jnp.zeros((TOKEN_TILE, ATT_HEAD_DIM), F32)], axis=0)
    return cos, sin


def _rope_column_order():
    p = ROPE_PAIRS
    return jnp.concatenate([jnp.arange(0, p), jnp.arange(2 * p, 3 * p), jnp.arange(p, 2 * p), jnp.arange(3 * p, 4 * p)])


def kernel(x, c, ctx, c_ctx, ada_w, ada_b, ln_g, ln_b, ffn_w_in, ffn_w_out, att_w_in, att_q_gain, att_k_gain,
           att_w_out, ml_w_in, ml_gate_b, ml_conv_w, ml_conv_b, ml_norm_g, ml_w_out):
    bsz, n_lat, d = x.shape
    n_ctx = ctx.shape[1]
    depth = ada_w.shape[0]
    dff = ffn_w_out.shape[2]
    assert depth == 2, "layer 0 attention, layer 1 (last) mLSTM"
    assert dff % FF_CHUNK == 0 and TOKEN_TILE % SCAN_CHUNK == 0
    tk = _Tokens(bsz, n_lat, n_ctx)
    alpha = (2.0 * depth) ** 0.25

    mod_rows = -(-(bsz + 1) // 8) * 8
    cc = jnp.concatenate([c, c_ctx[None, :], jnp.zeros((mod_rows - bsz - 1, d), F32)], axis=0)
    mods = _ada(cc, ada_w, ada_b).reshape(depth, mod_rows, N_MOD, d)

    win = ffn_w_in.astype(BF16)
    wout = ffn_w_out.astype(BF16)
    row2 = lambda a: a.reshape(1, -1)
    cos, sin = _rope_tables(n_lat)

    m_i = mods[0]
    h = _ffn(tk, (x.reshape(bsz * n_lat, d), ctx.reshape(bsz * n_ctx, d)), m_i, tk.tiles, win[0, 0], wout[0, 0],
             row2(ln_g[0, 0]), row2(ln_b[0, 0]), 0, alpha)
    dh = ATT_HEAD_DIM
    nqk = (ATT_HEADS + ATT_KV_HEADS) * dh
    order = _rope_column_order()
    wqk = att_w_in[0][:, :nqk].reshape(d, ATT_HEADS + ATT_KV_HEADS, dh)[:, :, order].reshape(d, nqk).astype(BF16)
    wvt = att_w_in[0][:, nqk:].T.astype(BF16)
    q, k, vt = _qkv(tk, h, m_i, wqk, wvt, row2(att_q_gain[0][order]), row2(att_k_gain[0][order]), cos, sin)
    o = _attention(tk, q, k, vt)
    h = _post(tk, o, h, m_i, att_w_out[0].astype(BF16), row2(ln_g[0, 1]), row2(ln_b[0, 1]), alpha)
    h = _ffn(tk, h, m_i, tk.tiles, win[0, 1], wout[0, 1], row2(ln_g[0, 2]), row2(ln_b[0, 2]), 6, alpha)

    m_i = mods[1]
    h = _ffn(tk, h, m_i, tk.tiles, win[1, 0], wout[1, 0], row2(ln_g[1, 0]), row2(ln_b[1, 0]), 0, alpha)
    hq = ML_HEADS * ML_QK
    w = ml_w_in[0]
    wqk = w[:, :2 * hq].astype(BF16)
    wv = w[:, 2 * hq:2 * hq + d].astype(BF16)
    wo = w[:, 2 * hq + d:2 * hq + 2 * d].astype(BF16)
    perm = jnp.array([0, 2, 1, 3])
    wg = w[:, 2 * hq + 2 * d:].reshape(d, 4, ML_HEADS)[:, perm].reshape(d, 4 * ML_HEADS)
    wg = jnp.pad(wg, ((0, 0), (0, 128 - 4 * ML_HEADS))).astype(BF16)
    gb = ml_gate_b[0].reshape(4, ML_HEADS)[perm].reshape(1, 4 * ML_HEADS)
    gb = jnp.pad(gb, ((0, 0), (0, 128 - 4 * ML_HEADS)))
    q, k, v, sig, gc, at = _mlproj(tk, h, m_i, wqk, wv, wo, wg, gb, ml_conv_w[0], row2(ml_conv_b[0]))
    hs = _scan(tk, q, k, v, gc, at)
    h = _readout(tk, hs, sig, h, m_i, row2(ml_norm_g[0]), ml_w_out[0].astype(BF16),
                 row2(ln_g[1, 1]), row2(ln_b[1, 1]), alpha)
    h = _ffn(tk, h, m_i, tk.lat_tiles, win[1, 1], wout[1, 1], row2(ln_g[1, 2]), row2(ln_b[1, 2]), 6, alpha)
    return h.reshape(bsz, n_lat, d)
```

```python
import functools

import jax
import jax.numpy as jnp
from jax import lax
from jax.experimental import pallas as pl
from jax.experimental.pallas import tpu as pltpu

F32 = jnp.float32
BF16 = jnp.bfloat16

N_MOD = 9
GRID_W = 64
ATT_HEADS = 8
ATT_KV_HEADS = 2
ATT_GROUP = ATT_HEADS // ATT_KV_HEADS
ATT_HEAD_DIM = 128
ROPE_PAIRS = ATT_HEAD_DIM // 4
ROPE_THETA = 10000.0
ML_HEADS = 4
ML_QK = 128
ML_V = 256
ML_UNITS = 2 * ML_HEADS
LN_EPS = 1e-5
RMS_EPS = 1e-6
LOG2_E = 1.4426950408889634

TOKEN_TILE = 512
ATT_Q_TILE = 256
ATT_KEY_BLOCK = 256
FF_CHUNK = 256
SCAN_CHUNK = 128
HALO = 8
VMEM_LIMIT = 56 * 1024 * 1024


def _params(*sem):
    return pltpu.CompilerParams(dimension_semantics=sem, vmem_limit_bytes=VMEM_LIMIT)


def _resident(shape):
    n = len(shape)
    return pl.BlockSpec(shape, lambda *_: (0,) * n, pipeline_mode=pl.Buffered(1))


def _layer_norm(x, g, b):
    mu = jnp.mean(x, axis=-1, keepdims=True)
    xc = x - mu
    var = jnp.mean(xc * xc, axis=-1, keepdims=True)
    return xc * lax.rsqrt(var + LN_EPS) * g + b


def _silu(x):
    return x * jax.nn.sigmoid(x)


def _modulated(h, mod_ref, k0):
    return h * (1.0 + mod_ref[k0 + 1:k0 + 2, :]) + mod_ref[k0:k0 + 1, :]


class _Tokens:
    def __init__(self, bsz, n_lat, n_ctx):
        tm = TOKEN_TILE
        assert n_lat % tm == 0 and (bsz * n_ctx) % tm == 0 and tm % n_ctx == 0
        self.bsz, self.n_lat, self.n_ctx = bsz, n_lat, n_ctx
        self.lat_rows = bsz * n_lat
        self.rows = bsz * (n_lat + n_ctx)
        self.tiles_per_seq = n_lat // tm
        self.lat_tiles = self.lat_rows // tm
        self.tiles = self.rows // tm

    def is_lat(self, t):
        return t < self.lat_tiles

    def mod_row(self, t):
        return jnp.where(t < self.lat_tiles, t // self.tiles_per_seq, self.bsz)

    def mod_spec(self, d):
        return pl.BlockSpec((None, N_MOD, d), lambda t: (self.mod_row(t), 0, 0))


def _tok_spec(width):
    return pl.BlockSpec((TOKEN_TILE, width), lambda t: (t, 0))


def _ada_kernel(c_ref, w_ref, b_ref, o_ref):
    s = _silu(c_ref[...]).astype(BF16)
    o_ref[...] = jnp.dot(s, w_ref[...].astype(BF16), preferred_element_type=F32) + b_ref[...]


def _ada(cc, ada_w, ada_b):
    depth, d, nd = ada_w.shape
    rows = cc.shape[0]
    return pl.pallas_call(
        _ada_kernel,
        grid=(depth, nd // d),
        in_specs=[
            pl.BlockSpec((rows, d), lambda i, j: (0, 0)),
            pl.BlockSpec((None, d, d), lambda i, j: (i, 0, j)),
            pl.BlockSpec((None, 1, d), lambda i, j: (i, 0, j)),
        ],
        out_specs=pl.BlockSpec((None, rows, d), lambda i, j: (i, 0, j)),
        out_shape=jax.ShapeDtypeStruct((depth, rows, nd), F32),
        compiler_params=_params("parallel", "parallel"),
        name="ada_mod",
    )(cc, ada_w, ada_b.reshape(depth, 1, nd))


def _ffn_kernel(*refs, k0, alpha, dff, lat_tiles, split_input):
    if split_input:
        hx_ref, hc_ref, mod_ref, win_ref, wout_ref, g_ref, b_ref, o_ref = refs
        h = jnp.where(pl.program_id(0) < lat_tiles, hx_ref[...], hc_ref[...])
    else:
        h_ref, mod_ref, win_ref, wout_ref, g_ref, b_ref, o_ref = refs
        h = h_ref[...]
    xm = _modulated(h, mod_ref, k0).astype(BF16)
    acc = None
    for j in range(dff // FF_CHUNK):
        cols = slice(FF_CHUNK * j, FF_CHUNK * (j + 1))
        a = jnp.dot(xm, win_ref[:, cols], preferred_element_type=F32)
        u = jnp.dot(xm, win_ref[:, dff + FF_CHUNK * j:dff + FF_CHUNK * (j + 1)], preferred_element_type=F32)
        hid = (_silu(a) * u).astype(BF16)
        y = jnp.dot(hid, wout_ref[cols, :], preferred_element_type=F32)
        acc = y if acc is None else acc + y
    gate = mod_ref[k0 + 2:k0 + 3, :]
    o_ref[...] = _layer_norm(alpha * h + (0.5 * gate) * acc, g_ref[...], b_ref[...])


def _ffn(tk, hs, mods, n_tiles, win, wout, g, b, k0, alpha):
    split = isinstance(hs, tuple)
    d, dff = wout.shape[1], wout.shape[0]
    tm = TOKEN_TILE
    if split:
        last_lat = tk.lat_tiles - 1
        h_specs = [pl.BlockSpec((tm, d), lambda t: (jnp.minimum(t, last_lat), 0)),
                   pl.BlockSpec((tm, d), lambda t: (jnp.maximum(t - tk.lat_tiles, 0), 0))]
        hs = list(hs)
    else:
        h_specs = [_tok_spec(d)]
        hs = [hs]
    kern = functools.partial(_ffn_kernel, k0=k0, alpha=alpha, dff=dff, lat_tiles=tk.lat_tiles, split_input=split)
    return pl.pallas_call(
        kern,
        grid=(n_tiles,),
        in_specs=h_specs + [tk.mod_spec(d), _resident(win.shape), _resident(wout.shape),
                            _resident((1, d)), _resident((1, d))],
        out_specs=_tok_spec(d),
        out_shape=jax.ShapeDtypeStruct((n_tiles * tm, d), F32),
        compiler_params=_params("parallel"),
        name="half_ffn",
    )(*hs, mods, win, wout, g, b)


def _qkv_kernel(h_ref, mod_ref, wqk_ref, wvt_ref, qg_ref, kg_ref, ct_ref, st_ref, q_ref, k_ref, vt_ref):
    xm = _modulated(h_ref[...], mod_ref, 3).astype(BF16)
    ct = ct_ref[...]
    st = st_ref[...]
    dh = ATT_HEAD_DIM

    def norm_rope(x, gain):
        xn = x * lax.rsqrt(jnp.mean(x * x, axis=-1, keepdims=True) + RMS_EPS) * gain
        return xn * ct + pltpu.roll(xn, dh // 2, 1) * st

    qg = qg_ref[...] * (dh ** -0.5 * LOG2_E)
    kg = kg_ref[...]
    for pair in range((ATT_HEADS + ATT_KV_HEADS) // 2):
        p = jnp.dot(xm, wqk_ref[:, 2 * pair * dh:2 * (pair + 1) * dh], preferred_element_type=F32)
        for half in range(2):
            hd = 2 * pair + half
            x = p[:, half * dh:(half + 1) * dh]
            if hd < ATT_HEADS:
                q_ref[:, hd * dh:(hd + 1) * dh] = norm_rope(x, qg).astype(BF16)
            else:
                hk = hd - ATT_HEADS
                k_ref[:, hk * dh:(hk + 1) * dh] = norm_rope(x, kg).astype(BF16)
    vt = lax.dot_general(wvt_ref[...], xm, (((1,), (1,)), ((), ())), preferred_element_type=F32)
    vt_ref[...] = vt.astype(BF16)


def _qkv(tk, h, mods, wqk, wvt, qg, kg, ctab, stab):
    d = h.shape[1]
    tm = TOKEN_TILE
    dh = ATT_HEAD_DIM
    nq, nk = ATT_HEADS * dh, ATT_KV_HEADS * dh
    tab_spec = pl.BlockSpec((tm, dh), lambda t: (jnp.where(tk.is_lat(t), t % tk.tiles_per_seq, tk.tiles_per_seq), 0))
    return pl.pallas_call(
        _qkv_kernel,
        grid=(tk.tiles,),
        in_specs=[_tok_spec(d), tk.mod_spec(d), _resident(wqk.shape), _resident(wvt.shape),
                  _resident((1, dh)), _resident((1, dh)), tab_spec, tab_spec],
        out_specs=[_tok_spec(nq), _tok_spec(nk), pl.BlockSpec((nk, tm), lambda t: (0, t))],
        out_shape=[jax.ShapeDtypeStruct((tk.rows, nq), BF16),
                   jax.ShapeDtypeStruct((tk.rows, nk), BF16),
                   jax.ShapeDtypeStruct((nk, tk.rows), BF16)],
        compiler_params=_params("parallel"),
        name="att_qkv",
    )(h, mods, wqk, wvt, qg, kg, ctab, stab)


def _attn_kernel(q_ref, kl_ref, kc_ref, vlt_ref, vct_ref, o_ref, s_scr, *, lat_q_tiles):
    dh = ATT_HEAD_DIM
    kb = ATT_KEY_BLOCK
    tq = q_ref.shape[0]
    n_lat, n_ctx = kl_ref.shape[0], kc_ref.shape[0]
    nt = (((1,), (1,)), ((), ()))

    def key_blocks(with_lat):
        blocks = []
        if with_lat:
            blocks += [(kl_ref, vlt_ref, b * kb, b * kb) for b in range(n_lat // kb)]
        base = n_lat if with_lat else 0
        blocks += [(kc_ref, vct_ref, b * kb, base + b * kb) for b in range(n_ctx // kb)]
        return blocks

    def scores(g, blocks):
        q = q_ref[:, g * dh:(g + 1) * dh]
        m8 = None
        for k_ref, _, r0, s0 in blocks:
            s = lax.dot_general(k_ref[r0:r0 + kb, :], q, nt, preferred_element_type=F32)
            s_scr[g % 2, s0:s0 + kb, :] = s
            bm = jnp.max(s.reshape(kb // 8, 8, tq), axis=0)
            m8 = bm if m8 is None else jnp.maximum(m8, bm)
        return jnp.max(m8, axis=0, keepdims=True)

    def weighted_values(g, blocks, m):
        l8 = None
        ot = None
        for _, vt_ref, r0, s0 in blocks:
            p = jnp.exp2(s_scr[g % 2, s0:s0 + kb, :] - m)
            ps = jnp.sum(p.reshape(kb // 8, 8, tq), axis=0)
            l8 = ps if l8 is None else l8 + ps
            o_blk = jnp.dot(vt_ref[:, r0:r0 + kb], p.astype(BF16), preferred_element_type=F32)
            ot = o_blk if ot is None else ot + o_blk
        denom = jnp.sum(l8, axis=0, keepdims=True)
        o_ref[:, g * dh:(g + 1) * dh] = jnp.transpose(ot / denom).astype(BF16)

    def run(blocks):
        m = scores(0, blocks)
        for g in range(ATT_GROUP):
            m_next = scores(g + 1, blocks) if g + 1 < ATT_GROUP else None
            weighted_values(g, blocks, m)
            m = m_next

    t = pl.program_id(2)

    @pl.when(t < lat_q_tiles)
    def _():
        run(key_blocks(True))

    @pl.when(t >= lat_q_tiles)
    def _():
        run(key_blocks(False))


def _attention(tk, q, k, vt):
    tq = ATT_Q_TILE
    dh = ATT_HEAD_DIM
    gw = ATT_GROUP * dh
    n_lat, n_ctx = tk.n_lat, tk.n_ctx
    assert n_lat % tq == 0 and n_ctx % tq == 0
    lat_q_tiles = n_lat // tq
    q_tiles = lat_q_tiles + n_ctx // tq
    ctx_block0 = tk.lat_rows // n_ctx

    def q_row(i, t):
        return jnp.where(t < lat_q_tiles, i * lat_q_tiles + t,
                         tk.lat_rows // tq + i * (n_ctx // tq) + (t - lat_q_tiles))

    q_spec = pl.BlockSpec((tq, gw), lambda i, j, t: (q_row(i, t), j))
    return pl.pallas_call(
        functools.partial(_attn_kernel, lat_q_tiles=lat_q_tiles),
        grid=(tk.bsz, ATT_KV_HEADS, q_tiles),
        in_specs=[q_spec,
                  pl.BlockSpec((n_lat, dh), lambda i, j, t: (i, j)),
                  pl.BlockSpec((n_ctx, dh), lambda i, j, t: (ctx_block0 + i, j)),
                  pl.BlockSpec((dh, n_lat), lambda i, j, t: (j, i)),
                  pl.BlockSpec((dh, n_ctx), lambda i, j, t: (j, ctx_block0 + i))],
        out_specs=q_spec,
        out_shape=jax.ShapeDtypeStruct(q.shape, BF16),
        scratch_shapes=[pltpu.VMEM((2, n_lat + n_ctx, tq), F32)],
        compiler_params=_params("parallel", "parallel", "parallel"),
        name="att_core",
    )(q, k, k, vt, vt)


def _post_kernel(o_ref, h_ref, mod_ref, w_ref, g_ref, b_ref, out_ref, *, alpha):
    y = jnp.dot(o_ref[...], w_ref[...], preferred_element_type=F32)
    out_ref[...] = _layer_norm(alpha * h_ref[...] + mod_ref[5:6, :] * y, g_ref[...], b_ref[...])


def _post(tk, o, h, mods, w, g, b, alpha):
    d = h.shape[1]
    return pl.pallas_call(
        functools.partial(_post_kernel, alpha=alpha),
        grid=(tk.tiles,),
        in_specs=[_tok_spec(d), _tok_spec(d), tk.mod_spec(d), _resident(w.shape), _resident((1, d)), _resident((1, d))],
        out_specs=_tok_spec(d),
        out_shape=jax.ShapeDtypeStruct(h.shape, F32),
        compiler_params=_params("parallel"),
        name="att_post",
    )(o, h, mods, w, g, b)


def _mlproj_kernel(h_ref, hp_ref, hn_ref, mod_ref, wqk_ref, wv_ref, wo_ref, wg_ref, gb_ref, cw_ref, cb_ref,
                   q_ref, kt_ref, v_ref, sig_ref, gc_ref, at_ref, xm_scr, *, tk):
    tm = TOKEN_TILE
    lc = SCAN_CHUNK
    t = pl.program_id(0)
    xm = _modulated(h_ref[...], mod_ref, 3).astype(BF16)
    xm_scr[0:tm, :] = xm
    xm_scr[tm:tm + HALO, :] = _modulated(hp_ref[...], mod_ref, 3).astype(BF16)
    xm_scr[tm + HALO:tm + 2 * HALO, :] = _modulated(hn_ref[...], mod_ref, 3).astype(BF16)

    row = lax.broadcasted_iota(jnp.int32, (tm, FF_CHUNK), 0)
    pos_in_seq = t % tk.tiles_per_seq
    lat_first = jnp.logical_and(tk.is_lat(t), pos_in_seq == 0)
    lat_last = jnp.logical_and(tk.is_lat(t), pos_in_seq == tk.tiles_per_seq - 1)
    is_ctx = jnp.logical_not(tk.is_lat(t))
    seq_first = jnp.logical_or(jnp.logical_and(row == 0, lat_first),
                               jnp.logical_and(row % tk.n_ctx == 0, is_ctx))
    seq_last = jnp.logical_or(jnp.logical_and(row == tm - 1, lat_last),
                              jnp.logical_and(row % tk.n_ctx == tk.n_ctx - 1, is_ctx))

    hq = ML_HEADS * ML_QK
    for c in range(2 * hq // FF_CHUNK):
        cols = slice(c * FF_CHUNK, (c + 1) * FF_CHUNK)
        p = jnp.dot(xm_scr[...], wqk_ref[:, cols], preferred_element_type=F32)
        main = p[0:tm]
        prev_row = p[tm + HALO - 1:tm + HALO]
        next_row = p[tm + HALO:tm + HALO + 1]
        down = jnp.where(row == 0, prev_row, pltpu.roll(main, 1, 0))
        up = jnp.where(row == tm - 1, next_row, pltpu.roll(main, tm - 1, 0))
        down = jnp.where(seq_first, 0.0, down)
        up = jnp.where(seq_last, 0.0, up)
        conv = down * cw_ref[0:1, cols] + main * cw_ref[1:2, cols] + up * cw_ref[2:3, cols] + cb_ref[:, cols]
        act = _silu(conv)
        if (c + 1) * FF_CHUNK <= hq:
            q_ref[:, cols] = act.astype(BF16)
        else:
            kact = act * (ML_QK ** -0.5)
            for rc in range(tm // lc):
                kt_ref[rc, c * FF_CHUNK - hq:(c + 1) * FF_CHUNK - hq, :] = (
                    jnp.transpose(kact[rc * lc:(rc + 1) * lc, :]).astype(BF16))
        v_ref[:, cols] = jnp.dot(xm, wv_ref[:, cols], preferred_element_type=F32).astype(BF16)
        sig_ref[:, cols] = jax.nn.sigmoid(jnp.dot(xm, wo_ref[:, cols], preferred_element_type=F32)).astype(BF16)

    g = jnp.dot(xm, wg_ref[...], preferred_element_type=F32) + gb_ref[...]
    logf = jax.nn.log_sigmoid(g)
    r_i = lax.broadcasted_iota(jnp.int32, (lc, lc), 0)
    c_i = lax.broadcasted_iota(jnp.int32, (lc, lc), 1)
    tri_lo = (c_i <= r_i).astype(F32)
    tri_hi = (c_i >= r_i).astype(F32)
    lane = lax.broadcasted_iota(jnp.int32, (lc, 128), 1)
    trow = lax.broadcasted_iota(jnp.int32, (lc, 128), 0)
    nh = ML_HEADS
    for c in range(tm // lc):
        rows = slice(c * lc, (c + 1) * lc)
        lf = logf[rows]
        pre = jnp.dot(tri_lo, lf, preferred_element_type=F32, precision=lax.Precision.HIGHEST)
        suf = jnp.dot(tri_hi, lf, preferred_element_type=F32, precision=lax.Precision.HIGHEST)
        bc = pltpu.roll(jnp.where(lane < 3 * nh, pre, suf), 128 - 2 * nh, 1)
        a = g[rows] - bc
        pf = a
        pb = a
        k = 1
        while k < lc:
            pf = jnp.maximum(pf, jnp.where(trow >= k, pltpu.roll(pf, k, 0), -jnp.inf))
            pb = jnp.maximum(pb, jnp.where(trow < lc - k, pltpu.roll(pb, lc - k, 0), -jnp.inf))
            k *= 2
        pm = jnp.where(lane < nh, pf, pb)
        gc_ref[rows, :] = jnp.where(lane < 2 * nh, bc,
                                    jnp.where(lane < 4 * nh, pltpu.roll(a, 2 * nh, 1), pltpu.roll(pm, 4 * nh, 1)))
        at_ref[c] = jnp.transpose(a)[0:ML_UNITS, :]


def _mlproj(tk, h, mods, wqk, wv, wo, wg, gb, cw, cb):
    d = h.shape[1]
    tm = TOKEN_TILE
    lc = SCAN_CHUNK
    hq = ML_HEADS * ML_QK
    blocks_per_tile = tm // HALO
    last_block = tk.rows // HALO - 1
    return pl.pallas_call(
        functools.partial(_mlproj_kernel, tk=tk),
        grid=(tk.tiles,),
        in_specs=[
            _tok_spec(d),
            pl.BlockSpec((HALO, d), lambda t: (jnp.maximum(t * blocks_per_tile - 1, 0), 0)),
            pl.BlockSpec((HALO, d), lambda t: (jnp.minimum((t + 1) * blocks_per_tile, last_block), 0)),
            tk.mod_spec(d),
            _resident(wqk.shape), _resident(wv.shape), _resident(wo.shape), _resident(wg.shape),
            _resident(gb.shape), _resident(cw.shape), _resident(cb.shape),
        ],
        out_specs=[
            _tok_spec(hq), pl.BlockSpec((tm // lc, hq, lc), lambda t: (t, 0, 0)),
            _tok_spec(d), _tok_spec(d), _tok_spec(128),
            pl.BlockSpec((tm // lc, ML_UNITS, lc), lambda t: (t, 0, 0)),
        ],
        out_shape=[
            jax.ShapeDtypeStruct((tk.rows, hq), BF16),
            jax.ShapeDtypeStruct((tk.rows // lc, hq, lc), BF16),
            jax.ShapeDtypeStruct((tk.rows, d), BF16),
            jax.ShapeDtypeStruct((tk.rows, d), BF16),
            jax.ShapeDtypeStruct((tk.rows, 128), F32),
            jax.ShapeDtypeStruct((tk.rows // lc, ML_UNITS, lc), F32),
        ],
        scratch_shapes=[pltpu.VMEM((tm + 2 * HALO, d), BF16)],
        compiler_params=_params("parallel"),
        name="ml_proj",
    )(h, h, h, mods, wqk, wv, wo, wg, gb, cw, cb)


def _scan_kernel(ql_ref, qc_ref, ktl_ref, ktc_ref, vl_ref, vc_ref, gl_ref, gcx_ref, al_ref, ac_ref, o_ref,
                 ct_scr, m_scr, *, n_lat_chunks, n_ctx_chunks):
    lc = SCAN_CHUNK
    dk, dv = ML_QK, ML_V
    assert lc == 128 and dk == 128 and dv == 2 * 128
    ct_scr[...] = jnp.zeros_like(ct_scr)
    m_scr[...] = jnp.zeros_like(m_scr)
    t_i = lax.broadcasted_iota(jnp.int32, (lc, lc), 0)
    s_i = lax.broadcasted_iota(jnp.int32, (lc, lc), 1)
    masks = (s_i <= t_i, s_i >= t_i)
    ones_blk = jnp.ones((lc, 128), BF16)

    def step(j, mode):
        if mode == "ctx":
            q_ref, kt_ref, v_ref, g_ref, a_ref, n_seq = qc_ref, ktc_ref, vc_ref, gcx_ref, ac_ref, n_ctx_chunks
        else:
            q_ref, kt_ref, v_ref, g_ref, a_ref, n_seq = ql_ref, ktl_ref, vl_ref, gl_ref, al_ref, n_lat_chunks
        chunk = (j, n_seq - 1 - j)
        for direction in range(2):
            c = chunk[direction]
            r0 = pl.multiple_of(c * lc, lc)
            gc = g_ref[pl.ds(r0, lc), :]
            at = a_ref[c]
            last = lc - 1 if direction == 0 else 0
            for hd in range(ML_HEADS):
                u = direction * ML_HEADS + hd
                kt = kt_ref[c, hd * dk:(hd + 1) * dk, :]
                vc = v_ref[pl.ds(r0, lc), hd * dv:(hd + 1) * dv]
                v_aug = jnp.concatenate([vc, ones_blk], axis=1)
                bcum = jnp.broadcast_to(gc[:, u:u + 1], (lc, 128))
                pmax = jnp.broadcast_to(gc[:, 2 * ML_UNITS + u:2 * ML_UNITS + u + 1], (lc, 128))
                a_row = at[u:u + 1, :]
                m = m_scr[u, 0:1, :]
                ct = ct_scr[u]
                mu = jnp.maximum(m, pmax)
                if mode != "ctx":
                    qc = q_ref[pl.ds(r0, lc), hd * dk:(hd + 1) * dk]
                    w = jnp.where(masks[direction], jnp.exp(a_row - mu), 0.0)
                    s = jnp.dot(qc, kt, preferred_element_type=F32)
                    eq = jnp.exp(m - mu) * qc.astype(F32)
                    lhs = jnp.concatenate([(s * w).astype(BF16), eq.astype(BF16)], axis=1)
                    rhs = jnp.concatenate([v_aug, ct.astype(BF16)], axis=0)
                    num = jnp.dot(lhs, rhs, preferred_element_type=F32)
                    inv = 1.0 / jnp.maximum(jnp.abs(num[:, dv:]), jnp.exp(-(bcum + mu)))
                    hout = num[:, :dv] * jnp.concatenate([inv, inv], axis=1)
                    if mode == "store":
                        o_ref[pl.ds(r0, lc), hd * dv:(hd + 1) * dv] = hout
                    else:
                        o_ref[pl.ds(r0, lc), hd * dv:(hd + 1) * dv] += hout
                mu_l = mu[last:last + 1, :]
                ktw = (kt.astype(F32) * jnp.exp(a_row - mu_l)).astype(BF16)
                upd = jnp.dot(ktw, v_aug, preferred_element_type=F32)
                decay = jnp.exp(m - mu_l)
                ct_scr[u] = ct * jnp.concatenate([decay, decay, decay], axis=1) + upd
                m_scr[u, 0:1, :] = bcum[last:last + 1, :] + mu_l

    half = n_lat_chunks // 2
    lax.fori_loop(0, n_ctx_chunks, lambda j, _: step(j, "ctx"), None)
    lax.fori_loop(0, half, lambda j, _: step(j, "store"), None)
    lax.fori_loop(half, n_lat_chunks, lambda j, _: step(j, "add"), None)


def _scan(tk, q, kt, v, gc, at):
    d = v.shape[1]
    lc = SCAN_CHUNK
    hq = ML_HEADS * ML_QK
    n_lat, n_ctx = tk.n_lat, tk.n_ctx
    n_lat_chunks, n_ctx_chunks = n_lat // lc, n_ctx // lc
    assert n_lat_chunks % 2 == 0
    ctx0 = tk.lat_rows // n_ctx
    ctx_chunk0 = tk.lat_rows // lc // n_ctx_chunks
    lat = lambda width: pl.BlockSpec((n_lat, width), lambda i: (i, 0))
    ctx = lambda width: pl.BlockSpec((n_ctx, width), lambda i: (ctx0 + i, 0))
    lat_chunks = lambda rows: pl.BlockSpec((n_lat_chunks, rows, lc), lambda i: (i, 0, 0))
    ctx_chunks = lambda rows: pl.BlockSpec((n_ctx_chunks, rows, lc), lambda i: (ctx_chunk0 + i, 0, 0))
    kern = functools.partial(_scan_kernel, n_lat_chunks=n_lat_chunks, n_ctx_chunks=n_ctx_chunks)
    return pl.pallas_call(
        kern,
        grid=(tk.bsz,),
        in_specs=[
            lat(hq), ctx(hq), lat_chunks(hq), ctx_chunks(hq), lat(d), ctx(d), lat(128), ctx(128),
            lat_chunks(ML_UNITS), ctx_chunks(ML_UNITS),
        ],
        out_specs=lat(d),
        out_shape=jax.ShapeDtypeStruct((tk.lat_rows, d), F32),
        scratch_shapes=[
            pltpu.VMEM((ML_UNITS, ML_QK, ML_V + 128), F32),
            pltpu.VMEM((ML_UNITS, 8, 128), F32),
        ],
        compiler_params=_params("parallel"),
        name="ml_scan",
    )(q, q, kt, kt, v, v, gc, gc, at, at)


def _readout_kernel(hs_ref, sig_ref, h_ref, mod_ref, ng_ref, w_ref, g_ref, b_ref, out_ref, *, alpha):
    dv = ML_V
    parts = []
    for hd in range(ML_HEADS):
        x = hs_ref[:, hd * dv:(hd + 1) * dv]
        mu = jnp.mean(x, axis=-1, keepdims=True)
        xc = x - mu
        var = jnp.mean(xc * xc, axis=-1, keepdims=True)
        parts.append(xc * lax.rsqrt(var + LN_EPS))
    hn = jnp.concatenate(parts, axis=1) * ng_ref[...]
    z = (hn * sig_ref[...].astype(F32)).astype(BF16)
    y = jnp.dot(z, w_ref[...], preferred_element_type=F32)
    out_ref[...] = _layer_norm(alpha * h_ref[...] + mod_ref[5:6, :] * y, g_ref[...], b_ref[...])


def _readout(tk, hs, sig, h, mods, ng, w, g, b, alpha):
    d = hs.shape[1]
    return pl.pallas_call(
        functools.partial(_readout_kernel, alpha=alpha),
        grid=(tk.lat_tiles,),
        in_specs=[_tok_spec(d), _tok_spec(d), _tok_spec(d), tk.mod_spec(d),
                  _resident((1, d)), _resident(w.shape), _resident((1, d)), _resident((1, d))],
        out_specs=_tok_spec(d),
        out_shape=jax.ShapeDtypeStruct(hs.shape, F32),
        compiler_params=_params("parallel"),
        name="ml_readout",
    )(hs, sig, h, mods, ng, w, g, b)


def _rope_tables(n_lat):
    rows = n_lat // GRID_W
    row = jnp.repeat(jnp.arange(rows, dtype=jnp.int32), GRID_W).astype(F32)
    col = jnp.tile(jnp.arange(GRID_W, dtype=jnp.int32), rows).astype(F32)
    inv = ROPE_THETA ** (-jnp.arange(ROPE_PAIRS, dtype=F32) / ROPE_PAIRS)
    ar, ac = row[:, None] * inv, col[:, None] * inv
    cos = jnp.concatenate([jnp.cos(ar), jnp.cos(ac), jnp.cos(ar), jnp.cos(ac)], axis=1)
    sin = jnp.concatenate([-jnp.sin(ar), -jnp.sin(ac), jnp.sin(ar), jnp.sin(ac)], axis=1)
    cos = jnp.concatenate([cos, jnp.ones((TOKEN_TILE, ATT_HEAD_DIM), F32)], axis=0)
    sin = jnp.concatenate([sin, jnp.zeros((TOKEN_TILE, ATT_HEAD_DIM), F32)], axis=0)
    return cos, sin


def _rope_column_order():
    p = ROPE_PAIRS
    return jnp.concatenate([jnp.arange(0, p), jnp.arange(2 * p, 3 * p), jnp.arange(p, 2 * p), jnp.arange(3 * p, 4 * p)])


def kernel(x, c, ctx, c_ctx, ada_w, ada_b, ln_g, ln_b, ffn_w_in, ffn_w_out, att_w_in, att_q_gain, att_k_gain,
           att_w_out, ml_w_in, ml_gate_b, ml_conv_w, ml_conv_b, ml_norm_g, ml_w_out):
    bsz, n_lat, d = x.shape
    n_ctx = ctx.shape[1]
    depth = ada_w.shape[0]
    dff = ffn_w_out.shape[2]
    assert depth == 2, "layer 0 attention, layer 1 (last) mLSTM"
    assert dff % FF_CHUNK == 0 and TOKEN_TILE % SCAN_CHUNK == 0
    tk = _Tokens(bsz, n_lat, n_ctx)
    alpha = (2.0 * depth) ** 0.25

    mod_rows = -(-(bsz + 1) // 8) * 8
    cc = jnp.concatenate([c, c_ctx[None, :], jnp.zeros((mod_rows - bsz - 1, d), F32)], axis=0)
    mods = _ada(cc, ada_w, ada_b).reshape(depth, mod_rows, N_MOD, d)

    win = ffn_w_in.astype(BF16)
    wout = ffn_w_out.astype(BF16)
    row2 = lambda a: a.reshape(1, -1)
    cos, sin = _rope_tables(n_lat)

    m_i = mods[0]
    h = _ffn(tk, (x.reshape(bsz * n_lat, d), ctx.reshape(bsz * n_ctx, d)), m_i, tk.tiles, win[0, 0], wout[0, 0],
             row2(ln_g[0, 0]), row2(ln_b[0, 0]), 0, alpha)
    dh = ATT_HEAD_DIM
    nqk = (ATT_HEADS + ATT_KV_HEADS) * dh
    order = _rope_column_order()
    wqk = att_w_in[0][:, :nqk].reshape(d, ATT_HEADS + ATT_KV_HEADS, dh)[:, :, order].reshape(d, nqk).astype(BF16)
    wvt = att_w_in[0][:, nqk:].T.astype(BF16)
    q, k, vt = _qkv(tk, h, m_i, wqk, wvt, row2(att_q_gain[0][order]), row2(att_k_gain[0][order]), cos, sin)
    o = _attention(tk, q, k, vt)
    h = _post(tk, o, h, m_i, att_w_out[0].astype(BF16), row2(ln_g[0, 1]), row2(ln_b[0, 1]), alpha)
    h = _ffn(tk, h, m_i, tk.tiles, win[0, 1], wout[0, 1], row2(ln_g[0, 2]), row2(ln_b[0, 2]), 6, alpha)

    m_i = mods[1]
    h = _ffn(tk, h, m_i, tk.tiles, win[1, 0], wout[1, 0], row2(ln_g[1, 0]), row2(ln_b[1, 0]), 0, alpha)
    hq = ML_HEADS * ML_QK
    w = ml_w_in[0]
    wqk = w[:, :2 * hq].astype(BF16)
    wv = w[:, 2 * hq:2 * hq + d].astype(BF16)
    wo = w[:, 2 * hq + d:2 * hq + 2 * d].astype(BF16)
    perm = jnp.array([0, 2, 1, 3])
    wg = w[:, 2 * hq + 2 * d:].reshape(d, 4, ML_HEADS)[:, perm].reshape(d, 4 * ML_HEADS)
    wg = jnp.pad(wg, ((0, 0), (0, 128 - 4 * ML_HEADS))).astype(BF16)
    gb = ml_gate_b[0].reshape(4, ML_HEADS)[perm].reshape(1, 4 * ML_HEADS)
    gb = jnp.pad(gb, ((0, 0), (0, 128 - 4 * ML_HEADS)))
    q, kt, v, sig, gc, at = _mlproj(tk, h, m_i, wqk, wv, wo, wg, gb, ml_conv_w[0], row2(ml_conv_b[0]))
    hs = _scan(tk, q, kt, v, gc, at)
    h = _readout(tk, hs, sig, h, m_i, row2(ml_norm_g[0]), ml_w_out[0].astype(BF16),
                 row2(ln_g[1, 1]), row2(ln_b[1, 1]), alpha)
    h = _ffn(tk, h, m_i, tk.lat_tiles, win[1, 1], wout[1, 1], row2(ln_g[1, 2]), row2(ln_b[1, 2]), 6, alpha)
    return h.reshape(bsz, n_lat, d)
```

```python
import functools

import jax
import jax.numpy as jnp
from jax import lax
from jax.experimental import pallas as pl
from jax.experimental.pallas import tpu as pltpu

F32 = jnp.float32
BF16 = jnp.bfloat16

N_MOD = 9
GRID_W = 64
ATT_HEADS = 8
ATT_KV_HEADS = 2
ATT_GROUP = ATT_HEADS // ATT_KV_HEADS
ATT_HEAD_DIM = 128
ROPE_PAIRS = ATT_HEAD_DIM // 4
ROPE_THETA = 10000.0
ML_HEADS = 4
ML_QK = 128
ML_V = 256
ML_UNITS = 2 * ML_HEADS
LN_EPS = 1e-5
RMS_EPS = 1e-6
LOG2_E = 1.4426950408889634

TOKEN_TILE = 512
ATT_Q_TILE = 256
ATT_KEY_BLOCK = 256
FF_CHUNK = 256
SCAN_CHUNK = 128
HALO = 8
VMEM_LIMIT = 56 * 1024 * 1024


def _params(*sem):
    return pltpu.CompilerParams(dimension_semantics=sem, vmem_limit_bytes=VMEM_LIMIT)


def _resident(shape):
    n = len(shape)
    return pl.BlockSpec(shape, lambda *_: (0,) * n, pipeline_mode=pl.Buffered(1))


def _layer_norm(x, g, b):
    mu = jnp.mean(x, axis=-1, keepdims=True)
    xc = x - mu
    var = jnp.mean(xc * xc, axis=-1, keepdims=True)
    return xc * lax.rsqrt(var + LN_EPS) * g + b


def _silu(x):
    return x * jax.nn.sigmoid(x)


def _modulated(h, mod_ref, k0):
    return h * (1.0 + mod_ref[k0 + 1:k0 + 2, :]) + mod_ref[k0:k0 + 1, :]


class _Tokens:
    def __init__(self, bsz, n_lat, n_ctx):
        tm = TOKEN_TILE
        assert n_lat % tm == 0 and (bsz * n_ctx) % tm == 0 and tm % n_ctx == 0
        self.bsz, self.n_lat, self.n_ctx = bsz, n_lat, n_ctx
        self.lat_rows = bsz * n_lat
        self.rows = bsz * (n_lat + n_ctx)
        self.tiles_per_seq = n_lat // tm
        self.lat_tiles = self.lat_rows // tm
        self.tiles = self.rows // tm

    def is_lat(self, t):
        return t < self.lat_tiles

    def mod_row(self, t):
        return jnp.where(t < self.lat_tiles, t // self.tiles_per_seq, self.bsz)

    def mod_spec(self, d):
        return pl.BlockSpec((None, N_MOD, d), lambda t: (self.mod_row(t), 0, 0))


def _tok_spec(width):
    return pl.BlockSpec((TOKEN_TILE, width), lambda t: (t, 0))


def _ada_kernel(c_ref, w_ref, b_ref, o_ref):
    s = _silu(c_ref[...]).astype(BF16)
    o_ref[...] = jnp.dot(s, w_ref[...].astype(BF16), preferred_element_type=F32) + b_ref[...]


def _ada(cc, ada_w, ada_b):
    depth, d, nd = ada_w.shape
    rows = cc.shape[0]
    return pl.pallas_call(
        _ada_kernel,
        grid=(depth, nd // d),
        in_specs=[
            pl.BlockSpec((rows, d), lambda i, j: (0, 0)),
            pl.BlockSpec((None, d, d), lambda i, j: (i, 0, j)),
            pl.BlockSpec((None, 1, d), lambda i, j: (i, 0, j)),
        ],
        out_specs=pl.BlockSpec((None, rows, d), lambda i, j: (i, 0, j)),
        out_shape=jax.ShapeDtypeStruct((depth, rows, nd), F32),
        compiler_params=_params("parallel", "parallel"),
        name="ada_mod",
    )(cc, ada_w, ada_b.reshape(depth, 1, nd))


def _ffn_kernel(*refs, k0, alpha, dff, lat_tiles, split_input):
    if split_input:
        hx_ref, hc_ref, mod_ref, win_ref, wout_ref, g_ref, b_ref, o_ref = refs
        h = jnp.where(pl.program_id(0) < lat_tiles, hx_ref[...], hc_ref[...])
    else:
        h_ref, mod_ref, win_ref, wout_ref, g_ref, b_ref, o_ref = refs
        h = h_ref[...]
    xm = _modulated(h, mod_ref, k0).astype(BF16)
    acc = None
    for j in range(dff // FF_CHUNK):
        cols = slice(FF_CHUNK * j, FF_CHUNK * (j + 1))
        a = jnp.dot(xm, win_ref[:, cols], preferred_element_type=F32)
        u = jnp.dot(xm, win_ref[:, dff + FF_CHUNK * j:dff + FF_CHUNK * (j + 1)], preferred_element_type=F32)
        hid = (_silu(a) * u).astype(BF16)
        y = jnp.dot(hid, wout_ref[cols, :], preferred_element_type=F32)
        acc = y if acc is None else acc + y
    gate = mod_ref[k0 + 2:k0 + 3, :]
    o_ref[...] = _layer_norm(alpha * h + (0.5 * gate) * acc, g_ref[...], b_ref[...])


def _ffn(tk, hs, mods, n_tiles, win, wout, g, b, k0, alpha):
    split = isinstance(hs, tuple)
    d, dff = wout.shape[1], wout.shape[0]
    tm = TOKEN_TILE
    if split:
        last_lat = tk.lat_tiles - 1
        h_specs = [pl.BlockSpec((tm, d), lambda t: (jnp.minimum(t, last_lat), 0)),
                   pl.BlockSpec((tm, d), lambda t: (jnp.maximum(t - tk.lat_tiles, 0), 0))]
        hs = list(hs)
    else:
        h_specs = [_tok_spec(d)]
        hs = [hs]
    kern = functools.partial(_ffn_kernel, k0=k0, alpha=alpha, dff=dff, lat_tiles=tk.lat_tiles, split_input=split)
    return pl.pallas_call(
        kern,
        grid=(n_tiles,),
        in_specs=h_specs + [tk.mod_spec(d), _resident(win.shape), _resident(wout.shape),
                            _resident((1, d)), _resident((1, d))],
        out_specs=_tok_spec(d),
        out_shape=jax.ShapeDtypeStruct((n_tiles * tm, d), F32),
        compiler_params=_params("parallel"),
        name="half_ffn",
    )(*hs, mods, win, wout, g, b)


def _qkv_kernel(h_ref, mod_ref, wqk_ref, wvt_ref, qg_ref, kg_ref, ct_ref, st_ref, q_ref, k_ref, vt_ref):
    xm = _modulated(h_ref[...], mod_ref, 3).astype(BF16)
    ct = ct_ref[...]
    st = st_ref[...]
    dh = ATT_HEAD_DIM

    ones_w = jnp.ones((2 * dh, dh), BF16)

    def norm_rope(x, gain):
        sq = x * x
        hi = sq.astype(BF16)
        lo = (sq - hi.astype(F32)).astype(BF16)
        ssq = jnp.dot(jnp.concatenate([hi, lo], axis=1), ones_w, preferred_element_type=F32)
        xn = x * lax.rsqrt(ssq * (1.0 / dh) + RMS_EPS) * gain
        return xn * ct + pltpu.roll(xn, dh // 2, 1) * st

    qg = qg_ref[...] * (dh ** -0.5 * LOG2_E)
    kg = kg_ref[...]
    n_pairs = (ATT_HEADS + ATT_KV_HEADS) // 2
    pair_dot = lambda i: jnp.dot(xm, wqk_ref[:, 2 * i * dh:2 * (i + 1) * dh], preferred_element_type=F32)
    pending = pair_dot(0)
    for pair in range(n_pairs):
        p = pending
        if pair + 1 < n_pairs:
            pending = pair_dot(pair + 1)
        for half in range(2):
            hd = 2 * pair + half
            x = p[:, half * dh:(half + 1) * dh]
            if hd < ATT_HEADS:
                q_ref[:, hd * dh:(hd + 1) * dh] = norm_rope(x, qg).astype(BF16)
            else:
                hk = hd - ATT_HEADS
                k_ref[:, hk * dh:(hk + 1) * dh] = norm_rope(x, kg).astype(BF16)
    vt = lax.dot_general(wvt_ref[...], xm, (((1,), (1,)), ((), ())), preferred_element_type=F32)
    vt_ref[...] = vt.astype(BF16)


def _qkv(tk, h, mods, wqk, wvt, qg, kg, ctab, stab):
    d = h.shape[1]
    tm = TOKEN_TILE
    dh = ATT_HEAD_DIM
    nq, nk = ATT_HEADS * dh, ATT_KV_HEADS * dh
    tab_spec = pl.BlockSpec((tm, dh), lambda t: (jnp.where(tk.is_lat(t), t % tk.tiles_per_seq, tk.tiles_per_seq), 0))
    return pl.pallas_call(
        _qkv_kernel,
        grid=(tk.tiles,),
        in_specs=[_tok_spec(d), tk.mod_spec(d), _resident(wqk.shape), _resident(wvt.shape),
                  _resident((1, dh)), _resident((1, dh)), tab_spec, tab_spec],
        out_specs=[_tok_spec(nq), _tok_spec(nk), pl.BlockSpec((nk, tm), lambda t: (0, t))],
        out_shape=[jax.ShapeDtypeStruct((tk.rows, nq), BF16),
                   jax.ShapeDtypeStruct((tk.rows, nk), BF16),
                   jax.ShapeDtypeStruct((nk, tk.rows), BF16)],
        compiler_params=_params("parallel"),
        name="att_qkv",
    )(h, mods, wqk, wvt, qg, kg, ctab, stab)


def _attn_kernel(q_ref, kl_ref, kc_ref, vlt_ref, vct_ref, o_ref, s_scr, *, lat_q_tiles):
    dh = ATT_HEAD_DIM
    kb = ATT_KEY_BLOCK
    tq = q_ref.shape[0]
    n_lat, n_ctx = kl_ref.shape[0], kc_ref.shape[0]
    nt = (((1,), (1,)), ((), ()))

    def key_blocks(with_lat):
        blocks = []
        if with_lat:
            blocks += [(kl_ref, vlt_ref, b * kb, b * kb) for b in range(n_lat // kb)]
        base = n_lat if with_lat else 0
        blocks += [(kc_ref, vct_ref, b * kb, base + b * kb) for b in range(n_ctx // kb)]
        return blocks

    def scores(g, blocks):
        q = q_ref[:, g * dh:(g + 1) * dh]
        m8 = None
        for k_ref, _, r0, s0 in blocks:
            s = lax.dot_general(k_ref[r0:r0 + kb, :], q, nt, preferred_element_type=F32)
            s_scr[g % 2, s0:s0 + kb, :] = s
            bm = jnp.max(s.reshape(kb // 8, 8, tq), axis=0)
            m8 = bm if m8 is None else jnp.maximum(m8, bm)
        return jnp.max(m8, axis=0, keepdims=True)

    def weighted_values(g, blocks, m):
        l8 = None
        ot = None
        for _, vt_ref, r0, s0 in blocks:
            p = jnp.exp2(s_scr[g % 2, s0:s0 + kb, :] - m)
            ps = jnp.sum(p.reshape(kb // 8, 8, tq), axis=0)
            l8 = ps if l8 is None else l8 + ps
            o_blk = jnp.dot(vt_ref[:, r0:r0 + kb], p.astype(BF16), preferred_element_type=F32)
            ot = o_blk if ot is None else ot + o_blk
        denom = jnp.sum(l8, axis=0, keepdims=True)
        o_ref[:, g * dh:(g + 1) * dh] = jnp.transpose(ot / denom).astype(BF16)

    def run(blocks):
        m = scores(0, blocks)
        for g in range(ATT_GROUP):
            m_next = scores(g + 1, blocks) if g + 1 < ATT_GROUP else None
            weighted_values(g, blocks, m)
            m = m_next

    t = pl.program_id(2)

    @pl.when(t < lat_q_tiles)
    def _():
        run(key_blocks(True))

    @pl.when(t >= lat_q_tiles)
    def _():
        run(key_blocks(False))


def _attention(tk, q, k, vt):
    tq = ATT_Q_TILE
    dh = ATT_HEAD_DIM
    gw = ATT_GROUP * dh
    n_lat, n_ctx = tk.n_lat, tk.n_ctx
    assert n_lat % tq == 0 and n_ctx % tq == 0
    lat_q_tiles = n_lat // tq
    q_tiles = lat_q_tiles + n_ctx // tq
    ctx_block0 = tk.lat_rows // n_ctx

    def q_row(i, t):
        return jnp.where(t < lat_q_tiles, i * lat_q_tiles + t,
                         tk.lat_rows // tq + i * (n_ctx // tq) + (t - lat_q_tiles))

    q_spec = pl.BlockSpec((tq, gw), lambda i, j, t: (q_row(i, t), j))
    return pl.pallas_call(
        functools.partial(_attn_kernel, lat_q_tiles=lat_q_tiles),
        grid=(tk.bsz, ATT_KV_HEADS, q_tiles),
        in_specs=[q_spec,
                  pl.BlockSpec((n_lat, dh), lambda i, j, t: (i, j)),
                  pl.BlockSpec((n_ctx, dh), lambda i, j, t: (ctx_block0 + i, j)),
                  pl.BlockSpec((dh, n_lat), lambda i, j, t: (j, i)),
                  pl.BlockSpec((dh, n_ctx), lambda i, j, t: (j, ctx_block0 + i))],
        out_specs=q_spec,
        out_shape=jax.ShapeDtypeStruct(q.shape, BF16),
        scratch_shapes=[pltpu.VMEM((2, n_lat + n_ctx, tq), F32)],
        compiler_params=_params("parallel", "parallel", "parallel"),
        name="att_core",
    )(q, k, k, vt, vt)


def _post_kernel(o_ref, h_ref, mod_ref, w_ref, g_ref, b_ref, out_ref, *, alpha):
    y = jnp.dot(o_ref[...], w_ref[...], preferred_element_type=F32)
    out_ref[...] = _layer_norm(alpha * h_ref[...] + mod_ref[5:6, :] * y, g_ref[...], b_ref[...])


def _post(tk, o, h, mods, w, g, b, alpha):
    d = h.shape[1]
    return pl.pallas_call(
        functools.partial(_post_kernel, alpha=alpha),
        grid=(tk.tiles,),
        in_specs=[_tok_spec(d), _tok_spec(d), tk.mod_spec(d), _resident(w.shape), _resident((1, d)), _resident((1, d))],
        out_specs=_tok_spec(d),
        out_shape=jax.ShapeDtypeStruct(h.shape, F32),
        compiler_params=_params("parallel"),
        name="att_post",
    )(o, h, mods, w, g, b)


def _mlproj_kernel(h_ref, hp_ref, hn_ref, mod_ref, wqk_ref, wv_ref, wo_ref, wg_ref, gb_ref, cw_ref, cb_ref,
                   q_ref, kt_ref, v_ref, sig_ref, gc_ref, at_ref, xm_scr, *, tk):
    tm = TOKEN_TILE
    lc = SCAN_CHUNK
    t = pl.program_id(0)
    xm = _modulated(h_ref[...], mod_ref, 3).astype(BF16)
    xm_scr[0:tm, :] = xm
    xm_scr[tm:tm + HALO, :] = _modulated(hp_ref[...], mod_ref, 3).astype(BF16)
    xm_scr[tm + HALO:tm + 2 * HALO, :] = _modulated(hn_ref[...], mod_ref, 3).astype(BF16)

    row = lax.broadcasted_iota(jnp.int32, (tm, FF_CHUNK), 0)
    pos_in_seq = t % tk.tiles_per_seq
    lat_first = jnp.logical_and(tk.is_lat(t), pos_in_seq == 0)
    lat_last = jnp.logical_and(tk.is_lat(t), pos_in_seq == tk.tiles_per_seq - 1)
    is_ctx = jnp.logical_not(tk.is_lat(t))
    seq_first = jnp.logical_or(jnp.logical_and(row == 0, lat_first),
                               jnp.logical_and(row % tk.n_ctx == 0, is_ctx))
    seq_last = jnp.logical_or(jnp.logical_and(row == tm - 1, lat_last),
                              jnp.logical_and(row % tk.n_ctx == tk.n_ctx - 1, is_ctx))

    hq = ML_HEADS * ML_QK
    n_rounds = 2 * hq // FF_CHUNK

    def round_matmuls(c):
        cols = slice(c * FF_CHUNK, (c + 1) * FF_CHUNK)
        return (jnp.dot(xm_scr[...], wqk_ref[:, cols], preferred_element_type=F32),
                jnp.dot(xm, wv_ref[:, cols], preferred_element_type=F32),
                jnp.dot(xm, wo_ref[:, cols], preferred_element_type=F32))

    pending = round_matmuls(0)
    for c in range(n_rounds):
        cols = slice(c * FF_CHUNK, (c + 1) * FF_CHUNK)
        p, v_c, o_c = pending
        if c + 1 < n_rounds:
            pending = round_matmuls(c + 1)
        main = p[0:tm]
        prev_row = p[tm + HALO - 1:tm + HALO]
        next_row = p[tm + HALO:tm + HALO + 1]
        down = jnp.where(row == 0, prev_row, pltpu.roll(main, 1, 0))
        up = jnp.where(row == tm - 1, next_row, pltpu.roll(main, tm - 1, 0))
        down = jnp.where(seq_first, 0.0, down)
        up = jnp.where(seq_last, 0.0, up)
        conv = down * cw_ref[0:1, cols] + main * cw_ref[1:2, cols] + up * cw_ref[2:3, cols] + cb_ref[:, cols]
        act = _silu(conv)
        if (c + 1) * FF_CHUNK <= hq:
            q_ref[:, cols] = act.astype(BF16)
        else:
            kact = act * (ML_QK ** -0.5)
            for rc in range(tm // lc):
                kt_ref[rc, c * FF_CHUNK - hq:(c + 1) * FF_CHUNK - hq, :] = (
                    jnp.transpose(kact[rc * lc:(rc + 1) * lc, :]).astype(BF16))
        v_ref[:, cols] = v_c.astype(BF16)
        sig_ref[:, cols] = jax.nn.sigmoid(o_c).astype(BF16)

    g = jnp.dot(xm, wg_ref[...], preferred_element_type=F32) + gb_ref[...]
    logf = jax.nn.log_sigmoid(g)
    r_i = lax.broadcasted_iota(jnp.int32, (lc, lc), 0)
    c_i = lax.broadcasted_iota(jnp.int32, (lc, lc), 1)
    tri_lo = (c_i <= r_i).astype(BF16)
    lane = lax.broadcasted_iota(jnp.int32, (lc, 128), 1)
    trow = lax.broadcasted_iota(jnp.int32, (lc, 128), 0)
    nh = ML_HEADS
    for c in range(tm // lc):
        rows = slice(c * lc, (c + 1) * lc)
        lf = logf[rows]
        hi = lf.astype(BF16)
        r1 = lf - hi.astype(F32)
        mid = r1.astype(BF16)
        lo = (r1 - mid.astype(F32)).astype(BF16)
        pre3 = jnp.dot(tri_lo, jnp.concatenate([hi, mid, lo], axis=1), preferred_element_type=F32)
        pre = pre3[:, 0:128] + pre3[:, 128:256] + pre3[:, 256:384]
        suf = pre[lc - 1:lc, :] - pre + lf
        bc = pltpu.roll(jnp.where(lane < 3 * nh, pre, suf), 128 - 2 * nh, 1)
        a = g[rows] - bc
        pf = a
        pb = a
        k = 1
        while k < lc:
            pf = jnp.maximum(pf, jnp.where(trow >= k, pltpu.roll(pf, k, 0), -jnp.inf))
            pb = jnp.maximum(pb, jnp.where(trow < lc - k, pltpu.roll(pb, lc - k, 0), -jnp.inf))
            k *= 2
        pm = jnp.where(lane < nh, pf, pb)
        gc_ref[rows, :] = jnp.where(lane < 2 * nh, bc,
                                    jnp.where(lane < 4 * nh, pltpu.roll(a, 2 * nh, 1), pltpu.roll(pm, 4 * nh, 1)))
        at_ref[c] = jnp.transpose(a)[0:ML_UNITS, :]


def _mlproj(tk, h, mods, wqk, wv, wo, wg, gb, cw, cb):
    d = h.shape[1]
    tm = TOKEN_TILE
    lc = SCAN_CHUNK
    hq = ML_HEADS * ML_QK
    blocks_per_tile = tm // HALO
    last_block = tk.rows // HALO - 1
    return pl.pallas_call(
        functools.partial(_mlproj_kernel, tk=tk),
        grid=(tk.tiles,),
        in_specs=[
            _tok_spec(d),
            pl.BlockSpec((HALO, d), lambda t: (jnp.maximum(t * blocks_per_tile - 1, 0), 0)),
            pl.BlockSpec((HALO, d), lambda t: (jnp.minimum((t + 1) * blocks_per_tile, last_block), 0)),
            tk.mod_spec(d),
            _resident(wqk.shape), _resident(wv.shape), _resident(wo.shape), _resident(wg.shape),
            _resident(gb.shape), _resident(cw.shape), _resident(cb.shape),
        ],
        out_specs=[
            _tok_spec(hq), pl.BlockSpec((tm // lc, hq, lc), lambda t: (t, 0, 0)),
            _tok_spec(d), _tok_spec(d), _tok_spec(128),
            pl.BlockSpec((tm // lc, ML_UNITS, lc), lambda t: (t, 0, 0)),
        ],
        out_shape=[
            jax.ShapeDtypeStruct((tk.rows, hq), BF16),
            jax.ShapeDtypeStruct((tk.rows // lc, hq, lc), BF16),
            jax.ShapeDtypeStruct((tk.rows, d), BF16),
            jax.ShapeDtypeStruct((tk.rows, d), BF16),
            jax.ShapeDtypeStruct((tk.rows, 128), F32),
            jax.ShapeDtypeStruct((tk.rows // lc, ML_UNITS, lc), F32),
        ],
        scratch_shapes=[pltpu.VMEM((tm + 2 * HALO, d), BF16)],
        compiler_params=_params("parallel"),
        name="ml_proj",
    )(h, h, h, mods, wqk, wv, wo, wg, gb, cw, cb)


def _scan_kernel(ql_ref, qc_ref, ktl_ref, ktc_ref, vl_ref, vc_ref, gl_ref, gcx_ref, al_ref, ac_ref, o_ref,
                 ct_scr, m_scr, *, n_lat_chunks, n_ctx_chunks):
    lc = SCAN_CHUNK
    dk, dv = ML_QK, ML_V
    assert lc == 128 and dk == 128 and dv == 2 * 128
    ct_scr[...] = jnp.zeros_like(ct_scr)
    m_scr[...] = jnp.zeros_like(m_scr)
    t_i = lax.broadcasted_iota(jnp.int32, (lc, lc), 0)
    s_i = lax.broadcasted_iota(jnp.int32, (lc, lc), 1)
    masks = (s_i <= t_i, s_i >= t_i)
    ones_blk = jnp.ones((lc, 128), BF16)

    def step(j, mode):
        if mode == "ctx":
            q_ref, kt_ref, v_ref, g_ref, a_ref, n_seq = qc_ref, ktc_ref, vc_ref, gcx_ref, ac_ref, n_ctx_chunks
        else:
            q_ref, kt_ref, v_ref, g_ref, a_ref, n_seq = ql_ref, ktl_ref, vl_ref, gl_ref, al_ref, n_lat_chunks
        chunk = (j, n_seq - 1 - j)
        units = []
        for direction in range(2):
            c = chunk[direction]
            r0 = pl.multiple_of(c * lc, lc)
            gc = g_ref[pl.ds(r0, lc), :]
            at = a_ref[c]
            for hd in range(ML_HEADS):
                units.append((direction, hd, direction * ML_HEADS + hd, c, r0, gc, at))

        st = {}
        for direction, hd, u, c, r0, gc, at in units:
            last = lc - 1 if direction == 0 else 0
            kt = kt_ref[c, hd * dk:(hd + 1) * dk, :]
            v_aug = jnp.concatenate([v_ref[pl.ds(r0, lc), hd * dv:(hd + 1) * dv], ones_blk], axis=1)
            bcum = jnp.broadcast_to(gc[:, u:u + 1], (lc, 128))
            pmax = jnp.broadcast_to(gc[:, 2 * ML_UNITS + u:2 * ML_UNITS + u + 1], (lc, 128))
            a_row = at[u:u + 1, :]
            m = m_scr[u, 0:1, :]
            mu = jnp.maximum(m, pmax)
            mu_l = mu[last:last + 1, :]
            st[u] = dict(kt=kt, v_aug=v_aug, bcum=bcum, a_row=a_row, m=m, mu=mu, mu_l=mu_l, ct=ct_scr[u],
                         b_last=bcum[last:last + 1, :])
        if mode != "ctx":
            for direction, hd, u, c, r0, gc, at in units:
                x = st[u]
                x["qc"] = q_ref[pl.ds(r0, lc), hd * dk:(hd + 1) * dk]
                x["s"] = jnp.dot(x["qc"], x["kt"], preferred_element_type=F32)
            for direction, hd, u, c, r0, gc, at in units:
                x = st[u]
                w = jnp.where(masks[direction], jnp.exp(x["a_row"] - x["mu"]), 0.0)
                eq = jnp.exp(x["m"] - x["mu"]) * x["qc"].astype(F32)
                lhs = jnp.concatenate([(x["s"] * w).astype(BF16), eq.astype(BF16)], axis=1)
                rhs = jnp.concatenate([x["v_aug"], x["ct"].astype(BF16)], axis=0)
                x["num"] = jnp.dot(lhs, rhs, preferred_element_type=F32)
        for direction, hd, u, c, r0, gc, at in units:
            x = st[u]
            ktw = (x["kt"].astype(F32) * jnp.exp(x["a_row"] - x["mu_l"])).astype(BF16)
            x["upd"] = jnp.dot(ktw, x["v_aug"], preferred_element_type=F32)
        if mode != "ctx":
            for direction, hd, u, c, r0, gc, at in units:
                x = st[u]
                num = x["num"]
                inv = 1.0 / jnp.maximum(jnp.abs(num[:, dv:]), jnp.exp(-(x["bcum"] + x["mu"])))
                hout = num[:, :dv] * jnp.concatenate([inv, inv], axis=1)
                if mode == "store":
                    o_ref[pl.ds(r0, lc), hd * dv:(hd + 1) * dv] = hout
                else:
                    o_ref[pl.ds(r0, lc), hd * dv:(hd + 1) * dv] += hout
        for direction, hd, u, c, r0, gc, at in units:
            x = st[u]
            decay = jnp.exp(x["m"] - x["mu_l"])
            ct_scr[u] = x["ct"] * jnp.concatenate([decay, decay, decay], axis=1) + x["upd"]
            m_scr[u, 0:1, :] = x["b_last"] + x["mu_l"]

    half = n_lat_chunks // 2
    lax.fori_loop(0, n_ctx_chunks, lambda j, _: step(j, "ctx"), None)
    lax.fori_loop(0, half, lambda j, _: step(j, "store"), None)
    lax.fori_loop(half, n_lat_chunks, lambda j, _: step(j, "add"), None)


def _scan(tk, q, kt, v, gc, at):
    d = v.shape[1]
    lc = SCAN_CHUNK
    hq = ML_HEADS * ML_QK
    n_lat, n_ctx = tk.n_lat, tk.n_ctx
    n_lat_chunks, n_ctx_chunks = n_lat // lc, n_ctx // lc
    assert n_lat_chunks % 2 == 0
    ctx0 = tk.lat_rows // n_ctx
    ctx_chunk0 = tk.lat_rows // lc // n_ctx_chunks
    lat = lambda width: pl.BlockSpec((n_lat, width), lambda i: (i, 0))
    ctx = lambda width: pl.BlockSpec((n_ctx, width), lambda i: (ctx0 + i, 0))
    lat_chunks = lambda rows: pl.BlockSpec((n_lat_chunks, rows, lc), lambda i: (i, 0, 0))
    ctx_chunks = lambda rows: pl.BlockSpec((n_ctx_chunks, rows, lc), lambda i: (ctx_chunk0 + i, 0, 0))
    kern = functools.partial(_scan_kernel, n_lat_chunks=n_lat_chunks, n_ctx_chunks=n_ctx_chunks)
    return pl.pallas_call(
        kern,
        grid=(tk.bsz,),
        in_specs=[
            lat(hq), ctx(hq), lat_chunks(hq), ctx_chunks(hq), lat(d), ctx(d), lat(128), ctx(128),
            lat_chunks(ML_UNITS), ctx_chunks(ML_UNITS),
        ],
        out_specs=lat(d),
        out_shape=jax.ShapeDtypeStruct((tk.lat_rows, d), F32),
        scratch_shapes=[
            pltpu.VMEM((ML_UNITS, ML_QK, ML_V + 128), F32),
            pltpu.VMEM((ML_UNITS, 8, 128), F32),
        ],
        compiler_params=_params("parallel"),
        name="ml_scan",
    )(q, q, kt, kt, v, v, gc, gc, at, at)


def _readout_kernel(hs_ref, sig_ref, h_ref, mod_ref, ng_ref, w_ref, g_ref, b_ref, out_ref, *, alpha):
    dv = ML_V
    parts = []
    for hd in range(ML_HEADS):
        x = hs_ref[:, hd * dv:(hd + 1) * dv]
        mu = jnp.mean(x, axis=-1, keepdims=True)
        xc = x - mu
        var = jnp.mean(xc * xc, axis=-1, keepdims=True)
        parts.append(xc * lax.rsqrt(var + LN_EPS))
    hn = jnp.concatenate(parts, axis=1) * ng_ref[...]
    z = (hn * sig_ref[...].astype(F32)).astype(BF16)
    y = jnp.dot(z, w_ref[...], preferred_element_type=F32)
    out_ref[...] = _layer_norm(alpha * h_ref[...] + mod_ref[5:6, :] * y, g_ref[...], b_ref[...])


def _readout(tk, hs, sig, h, mods, ng, w, g, b, alpha):
    d = hs.shape[1]
    return pl.pallas_call(
        functools.partial(_readout_kernel, alpha=alpha),
        grid=(tk.lat_tiles,),
        in_specs=[_tok_spec(d), _tok_spec(d), _tok_spec(d), tk.mod_spec(d),
                  _resident((1, d)), _resident(w.shape), _resident((1, d)), _resident((1, d))],
        out_specs=_tok_spec(d),
        out_shape=jax.ShapeDtypeStruct(hs.shape, F32),
        compiler_params=_params("parallel"),
        name="ml_readout",
    )(hs, sig, h, mods, ng, w, g, b)


def _rope_tables(n_lat):
    rows = n_lat // GRID_W
    row = jnp.repeat(jnp.arange(rows, dtype=jnp.int32), GRID_W).astype(F32)
    col = jnp.tile(jnp.arange(GRID_W, dtype=jnp.int32), rows).astype(F32)
    inv = ROPE_THETA ** (-jnp.arange(ROPE_PAIRS, dtype=F32) / ROPE_PAIRS)
    ar, ac = row[:, None] * inv, col[:, None] * inv
    cos = jnp.concatenate([jnp.cos(ar), jnp.cos(ac), jnp.cos(ar), jnp.cos(ac)], axis=1)
    sin = jnp.concatenate([-jnp.sin(ar), -jnp.sin(ac), jnp.sin(ar), jnp.sin(ac)], axis=1)
    cos = jnp.concatenate([cos, jnp.ones((TOKEN_TILE, ATT_HEAD_DIM), F32)], axis=0)
    sin = jnp.concatenate([sin, jnp.zeros((TOKEN_TILE, ATT_HEAD_DIM), F32)], axis=0)
    return cos, sin


def _rope_column_order():
    p = ROPE_PAIRS
    return jnp.concatenate([jnp.arange(0, p), jnp.arange(2 * p, 3 * p), jnp.arange(p, 2 * p), jnp.arange(3 * p, 4 * p)])


def kernel(x, c, ctx, c_ctx, ada_w, ada_b, ln_g, ln_b, ffn_w_in, ffn_w_out, att_w_in, att_q_gain, att_k_gain,
           att_w_out, ml_w_in, ml_gate_b, ml_conv_w, ml_conv_b, ml_norm_g, ml_w_out):
    bsz, n_lat, d = x.shape
    n_ctx = ctx.shape[1]
    depth = ada_w.shape[0]
    dff = ffn_w_out.shape[2]
    assert depth == 2, "layer 0 attention, layer 1 (last) mLSTM"
    assert dff % FF_CHUNK == 0 and TOKEN_TILE % SCAN_CHUNK == 0
    tk = _Tokens(bsz, n_lat, n_ctx)
    alpha = (2.0 * depth) ** 0.25

    mod_rows = -(-(bsz + 1) // 8) * 8
    cc = jnp.concatenate([c, c_ctx[None, :], jnp.zeros((mod_rows - bsz - 1, d), F32)], axis=0)
    mods = _ada(cc, ada_w, ada_b).reshape(depth, mod_rows, N_MOD, d)

    win = ffn_w_in.astype(BF16)
    wout = ffn_w_out.astype(BF16)
    row2 = lambda a: a.reshape(1, -1)
    cos, sin = _rope_tables(n_lat)

    m_i = mods[0]
    h = _ffn(tk, (x.reshape(bsz * n_lat, d), ctx.reshape(bsz * n_ctx, d)), m_i, tk.tiles, win[0, 0], wout[0, 0],
             row2(ln_g[0, 0]), row2(ln_b[0, 0]), 0, alpha)
    dh = ATT_HEAD_DIM
    nqk = (ATT_HEADS + ATT_KV_HEADS) * dh
    order = _rope_column_order()
    wqk = att_w_in[0][:, :nqk].reshape(d, ATT_HEADS + ATT_KV_HEADS, dh)[:, :, order].reshape(d, nqk).astype(BF16)
    wvt = att_w_in[0][:, nqk:].T.astype(BF16)
    q, k, vt = _qkv(tk, h, m_i, wqk, wvt, row2(att_q_gain[0][order]), row2(att_k_gain[0][order]), cos, sin)
    o = _attention(tk, q, k, vt)
    h = _post(tk, o, h, m_i, att_w_out[0].astype(BF16), row2(ln_g[0, 1]), row2(ln_b[0, 1]), alpha)
    h = _ffn(tk, h, m_i, tk.tiles, win[0, 1], wout[0, 1], row2(ln_g[0, 2]), row2(ln_b[0, 2]), 6, alpha)

    m_i = mods[1]
    h = _ffn(tk, h, m_i, tk.tiles, win[1, 0], wout[1, 0], row2(ln_g[1, 0]), row2(ln_b[1, 0]), 0, alpha)
    hq = ML_HEADS * ML_QK
    w = ml_w_in[0]
    wqk = w[:, :2 * hq].astype(BF16)
    wv = w[:, 2 * hq:2 * hq + d].astype(BF16)
    wo = w[:, 2 * hq + d:2 * hq + 2 * d].astype(BF16)
    perm = jnp.array([0, 2, 1, 3])
    wg = w[:, 2 * hq + 2 * d:].reshape(d, 4, ML_HEADS)[:, perm].reshape(d, 4 * ML_HEADS)
    wg = jnp.pad(wg, ((0, 0), (0, 128 - 4 * ML_HEADS))).astype(BF16)
    gb = ml_gate_b[0].reshape(4, ML_HEADS)[perm].reshape(1, 4 * ML_HEADS)
    gb = jnp.pad(gb, ((0, 0), (0, 128 - 4 * ML_HEADS)))
    q, kt, v, sig, gc, at = _mlproj(tk, h, m_i, wqk, wv, wo, wg, gb, ml_conv_w[0], row2(ml_conv_b[0]))
    hs = _scan(tk, q, kt, v, gc, at)
    h = _readout(tk, hs, sig, h, m_i, row2(ml_norm_g[0]), ml_w_out[0].astype(BF16),
                 row2(ln_g[1, 1]), row2(ln_b[1, 1]), alpha)
    h = _ffn(tk, h, m_i, tk.lat_tiles, win[1, 1], wout[1, 1], row2(ln_g[1, 2]), row2(ln_b[1, 2]), 6, alpha)
    return h.reshape(bsz, n_lat, d)
```

```python
import functools

import jax
import jax.numpy as jnp
from jax import lax
from jax.experimental import pallas as pl
from jax.experimental.pallas import tpu as pltpu

F32 = jnp.float32
BF16 = jnp.bfloat16

N_MOD = 9
GRID_W = 64
ATT_HEADS = 8
ATT_KV_HEADS = 2
ATT_GROUP = ATT_HEADS // ATT_KV_HEADS
ATT_HEAD_DIM = 128
ROPE_PAIRS = ATT_HEAD_DIM // 4
ROPE_THETA = 10000.0
ML_HEADS = 4
ML_QK = 128
ML_V = 256
ML_UNITS = 2 * ML_HEADS
LN_EPS = 1e-5
RMS_EPS = 1e-6
LOG2_E = 1.4426950408889634

TOKEN_TILE = 1024
SUB_TILE = 512
ATT_Q_TILE = 256
ATT_KEY_BLOCK = 256
FF_CHUNK = 256
SCAN_CHUNK = 128
HALO = 8
VMEM_LIMIT = 56 * 1024 * 1024


def _params(*sem):
    return pltpu.CompilerParams(dimension_semantics=sem, vmem_limit_bytes=VMEM_LIMIT)


def _resident(shape):
    n = len(shape)
    return pl.BlockSpec(shape, lambda *_: (0,) * n, pipeline_mode=pl.Buffered(1))


def _post_norm(h, gate, y, g, b, alpha):
    x = h + (gate * (1.0 / alpha)) * y
    mu = jnp.mean(x, axis=-1, keepdims=True)
    xc = x - mu
    var = jnp.mean(xc * xc, axis=-1, keepdims=True)
    return xc * lax.rsqrt(var + LN_EPS / (alpha * alpha)) * g + b


def _silu(x):
    return x * jax.nn.sigmoid(x)


def _modulated(h, mod_ref, k0):
    return h * (1.0 + mod_ref[k0 + 1:k0 + 2, :]) + mod_ref[k0:k0 + 1, :]


class _Tokens:
    def __init__(self, bsz, n_lat, n_ctx):
        tm = TOKEN_TILE
        assert n_lat % tm == 0 and (bsz * n_ctx) % tm == 0 and tm % n_ctx == 0
        self.bsz, self.n_lat, self.n_ctx = bsz, n_lat, n_ctx
        self.lat_rows = bsz * n_lat
        self.rows = bsz * (n_lat + n_ctx)
        self.tiles_per_seq = n_lat // tm
        self.lat_tiles = self.lat_rows // tm
        self.tiles = self.rows // tm

    def is_lat(self, t):
        return t < self.lat_tiles

    def mod_row(self, t):
        return jnp.where(t < self.lat_tiles, t // self.tiles_per_seq, self.bsz)

    def mod_spec(self, d):
        return pl.BlockSpec((None, N_MOD, d), lambda t: (self.mod_row(t), 0, 0))


def _tok_spec(width):
    return pl.BlockSpec((TOKEN_TILE, width), lambda t: (t, 0))


def _ada_kernel(c_ref, w_ref, b_ref, o_ref):
    s = _silu(c_ref[...]).astype(BF16)
    o_ref[...] = jnp.dot(s, w_ref[...].astype(BF16), preferred_element_type=F32) + b_ref[...]


def _ada(cc, ada_w, ada_b):
    depth, d, nd = ada_w.shape
    rows = cc.shape[0]
    return pl.pallas_call(
        _ada_kernel,
        grid=(depth, nd // d),
        in_specs=[
            pl.BlockSpec((rows, d), lambda i, j: (0, 0)),
            pl.BlockSpec((None, d, d), lambda i, j: (i, 0, j)),
            pl.BlockSpec((None, 1, d), lambda i, j: (i, 0, j)),
        ],
        out_specs=pl.BlockSpec((None, rows, d), lambda i, j: (i, 0, j)),
        out_shape=jax.ShapeDtypeStruct((depth, rows, nd), F32),
        compiler_params=_params("parallel", "parallel"),
        name="ada_mod",
    )(cc, ada_w, ada_b.reshape(depth, 1, nd))


def _ffn_kernel(*refs, k0, alpha, dff, lat_tiles, split_input):
    if split_input:
        hx_ref, hc_ref, mod_ref, win_ref, wout_ref, g_ref, b_ref, o_ref = refs
        is_lat = pl.program_id(0) < lat_tiles
        load = lambda rows: jnp.where(is_lat, hx_ref[rows, :], hc_ref[rows, :])
    else:
        h_ref, mod_ref, win_ref, wout_ref, g_ref, b_ref, o_ref = refs
        load = lambda rows: h_ref[rows, :]
    n_sub = o_ref.shape[0] // SUB_TILE
    sub = lambda r: slice(r * SUB_TILE, (r + 1) * SUB_TILE)

    def ffn(r):
        h = load(sub(r))
        xm = _modulated(h, mod_ref, k0).astype(BF16)
        acc = None
        for j in range(dff // FF_CHUNK):
            cols = slice(FF_CHUNK * j, FF_CHUNK * (j + 1))
            a = jnp.dot(xm, win_ref[:, cols], preferred_element_type=F32)
            u = jnp.dot(xm, win_ref[:, dff + FF_CHUNK * j:dff + FF_CHUNK * (j + 1)], preferred_element_type=F32)
            hid = (_silu(a) * u).astype(BF16)
            y = jnp.dot(hid, wout_ref[cols, :], preferred_element_type=F32)
            acc = y if acc is None else acc + y
        return h, acc

    gate = mod_ref[k0 + 2:k0 + 3, :]
    pending = ffn(0)
    for r in range(n_sub):
        h, acc = pending
        if r + 1 < n_sub:
            pending = ffn(r + 1)
        o_ref[sub(r), :] = _post_norm(h, 0.5 * gate, acc, g_ref[...], b_ref[...], alpha)


def _ffn(tk, hs, mods, n_tiles, win, wout, g, b, k0, alpha):
    split = isinstance(hs, tuple)
    d, dff = wout.shape[1], wout.shape[0]
    tm = TOKEN_TILE
    if split:
        last_lat = tk.lat_tiles - 1
        h_specs = [pl.BlockSpec((tm, d), lambda t: (jnp.minimum(t, last_lat), 0)),
                   pl.BlockSpec((tm, d), lambda t: (jnp.maximum(t - tk.lat_tiles, 0), 0))]
        hs = list(hs)
    else:
        h_specs = [_tok_spec(d)]
        hs = [hs]
    kern = functools.partial(_ffn_kernel, k0=k0, alpha=alpha, dff=dff, lat_tiles=tk.lat_tiles, split_input=split)
    return pl.pallas_call(
        kern,
        grid=(n_tiles,),
        in_specs=h_specs + [tk.mod_spec(d), _resident(win.shape), _resident(wout.shape),
                            _resident((1, d)), _resident((1, d))],
        out_specs=_tok_spec(d),
        out_shape=jax.ShapeDtypeStruct((n_tiles * tm, d), F32),
        compiler_params=_params("parallel"),
        name="half_ffn",
    )(*hs, mods, win, wout, g, b)


def _qkv_kernel(h_ref, mod_ref, wqk_ref, wvt_ref, qg_ref, kg_ref, ct_ref, st_ref, q_ref, k_ref, vt_ref):
    xm = _modulated(h_ref[...], mod_ref, 3).astype(BF16)
    ct = ct_ref[...]
    st = st_ref[...]
    dh = ATT_HEAD_DIM

    ones_w = jnp.ones((2 * dh, dh), BF16)

    def norm_rope(x, gain):
        sq = x * x
        hi = sq.astype(BF16)
        lo = (sq - hi.astype(F32)).astype(BF16)
        ssq = jnp.dot(jnp.concatenate([hi, lo], axis=1), ones_w, preferred_element_type=F32)
        xn = x * lax.rsqrt(ssq * (1.0 / dh) + RMS_EPS) * gain
        return xn * ct + pltpu.roll(xn, dh // 2, 1) * st

    qg = qg_ref[...] * (dh ** -0.5 * LOG2_E)
    kg = kg_ref[...]
    n_pairs = (ATT_HEADS + ATT_KV_HEADS) // 2
    pair_dot = lambda i: jnp.dot(xm, wqk_ref[:, 2 * i * dh:2 * (i + 1) * dh], preferred_element_type=F32)
    pending = pair_dot(0)
    for pair in range(n_pairs):
        p = pending
        if pair + 1 < n_pairs:
            pending = pair_dot(pair + 1)
        for half in range(2):
            hd = 2 * pair + half
            x = p[:, half * dh:(half + 1) * dh]
            if hd < ATT_HEADS:
                q_ref[:, hd * dh:(hd + 1) * dh] = norm_rope(x, qg).astype(BF16)
            else:
                hk = hd - ATT_HEADS
                k_ref[:, hk * dh:(hk + 1) * dh] = norm_rope(x, kg).astype(BF16)
    vt = lax.dot_general(wvt_ref[...], xm, (((1,), (1,)), ((), ())), preferred_element_type=F32)
    vt_ref[...] = vt.astype(BF16)


def _qkv(tk, h, mods, wqk, wvt, qg, kg, ctab, stab):
    d = h.shape[1]
    tm = TOKEN_TILE
    dh = ATT_HEAD_DIM
    nq, nk = ATT_HEADS * dh, ATT_KV_HEADS * dh
    tab_spec = pl.BlockSpec((tm, dh), lambda t: (jnp.where(tk.is_lat(t), t % tk.tiles_per_seq, tk.tiles_per_seq), 0))
    return pl.pallas_call(
        _qkv_kernel,
        grid=(tk.tiles,),
        in_specs=[_tok_spec(d), tk.mod_spec(d), _resident(wqk.shape), _resident(wvt.shape),
                  _resident((1, dh)), _resident((1, dh)), tab_spec, tab_spec],
        out_specs=[_tok_spec(nq), _tok_spec(nk), pl.BlockSpec((nk, tm), lambda t: (0, t))],
        out_shape=[jax.ShapeDtypeStruct((tk.rows, nq), BF16),
                   jax.ShapeDtypeStruct((tk.rows, nk), BF16),
                   jax.ShapeDtypeStruct((nk, tk.rows), BF16)],
        compiler_params=_params("parallel"),
        name="att_qkv",
    )(h, mods, wqk, wvt, qg, kg, ctab, stab)


def _attn_kernel(q_ref, kl_ref, kc_ref, vlt_ref, vct_ref, o_ref, s_scr, *, lat_q_tiles):
    dh = ATT_HEAD_DIM
    kb = ATT_KEY_BLOCK
    tq = q_ref.shape[0]
    n_lat, n_ctx = kl_ref.shape[0], kc_ref.shape[0]
    nt = (((1,), (1,)), ((), ()))

    def key_blocks(with_lat):
        blocks = []
        if with_lat:
            blocks += [(kl_ref, vlt_ref, b * kb, b * kb) for b in range(n_lat // kb)]
        base = n_lat if with_lat else 0
        blocks += [(kc_ref, vct_ref, b * kb, base + b * kb) for b in range(n_ctx // kb)]
        return blocks

    def scores(g, blocks):
        q = q_ref[:, g * dh:(g + 1) * dh]
        m8 = None
        for k_ref, _, r0, s0 in blocks:
            s = lax.dot_general(k_ref[r0:r0 + kb, :], q, nt, preferred_element_type=F32)
            s_scr[g % 2, s0:s0 + kb, :] = s
            bm = jnp.max(s.reshape(kb // 8, 8, tq), axis=0)
            m8 = bm if m8 is None else jnp.maximum(m8, bm)
        return jnp.max(m8, axis=0, keepdims=True)

    def weighted_values(g, blocks, m):
        l8 = None
        ot = None
        for _, vt_ref, r0, s0 in blocks:
            p = jnp.exp2(s_scr[g % 2, s0:s0 + kb, :] - m)
            ps = jnp.sum(p.reshape(kb // 8, 8, tq), axis=0)
            l8 = ps if l8 is None else l8 + ps
            o_blk = jnp.dot(vt_ref[:, r0:r0 + kb], p.astype(BF16), preferred_element_type=F32)
            ot = o_blk if ot is None else ot + o_blk
        denom = jnp.sum(l8, axis=0, keepdims=True)
        o_ref[:, g * dh:(g + 1) * dh] = jnp.transpose(ot / denom).astype(BF16)

    def run(blocks):
        m = scores(0, blocks)
        for g in range(ATT_GROUP):
            m_next = scores(g + 1, blocks) if g + 1 < ATT_GROUP else None
            weighted_values(g, blocks, m)
            m = m_next

    t = pl.program_id(2)

    @pl.when(t < lat_q_tiles)
    def _():
        run(key_blocks(True))

    @pl.when(t >= lat_q_tiles)
    def _():
        run(key_blocks(False))


def _attention(tk, q, k, vt):
    tq = ATT_Q_TILE
    dh = ATT_HEAD_DIM
    gw = ATT_GROUP * dh
    n_lat, n_ctx = tk.n_lat, tk.n_ctx
    assert n_lat % tq == 0 and n_ctx % tq == 0
    lat_q_tiles = n_lat // tq
    q_tiles = lat_q_tiles + n_ctx // tq
    ctx_block0 = tk.lat_rows // n_ctx

    def q_row(i, t):
        return jnp.where(t < lat_q_tiles, i * lat_q_tiles + t,
                         tk.lat_rows // tq + i * (n_ctx // tq) + (t - lat_q_tiles))

    q_spec = pl.BlockSpec((tq, gw), lambda i, j, t: (q_row(i, t), j))
    return pl.pallas_call(
        functools.partial(_attn_kernel, lat_q_tiles=lat_q_tiles),
        grid=(tk.bsz, ATT_KV_HEADS, q_tiles),
        in_specs=[q_spec,
                  pl.BlockSpec((n_lat, dh), lambda i, j, t: (i, j)),
                  pl.BlockSpec((n_ctx, dh), lambda i, j, t: (ctx_block0 + i, j)),
                  pl.BlockSpec((dh, n_lat), lambda i, j, t: (j, i)),
                  pl.BlockSpec((dh, n_ctx), lambda i, j, t: (j, ctx_block0 + i))],
        out_specs=q_spec,
        out_shape=jax.ShapeDtypeStruct(q.shape, BF16),
        scratch_shapes=[pltpu.VMEM((2, n_lat + n_ctx, tq), F32)],
        compiler_params=_params("parallel", "parallel", "parallel"),
        name="att_core",
    )(q, k, k, vt, vt)


def _post_kernel(o_ref, h_ref, mod_ref, w_ref, g_ref, b_ref, out_ref, *, alpha):
    n_sub = o_ref.shape[0] // SUB_TILE
    sub = lambda r: slice(r * SUB_TILE, (r + 1) * SUB_TILE)
    proj = lambda r: jnp.dot(o_ref[sub(r), :], w_ref[...], preferred_element_type=F32)
    pending = proj(0)
    for r in range(n_sub):
        y = pending
        if r + 1 < n_sub:
            pending = proj(r + 1)
        out_ref[sub(r), :] = _post_norm(h_ref[sub(r), :], mod_ref[5:6, :], y, g_ref[...], b_ref[...], alpha)


def _post(tk, o, h, mods, w, g, b, alpha):
    d = h.shape[1]
    return pl.pallas_call(
        functools.partial(_post_kernel, alpha=alpha),
        grid=(tk.tiles,),
        in_specs=[_tok_spec(d), _tok_spec(d), tk.mod_spec(d), _resident(w.shape), _resident((1, d)), _resident((1, d))],
        out_specs=_tok_spec(d),
        out_shape=jax.ShapeDtypeStruct(h.shape, F32),
        compiler_params=_params("parallel"),
        name="att_post",
    )(o, h, mods, w, g, b)


def _mlproj_kernel(h_ref, hp_ref, hn_ref, mod_ref, wqk_ref, wv_ref, wo_ref, wg_ref, gb_ref, cw_ref, cb_ref,
                   q_ref, kt_ref, v_ref, sig_ref, gc_ref, at_ref, xm_scr, *, tk):
    tm = TOKEN_TILE
    lc = SCAN_CHUNK
    t = pl.program_id(0)
    xm = _modulated(h_ref[...], mod_ref, 3).astype(BF16)
    xm_scr[0:tm, :] = xm
    xm_scr[tm:tm + HALO, :] = _modulated(hp_ref[...], mod_ref, 3).astype(BF16)
    xm_scr[tm + HALO:tm + 2 * HALO, :] = _modulated(hn_ref[...], mod_ref, 3).astype(BF16)

    row = lax.broadcasted_iota(jnp.int32, (tm, FF_CHUNK), 0)
    pos_in_seq = t % tk.tiles_per_seq
    lat_first = jnp.logical_and(tk.is_lat(t), pos_in_seq == 0)
    lat_last = jnp.logical_and(tk.is_lat(t), pos_in_seq == tk.tiles_per_seq - 1)
    is_ctx = jnp.logical_not(tk.is_lat(t))
    seq_first = jnp.logical_or(jnp.logical_and(row == 0, lat_first),
                               jnp.logical_and(row % tk.n_ctx == 0, is_ctx))
    seq_last = jnp.logical_or(jnp.logical_and(row == tm - 1, lat_last),
                              jnp.logical_and(row % tk.n_ctx == tk.n_ctx - 1, is_ctx))

    hq = ML_HEADS * ML_QK
    n_rounds = 2 * hq // FF_CHUNK

    def round_matmuls(c):
        cols = slice(c * FF_CHUNK, (c + 1) * FF_CHUNK)
        return (jnp.dot(xm_scr[...], wqk_ref[:, cols], preferred_element_type=F32),
                jnp.dot(xm, wv_ref[:, cols], preferred_element_type=F32),
                jnp.dot(xm, wo_ref[:, cols], preferred_element_type=F32))

    pending = round_matmuls(0)
    for c in range(n_rounds):
        cols = slice(c * FF_CHUNK, (c + 1) * FF_CHUNK)
        p, v_c, o_c = pending
        if c + 1 < n_rounds:
            pending = round_matmuls(c + 1)
        main = p[0:tm]
        prev_row = p[tm + HALO - 1:tm + HALO]
        next_row = p[tm + HALO:tm + HALO + 1]
        down = jnp.where(row == 0, prev_row, pltpu.roll(main, 1, 0))
        up = jnp.where(row == tm - 1, next_row, pltpu.roll(main, tm - 1, 0))
        down = jnp.where(seq_first, 0.0, down)
        up = jnp.where(seq_last, 0.0, up)
        conv = down * cw_ref[0:1, cols] + main * cw_ref[1:2, cols] + up * cw_ref[2:3, cols] + cb_ref[:, cols]
        act = _silu(conv)
        if (c + 1) * FF_CHUNK <= hq:
            q_ref[:, cols] = act.astype(BF16)
        else:
            kact = act * (ML_QK ** -0.5)
            for rc in range(tm // lc):
                kt_ref[rc, c * FF_CHUNK - hq:(c + 1) * FF_CHUNK - hq, :] = (
                    jnp.transpose(kact[rc * lc:(rc + 1) * lc, :]).astype(BF16))
        v_ref[:, cols] = v_c.astype(BF16)
        sig_ref[:, cols] = jax.nn.sigmoid(o_c).astype(BF16)

    g = jnp.dot(xm, wg_ref[...], preferred_element_type=F32) + gb_ref[...]
    logf = jax.nn.log_sigmoid(g)
    r_i = lax.broadcasted_iota(jnp.int32, (lc, lc), 0)
    c_i = lax.broadcasted_iota(jnp.int32, (lc, lc), 1)
    tri_lo = (c_i <= r_i).astype(BF16)
    lane = lax.broadcasted_iota(jnp.int32, (lc, 128), 1)
    trow = lax.broadcasted_iota(jnp.int32, (lc, 128), 0)
    nh = ML_HEADS
    for c in range(tm // lc):
        rows = slice(c * lc, (c + 1) * lc)
        lf = logf[rows]
        hi = lf.astype(BF16)
        r1 = lf - hi.astype(F32)
        mid = r1.astype(BF16)
        lo = (r1 - mid.astype(F32)).astype(BF16)
        pre3 = jnp.dot(tri_lo, jnp.concatenate([hi, mid, lo], axis=1), preferred_element_type=F32)
        pre = pre3[:, 0:128] + pre3[:, 128:256] + pre3[:, 256:384]
        suf = pre[lc - 1:lc, :] - pre + lf
        bc = pltpu.roll(jnp.where(lane < 3 * nh, pre, suf), 128 - 2 * nh, 1)
        a = g[rows] - bc
        pf = a
        pb = a
        k = 1
        while k < lc:
            pf = jnp.maximum(pf, jnp.where(trow >= k, pltpu.roll(pf, k, 0), -jnp.inf))
            pb = jnp.maximum(pb, jnp.where(trow < lc - k, pltpu.roll(pb, lc - k, 0), -jnp.inf))
            k *= 2
        pm = jnp.where(lane < nh, pf, pb)
        gc_ref[rows, :] = jnp.where(lane < 2 * nh, bc,
                                    jnp.where(lane < 4 * nh, pltpu.roll(a, 2 * nh, 1), pltpu.roll(pm, 4 * nh, 1)))
        at_ref[c] = jnp.transpose(a)[0:ML_UNITS, :]


def _mlproj(tk, h, mods, wqk, wv, wo, wg, gb, cw, cb):
    d = h.shape[1]
    tm = TOKEN_TILE
    lc = SCAN_CHUNK
    hq = ML_HEADS * ML_QK
    blocks_per_tile = tm // HALO
    last_block = tk.rows // HALO - 1
    return pl.pallas_call(
        functools.partial(_mlproj_kernel, tk=tk),
        grid=(tk.tiles,),
        in_specs=[
            _tok_spec(d),
            pl.BlockSpec((HALO, d), lambda t: (jnp.maximum(t * blocks_per_tile - 1, 0), 0)),
            pl.BlockSpec((HALO, d), lambda t: (jnp.minimum((t + 1) * blocks_per_tile, last_block), 0)),
            tk.mod_spec(d),
            _resident(wqk.shape), _resident(wv.shape), _resident(wo.shape), _resident(wg.shape),
            _resident(gb.shape), _resident(cw.shape), _resident(cb.shape),
        ],
        out_specs=[
            _tok_spec(hq), pl.BlockSpec((tm // lc, hq, lc), lambda t: (t, 0, 0)),
            _tok_spec(d), _tok_spec(d), _tok_spec(128),
            pl.BlockSpec((tm // lc, ML_UNITS, lc), lambda t: (t, 0, 0)),
        ],
        out_shape=[
            jax.ShapeDtypeStruct((tk.rows, hq), BF16),
            jax.ShapeDtypeStruct((tk.rows // lc, hq, lc), BF16),
            jax.ShapeDtypeStruct((tk.rows, d), BF16),
            jax.ShapeDtypeStruct((tk.rows, d), BF16),
            jax.ShapeDtypeStruct((tk.rows, 128), F32),
            jax.ShapeDtypeStruct((tk.rows // lc, ML_UNITS, lc), F32),
        ],
        scratch_shapes=[pltpu.VMEM((tm + 2 * HALO, d), BF16)],
        compiler_params=_params("parallel"),
        name="ml_proj",
    )(h, h, h, mods, wqk, wv, wo, wg, gb, cw, cb)


def _scan_kernel(ql_ref, qc_ref, ktl_ref, ktc_ref, vl_ref, vc_ref, gl_ref, gcx_ref, al_ref, ac_ref, o_ref,
                 ct_scr, m_scr, *, n_lat_chunks, n_ctx_chunks):
    lc = SCAN_CHUNK
    dk, dv = ML_QK, ML_V
    assert lc == 128 and dk == 128 and dv == 2 * 128
    ct_scr[...] = jnp.zeros_like(ct_scr)
    m_scr[...] = jnp.zeros_like(m_scr)
    t_i = lax.broadcasted_iota(jnp.int32, (lc, lc), 0)
    s_i = lax.broadcasted_iota(jnp.int32, (lc, lc), 1)
    masks = (s_i <= t_i, s_i >= t_i)
    ones_blk = jnp.ones((lc, 128), BF16)

    def step(j, mode):
        if mode == "ctx":
            q_ref, kt_ref, v_ref, g_ref, a_ref, n_seq = qc_ref, ktc_ref, vc_ref, gcx_ref, ac_ref, n_ctx_chunks
        else:
            q_ref, kt_ref, v_ref, g_ref, a_ref, n_seq = ql_ref, ktl_ref, vl_ref, gl_ref, al_ref, n_lat_chunks
        chunk = (j, n_seq - 1 - j)
        units = []
        for direction in range(2):
            c = chunk[direction]
            r0 = pl.multiple_of(c * lc, lc)
            gc = g_ref[pl.ds(r0, lc), :]
            at = a_ref[c]
            for hd in range(ML_HEADS):
                units.append((direction, hd, direction * ML_HEADS + hd, c, r0, gc, at))

        st = {}
        for direction, hd, u, c, r0, gc, at in units:
            last = lc - 1 if direction == 0 else 0
            kt = kt_ref[c, hd * dk:(hd + 1) * dk, :]
            v_aug = jnp.concatenate([v_ref[pl.ds(r0, lc), hd * dv:(hd + 1) * dv], ones_blk], axis=1)
            bcum = jnp.broadcast_to(gc[:, u:u + 1], (lc, 128))
            pmax = jnp.broadcast_to(gc[:, 2 * ML_UNITS + u:2 * ML_UNITS + u + 1], (lc, 128))
            a_row = at[u:u + 1, :]
            m = m_scr[u, 0:1, :]
            mu = jnp.maximum(m, pmax)
            mu_l = mu[last:last + 1, :]
            st[u] = dict(kt=kt, v_aug=v_aug, bcum=bcum, a_row=a_row, m=m, mu=mu, mu_l=mu_l, ct=ct_scr[u],
                         b_last=bcum[last:last + 1, :])
        if mode != "ctx":
            for direction, hd, u, c, r0, gc, at in units:
                x = st[u]
                x["qc"] = q_ref[pl.ds(r0, lc), hd * dk:(hd + 1) * dk]
                x["s"] = jnp.dot(x["qc"], x["kt"], preferred_element_type=F32)
            for direction, hd, u, c, r0, gc, at in units:
                x = st[u]
                w = jnp.where(masks[direction], jnp.exp(x["a_row"] - x["mu"]), 0.0)
                eq = jnp.exp(x["m"] - x["mu"]) * x["qc"].astype(F32)
                lhs = jnp.concatenate([(x["s"] * w).astype(BF16), eq.astype(BF16)], axis=1)
                rhs = jnp.concatenate([x["v_aug"], x["ct"].astype(BF16)], axis=0)
                x["num"] = jnp.dot(lhs, rhs, preferred_element_type=F32)
        for direction, hd, u, c, r0, gc, at in units:
            x = st[u]
            ktw = (x["kt"].astype(F32) * jnp.exp(x["a_row"] - x["mu_l"])).astype(BF16)
            x["upd"] = jnp.dot(ktw, x["v_aug"], preferred_element_type=F32)
        if mode != "ctx":
            for direction, hd, u, c, r0, gc, at in units:
                x = st[u]
                num = x["num"]
                inv = 1.0 / jnp.maximum(jnp.abs(num[:, dv:]), jnp.exp(-(x["bcum"] + x["mu"])))
                hout = num[:, :dv] * jnp.concatenate([inv, inv], axis=1)
                if mode == "store":
                    o_ref[pl.ds(r0, lc), hd * dv:(hd + 1) * dv] = hout
                else:
                    o_ref[pl.ds(r0, lc), hd * dv:(hd + 1) * dv] += hout
        for direction, hd, u, c, r0, gc, at in units:
            x = st[u]
            decay = jnp.exp(x["m"] - x["mu_l"])
            ct_scr[u] = x["ct"] * jnp.concatenate([decay, decay, decay], axis=1) + x["upd"]
            m_scr[u, 0:1, :] = x["b_last"] + x["mu_l"]

    half = n_lat_chunks // 2
    lax.fori_loop(0, n_ctx_chunks, lambda j, _: step(j, "ctx"), None)
    lax.fori_loop(0, half, lambda j, _: step(j, "store"), None)
    lax.fori_loop(half, n_lat_chunks, lambda j, _: step(j, "add"), None)


def _scan(tk, q, kt, v, gc, at):
    d = v.shape[1]
    lc = SCAN_CHUNK
    hq = ML_HEADS * ML_QK
    n_lat, n_ctx = tk.n_lat, tk.n_ctx
    n_lat_chunks, n_ctx_chunks = n_lat // lc, n_ctx // lc
    assert n_lat_chunks % 2 == 0
    ctx0 = tk.lat_rows // n_ctx
    ctx_chunk0 = tk.lat_rows // lc // n_ctx_chunks
    lat = lambda width: pl.BlockSpec((n_lat, width), lambda i: (i, 0))
    ctx = lambda width: pl.BlockSpec((n_ctx, width), lambda i: (ctx0 + i, 0))
    lat_chunks = lambda rows: pl.BlockSpec((n_lat_chunks, rows, lc), lambda i: (i, 0, 0))
    ctx_chunks = lambda rows: pl.BlockSpec((n_ctx_chunks, rows, lc), lambda i: (ctx_chunk0 + i, 0, 0))
    kern = functools.partial(_scan_kernel, n_lat_chunks=n_lat_chunks, n_ctx_chunks=n_ctx_chunks)
    return pl.pallas_call(
        kern,
        grid=(tk.bsz,),
        in_specs=[
            lat(hq), ctx(hq), lat_chunks(hq), ctx_chunks(hq), lat(d), ctx(d), lat(128), ctx(128),
            lat_chunks(ML_UNITS), ctx_chunks(ML_UNITS),
        ],
        out_specs=lat(d),
        out_shape=jax.ShapeDtypeStruct((tk.lat_rows, d), F32),
        scratch_shapes=[
            pltpu.VMEM((ML_UNITS, ML_QK, ML_V + 128), F32),
            pltpu.VMEM((ML_UNITS, 8, 128), F32),
        ],
        compiler_params=_params("parallel"),
        name="ml_scan",
    )(q, q, kt, kt, v, v, gc, gc, at, at)


def _readout_kernel(hs_ref, sig_ref, h_ref, mod_ref, ng_ref, w_ref, g_ref, b_ref, out_ref, *, alpha):
    dv = ML_V
    n_sub = hs_ref.shape[0] // SUB_TILE

    def projected(r):
        rows = slice(r * SUB_TILE, (r + 1) * SUB_TILE)
        parts = []
        for hd in range(ML_HEADS):
            x = hs_ref[rows, hd * dv:(hd + 1) * dv]
            mu = jnp.mean(x, axis=-1, keepdims=True)
            xc = x - mu
            var = jnp.mean(xc * xc, axis=-1, keepdims=True)
            parts.append(xc * lax.rsqrt(var + LN_EPS))
        hn = jnp.concatenate(parts, axis=1) * ng_ref[...]
        z = (hn * sig_ref[rows, :].astype(F32)).astype(BF16)
        return jnp.dot(z, w_ref[...], preferred_element_type=F32)

    pending = projected(0)
    for r in range(n_sub):
        rows = slice(r * SUB_TILE, (r + 1) * SUB_TILE)
        y = pending
        if r + 1 < n_sub:
            pending = projected(r + 1)
        out_ref[rows, :] = _post_norm(h_ref[rows, :], mod_ref[5:6, :], y, g_ref[...], b_ref[...], alpha)


def _readout(tk, hs, sig, h, mods, ng, w, g, b, alpha):
    d = hs.shape[1]
    return pl.pallas_call(
        functools.partial(_readout_kernel, alpha=alpha),
        grid=(tk.lat_tiles,),
        in_specs=[_tok_spec(d), _tok_spec(d), _tok_spec(d), tk.mod_spec(d),
                  _resident((1, d)), _resident(w.shape), _resident((1, d)), _resident((1, d))],
        out_specs=_tok_spec(d),
        out_shape=jax.ShapeDtypeStruct(hs.shape, F32),
        compiler_params=_params("parallel"),
        name="ml_readout",
    )(hs, sig, h, mods, ng, w, g, b)


def _rope_tables(n_lat):
    rows = n_lat // GRID_W
    row = jnp.repeat(jnp.arange(rows, dtype=jnp.int32), GRID_W).astype(F32)
    col = jnp.tile(jnp.arange(GRID_W, dtype=jnp.int32), rows).astype(F32)
    inv = ROPE_THETA ** (-jnp.arange(ROPE_PAIRS, dtype=F32) / ROPE_PAIRS)
    ar, ac = row[:, None] * inv, col[:, None] * inv
    cos = jnp.concatenate([jnp.cos(ar), jnp.cos(ac), jnp.cos(ar), jnp.cos(ac)], axis=1)
    sin = jnp.concatenate([-jnp.sin(ar), -jnp.sin(ac), jnp.sin(ar), jnp.sin(ac)], axis=1)
    cos = jnp.concatenate([cos, jnp.ones((TOKEN_TILE, ATT_HEAD_DIM), F32)], axis=0)
    sin = jnp.concatenate([sin, jnp.zeros((TOKEN_TILE, ATT_HEAD_DIM), F32)], axis=0)
    return cos, sin


def _rope_column_order():
    p = ROPE_PAIRS
    return jnp.concatenate([jnp.arange(0, p), jnp.arange(2 * p, 3 * p), jnp.arange(p, 2 * p), jnp.arange(3 * p, 4 * p)])


def kernel(x, c, ctx, c_ctx, ada_w, ada_b, ln_g, ln_b, ffn_w_in, ffn_w_out, att_w_in, att_q_gain, att_k_gain,
           att_w_out, ml_w_in, ml_gate_b, ml_conv_w, ml_conv_b, ml_norm_g, ml_w_out):
    bsz, n_lat, d = x.shape
    n_ctx = ctx.shape[1]
    depth = ada_w.shape[0]
    dff = ffn_w_out.shape[2]
    assert depth == 2, "layer 0 attention, layer 1 (last) mLSTM"
    assert dff % FF_CHUNK == 0 and TOKEN_TILE % SCAN_CHUNK == 0
    tk = _Tokens(bsz, n_lat, n_ctx)
    alpha = (2.0 * depth) ** 0.25

    mod_rows = -(-(bsz + 1) // 8) * 8
    cc = jnp.concatenate([c, c_ctx[None, :], jnp.zeros((mod_rows - bsz - 1, d), F32)], axis=0)
    mods = _ada(cc, ada_w, ada_b).reshape(depth, mod_rows, N_MOD, d)

    win = ffn_w_in.astype(BF16)
    wout = ffn_w_out.astype(BF16)
    row2 = lambda a: a.reshape(1, -1)
    cos, sin = _rope_tables(n_lat)

    m_i = mods[0]
    h = _ffn(tk, (x.reshape(bsz * n_lat, d), ctx.reshape(bsz * n_ctx, d)), m_i, tk.tiles, win[0, 0], wout[0, 0],
             row2(ln_g[0, 0]), row2(ln_b[0, 0]), 0, alpha)
    dh = ATT_HEAD_DIM
    nqk = (ATT_HEADS + ATT_KV_HEADS) * dh
    order = _rope_column_order()
    wqk = att_w_in[0][:, :nqk].reshape(d, ATT_HEADS + ATT_KV_HEADS, dh)[:, :, order].reshape(d, nqk).astype(BF16)
    wvt = att_w_in[0][:, nqk:].T.astype(BF16)
    q, k, vt = _qkv(tk, h, m_i, wqk, wvt, row2(att_q_gain[0][order]), row2(att_k_gain[0][order]), cos, sin)
    o = _attention(tk, q, k, vt)
    h = _post(tk, o, h, m_i, att_w_out[0].astype(BF16), row2(ln_g[0, 1]), row2(ln_b[0, 1]), alpha)
    h = _ffn(tk, h, m_i, tk.tiles, win[0, 1], wout[0, 1], row2(ln_g[0, 2]), row2(ln_b[0, 2]), 6, alpha)

    m_i = mods[1]
    h = _ffn(tk, h, m_i, tk.tiles, win[1, 0], wout[1, 0], row2(ln_g[1, 0]), row2(ln_b[1, 0]), 0, alpha)
    hq = ML_HEADS * ML_QK
    w = ml_w_in[0]
    wqk = w[:, :2 * hq].astype(BF16)
    wv = w[:, 2 * hq:2 * hq + d].astype(BF16)
    wo = w[:, 2 * hq + d:2 * hq + 2 * d].astype(BF16)
    perm = jnp.array([0, 2, 1, 3])
    wg = w[:, 2 * hq + 2 * d:].reshape(d, 4, ML_HEADS)[:, perm].reshape(d, 4 * ML_HEADS)
    wg = jnp.pad(wg, ((0, 0), (0, 128 - 4 * ML_HEADS))).astype(BF16)
    gb = ml_gate_b[0].reshape(4, ML_HEADS)[perm].reshape(1, 4 * ML_HEADS)
    gb = jnp.pad(gb, ((0, 0), (0, 128 - 4 * ML_HEADS)))
    q, kt, v, sig, gc, at = _mlproj(tk, h, m_i, wqk, wv, wo, wg, gb, ml_conv_w[0], row2(ml_conv_b[0]))
    hs = _scan(tk, q, kt, v, gc, at)
    h = _readout(tk, hs, sig, h, m_i, row2(ml_norm_g[0]), ml_w_out[0].astype(BF16),
                 row2(ln_g[1, 1]), row2(ln_b[1, 1]), alpha)
    h = _ffn(tk, h, m_i, tk.lat_tiles, win[1, 1], wout[1, 1], row2(ln_g[1, 2]), row2(ln_b[1, 2]), 6, alpha)
    return h.reshape(bsz, n_lat, d)
```

```python
import functools

import jax
import jax.numpy as jnp
from jax import lax
from jax.experimental import pallas as pl
from jax.experimental.pallas import tpu as pltpu

F32 = jnp.float32
BF16 = jnp.bfloat16

N_MOD = 9
GRID_W = 64
ATT_HEADS = 8
ATT_KV_HEADS = 2
ATT_GROUP = ATT_HEADS // ATT_KV_HEADS
ATT_HEAD_DIM = 128
ROPE_PAIRS = ATT_HEAD_DIM // 4
ROPE_THETA = 10000.0
ML_HEADS = 4
ML_QK = 128
ML_V = 256
ML_UNITS = 2 * ML_HEADS
LN_EPS = 1e-5
RMS_EPS = 1e-6
LOG2_E = 1.4426950408889634

TOKEN_TILE = 1024
SUB_TILE = 512
ATT_Q_TILE = 256
ATT_KEY_BLOCK = 256
FF_CHUNK = 256
FFN_HIDDEN_CHUNK = 256
SCAN_CHUNK = 128
HALO = 16
VMEM_LIMIT = 56 * 1024 * 1024


def _params(*sem):
    return pltpu.CompilerParams(dimension_semantics=sem, vmem_limit_bytes=VMEM_LIMIT)


def _resident(shape):
    n = len(shape)
    return pl.BlockSpec(shape, lambda *_: (0,) * n, pipeline_mode=pl.Buffered(1))


def _post_norm(h, gate, y, g, b, alpha):
    x = h + (gate * (1.0 / alpha)) * y
    mu = jnp.mean(x, axis=-1, keepdims=True)
    xc = x - mu
    var = jnp.mean(xc * xc, axis=-1, keepdims=True)
    return xc * lax.rsqrt(var + LN_EPS / (alpha * alpha)) * g + b


def _silu(x):
    return x * jax.nn.sigmoid(x)


def _modulated(h, mod_ref, k0):
    return h * (1.0 + mod_ref[k0 + 1:k0 + 2, :]) + mod_ref[k0:k0 + 1, :]


class _Tokens:
    def __init__(self, bsz, n_lat, n_ctx):
        tm = TOKEN_TILE
        assert n_lat % tm == 0 and (bsz * n_ctx) % tm == 0 and tm % n_ctx == 0
        self.bsz, self.n_lat, self.n_ctx = bsz, n_lat, n_ctx
        self.lat_rows = bsz * n_lat
        self.rows = bsz * (n_lat + n_ctx)
        self.tiles_per_seq = n_lat // tm
        self.lat_tiles = self.lat_rows // tm
        self.tiles = self.rows // tm

    def is_lat(self, t):
        return t < self.lat_tiles

    def mod_row(self, t):
        return jnp.where(t < self.lat_tiles, t // self.tiles_per_seq, self.bsz)

    def mod_spec(self, d):
        return pl.BlockSpec((None, N_MOD, d), lambda t: (self.mod_row(t), 0, 0))


def _tok_spec(width):
    return pl.BlockSpec((TOKEN_TILE, width), lambda t: (t, 0))


def _ada_kernel(c_ref, w_ref, b_ref, o_ref):
    s = _silu(c_ref[...]).astype(BF16)
    o_ref[...] = jnp.dot(s, w_ref[...].astype(BF16), preferred_element_type=F32) + b_ref[...]


def _ada(cc, ada_w, ada_b):
    depth, d, nd = ada_w.shape
    rows = cc.shape[0]
    return pl.pallas_call(
        _ada_kernel,
        grid=(depth, nd // d),
        in_specs=[
            pl.BlockSpec((rows, d), lambda i, j: (0, 0)),
            pl.BlockSpec((None, d, d), lambda i, j: (i, 0, j)),
            pl.BlockSpec((None, 1, d), lambda i, j: (i, 0, j)),
        ],
        out_specs=pl.BlockSpec((None, rows, d), lambda i, j: (i, 0, j)),
        out_shape=jax.ShapeDtypeStruct((depth, rows, nd), F32),
        compiler_params=_params("parallel", "parallel"),
        name="ada_mod",
    )(cc, ada_w, ada_b.reshape(depth, 1, nd))


def _ffn_kernel(*refs, k0, alpha, dff, lat_tiles, split_input):
    if split_input:
        hx_ref, hc_ref, mod_ref, win_ref, wout_ref, g_ref, b_ref, o_ref = refs
        is_lat = pl.program_id(0) < lat_tiles
        load = lambda rows: jnp.where(is_lat, hx_ref[rows, :], hc_ref[rows, :])
    else:
        h_ref, mod_ref, win_ref, wout_ref, g_ref, b_ref, o_ref = refs
        load = lambda rows: h_ref[rows, :]
    n_sub = o_ref.shape[0] // SUB_TILE
    sub = lambda r: slice(r * SUB_TILE, (r + 1) * SUB_TILE)

    def ffn(r):
        h = load(sub(r))
        xm = _modulated(h, mod_ref, k0).astype(BF16)
        acc = None
        for c0 in range(0, dff, FFN_HIDDEN_CHUNK):
            cols = slice(c0, min(c0 + FFN_HIDDEN_CHUNK, dff))
            a = jnp.dot(xm, win_ref[:, cols], preferred_element_type=F32)
            u = jnp.dot(xm, win_ref[:, dff + cols.start:dff + cols.stop], preferred_element_type=F32)
            hid = (_silu(a) * u).astype(BF16)
            y = jnp.dot(hid, wout_ref[cols, :], preferred_element_type=F32)
            acc = y if acc is None else acc + y
        return h, acc

    gate = mod_ref[k0 + 2:k0 + 3, :]
    pending = ffn(0)
    for r in range(n_sub):
        h, acc = pending
        if r + 1 < n_sub:
            pending = ffn(r + 1)
        o_ref[sub(r), :] = _post_norm(h, 0.5 * gate, acc, g_ref[...], b_ref[...], alpha)


def _ffn(tk, hs, mods, n_tiles, win, wout, g, b, k0, alpha):
    split = isinstance(hs, tuple)
    d, dff = wout.shape[1], wout.shape[0]
    tm = TOKEN_TILE
    if split:
        last_lat = tk.lat_tiles - 1
        h_specs = [pl.BlockSpec((tm, d), lambda t: (jnp.minimum(t, last_lat), 0)),
                   pl.BlockSpec((tm, d), lambda t: (jnp.maximum(t - tk.lat_tiles, 0), 0))]
        hs = list(hs)
    else:
        h_specs = [_tok_spec(d)]
        hs = [hs]
    kern = functools.partial(_ffn_kernel, k0=k0, alpha=alpha, dff=dff, lat_tiles=tk.lat_tiles, split_input=split)
    return pl.pallas_call(
        kern,
        grid=(n_tiles,),
        in_specs=h_specs + [tk.mod_spec(d), _resident(win.shape), _resident(wout.shape),
                            _resident((1, d)), _resident((1, d))],
        out_specs=_tok_spec(d),
        out_shape=jax.ShapeDtypeStruct((n_tiles * tm, d), F32),
        compiler_params=_params("parallel"),
        name="half_ffn",
    )(*hs, mods, win, wout, g, b)


def _qkv_kernel(h_ref, mod_ref, wqk_ref, wvt_ref, qg_ref, kg_ref, ct_ref, st_ref, q_ref, k_ref, vt_ref):
    xm = _modulated(h_ref[...], mod_ref, 3).astype(BF16)
    ct = ct_ref[...]
    st = st_ref[...]
    dh = ATT_HEAD_DIM

    ones_w = jnp.ones((2 * dh, dh), BF16)

    def norm_rope(x, gain):
        sq = x * x
        hi = sq.astype(BF16)
        lo = (sq - hi.astype(F32)).astype(BF16)
        ssq = jnp.dot(jnp.concatenate([hi, lo], axis=1), ones_w, preferred_element_type=F32)
        xn = x * lax.rsqrt(ssq * (1.0 / dh) + RMS_EPS) * gain
        return xn * ct + pltpu.roll(xn, dh // 2, 1) * st

    qg = qg_ref[...] * (dh ** -0.5 * LOG2_E)
    kg = kg_ref[...]
    n_pairs = (ATT_HEADS + ATT_KV_HEADS) // 2
    pair_dot = lambda i: jnp.dot(xm, wqk_ref[:, 2 * i * dh:2 * (i + 1) * dh], preferred_element_type=F32)
    pending = pair_dot(0)
    for pair in range(n_pairs):
        p = pending
        if pair + 1 < n_pairs:
            pending = pair_dot(pair + 1)
        for half in range(2):
            hd = 2 * pair + half
            x = p[:, half * dh:(half + 1) * dh]
            if hd < ATT_HEADS:
                q_ref[:, hd * dh:(hd + 1) * dh] = norm_rope(x, qg).astype(BF16)
            else:
                hk = hd - ATT_HEADS
                k_ref[:, hk * dh:(hk + 1) * dh] = norm_rope(x, kg).astype(BF16)
    vt = lax.dot_general(wvt_ref[...], xm, (((1,), (1,)), ((), ())), preferred_element_type=F32)
    vt_ref[...] = vt.astype(BF16)


def _qkv(tk, h, mods, wqk, wvt, qg, kg, ctab, stab):
    d = h.shape[1]
    tm = TOKEN_TILE
    dh = ATT_HEAD_DIM
    nq, nk = ATT_HEADS * dh, ATT_KV_HEADS * dh
    tab_spec = pl.BlockSpec((tm, dh), lambda t: (jnp.where(tk.is_lat(t), t % tk.tiles_per_seq, tk.tiles_per_seq), 0))
    return pl.pallas_call(
        _qkv_kernel,
        grid=(tk.tiles,),
        in_specs=[_tok_spec(d), tk.mod_spec(d), _resident(wqk.shape), _resident(wvt.shape),
                  _resident((1, dh)), _resident((1, dh)), tab_spec, tab_spec],
        out_specs=[_tok_spec(nq), _tok_spec(nk), pl.BlockSpec((nk, tm), lambda t: (0, t))],
        out_shape=[jax.ShapeDtypeStruct((tk.rows, nq), BF16),
                   jax.ShapeDtypeStruct((tk.rows, nk), BF16),
                   jax.ShapeDtypeStruct((nk, tk.rows), BF16)],
        compiler_params=_params("parallel"),
        name="att_qkv",
    )(h, mods, wqk, wvt, qg, kg, ctab, stab)


def _attn_kernel(q_ref, kl_ref, kc_ref, vlt_ref, vct_ref, o_ref, s_scr, *, lat_q_tiles):
    dh = ATT_HEAD_DIM
    kb = ATT_KEY_BLOCK
    tq = q_ref.shape[0]
    n_lat, n_ctx = kl_ref.shape[0], kc_ref.shape[0]
    nt = (((1,), (1,)), ((), ()))

    def key_blocks(with_lat):
        blocks = []
        if with_lat:
            blocks += [(kl_ref, vlt_ref, b * kb, b * kb) for b in range(n_lat // kb)]
        base = n_lat if with_lat else 0
        blocks += [(kc_ref, vct_ref, b * kb, base + b * kb) for b in range(n_ctx // kb)]
        return blocks

    def scores(hd, blocks):
        kv = (hd // ATT_GROUP) * dh
        q = q_ref[:, hd * dh:(hd + 1) * dh]
        m8 = None
        for k_ref, _, r0, s0 in blocks:
            s = lax.dot_general(k_ref[r0:r0 + kb, kv:kv + dh], q, nt, preferred_element_type=F32)
            s_scr[hd % 2, s0:s0 + kb, :] = s
            bm = jnp.max(s.reshape(kb // 8, 8, tq), axis=0)
            m8 = bm if m8 is None else jnp.maximum(m8, bm)
        return jnp.max(m8, axis=0, keepdims=True)

    def weighted_values(hd, blocks, m):
        kv = (hd // ATT_GROUP) * dh
        l8 = None
        ot = None
        for _, vt_ref, r0, s0 in blocks:
            p = jnp.exp2(s_scr[hd % 2, s0:s0 + kb, :] - m)
            ps = jnp.sum(p.reshape(kb // 8, 8, tq), axis=0)
            l8 = ps if l8 is None else l8 + ps
            o_blk = jnp.dot(vt_ref[kv:kv + dh, r0:r0 + kb], p.astype(BF16), preferred_element_type=F32)
            ot = o_blk if ot is None else ot + o_blk
        denom = jnp.sum(l8, axis=0, keepdims=True)
        o_ref[:, hd * dh:(hd + 1) * dh] = jnp.transpose(ot / denom).astype(BF16)

    def run(blocks):
        m = scores(0, blocks)
        for hd in range(ATT_HEADS):
            m_next = scores(hd + 1, blocks) if hd + 1 < ATT_HEADS else None
            weighted_values(hd, blocks, m)
            m = m_next

    t = pl.program_id(1)

    @pl.when(t < lat_q_tiles)
    def _():
        run(key_blocks(True))

    @pl.when(t >= lat_q_tiles)
    def _():
        run(key_blocks(False))


def _attention(tk, q, k, vt):
    tq = ATT_Q_TILE
    dh = ATT_HEAD_DIM
    nq, nk = ATT_HEADS * dh, ATT_KV_HEADS * dh
    n_lat, n_ctx = tk.n_lat, tk.n_ctx
    assert n_lat % tq == 0 and n_ctx % tq == 0
    lat_q_tiles = n_lat // tq
    q_tiles = lat_q_tiles + n_ctx // tq
    ctx_block0 = tk.lat_rows // n_ctx

    def q_row(i, t):
        return jnp.where(t < lat_q_tiles, i * lat_q_tiles + t,
                         tk.lat_rows // tq + i * (n_ctx // tq) + (t - lat_q_tiles))

    q_spec = pl.BlockSpec((tq, nq), lambda i, t: (q_row(i, t), 0))
    return pl.pallas_call(
        functools.partial(_attn_kernel, lat_q_tiles=lat_q_tiles),
        grid=(tk.bsz, q_tiles),
        in_specs=[q_spec,
                  pl.BlockSpec((n_lat, nk), lambda i, t: (i, 0)),
                  pl.BlockSpec((n_ctx, nk), lambda i, t: (ctx_block0 + i, 0)),
                  pl.BlockSpec((nk, n_lat), lambda i, t: (0, i)),
                  pl.BlockSpec((nk, n_ctx), lambda i, t: (0, ctx_block0 + i))],
        out_specs=q_spec,
        out_shape=jax.ShapeDtypeStruct(q.shape, BF16),
        scratch_shapes=[pltpu.VMEM((2, n_lat + n_ctx, tq), F32)],
        compiler_params=_params("parallel", "parallel"),
        name="att_core",
    )(q, k, k, vt, vt)


def _post_kernel(o_ref, h_ref, mod_ref, w_ref, g_ref, b_ref, out_ref, *, alpha):
    n_sub = o_ref.shape[0] // SUB_TILE
    sub = lambda r: slice(r * SUB_TILE, (r + 1) * SUB_TILE)
    proj = lambda r: jnp.dot(o_ref[sub(r), :], w_ref[...], preferred_element_type=F32)
    pending = proj(0)
    for r in range(n_sub):
        y = pending
        if r + 1 < n_sub:
            pending = proj(r + 1)
        out_ref[sub(r), :] = _post_norm(h_ref[sub(r), :], mod_ref[5:6, :], y, g_ref[...], b_ref[...], alpha)


def _post(tk, o, h, mods, w, g, b, alpha):
    d = h.shape[1]
    return pl.pallas_call(
        functools.partial(_post_kernel, alpha=alpha),
        grid=(tk.tiles,),
        in_specs=[_tok_spec(d), _tok_spec(d), tk.mod_spec(d), _resident(w.shape), _resident((1, d)), _resident((1, d))],
        out_specs=_tok_spec(d),
        out_shape=jax.ShapeDtypeStruct(h.shape, F32),
        compiler_params=_params("parallel"),
        name="att_post",
    )(o, h, mods, w, g, b)


def _mlproj_kernel(h_ref, hp_ref, hn_ref, mod_ref, wqk_ref, wv_ref, wo_ref, wg_ref, gb_ref, cw_ref, cb_ref,
                   q_ref, kt_ref, v_ref, sig_ref, gc_ref, at_ref, xm_scr, p_scr, *, tk):
    tm = TOKEN_TILE
    lc = SCAN_CHUNK
    t = pl.program_id(0)
    xm = _modulated(h_ref[...], mod_ref, 3).astype(BF16)
    xm_scr[0:HALO, :] = _modulated(hp_ref[...], mod_ref, 3).astype(BF16)
    xm_scr[HALO:HALO + tm, :] = xm
    xm_scr[HALO + tm:tm + 2 * HALO, :] = _modulated(hn_ref[...], mod_ref, 3).astype(BF16)

    pos_in_seq = t % tk.tiles_per_seq
    is_ctx = jnp.logical_not(tk.is_lat(t))
    has_prev = jnp.logical_and(tk.is_lat(t), pos_in_seq != 0).astype(F32)
    has_next = jnp.logical_and(tk.is_lat(t), pos_in_seq != tk.tiles_per_seq - 1).astype(F32)
    row8 = lax.broadcasted_iota(jnp.int32, (8, FF_CHUNK), 0)
    inner_ends = range(tk.n_ctx, tm, tk.n_ctx)

    def patched(x, fixes):
        pieces, pos = [], 0
        for slab0, r, value, cond in sorted(fixes, key=lambda f: f[0]):
            mask = row8 == r if cond is None else jnp.logical_and(row8 == r, cond)
            pieces += [x[pos:slab0], jnp.where(mask, value, x[slab0:slab0 + 8])]
            pos = slab0 + 8
        return jnp.concatenate([p for p in pieces + [x[pos:]] if p.shape[0]], axis=0)

    hq = ML_HEADS * ML_QK
    n_rounds = 2 * hq // FF_CHUNK

    def round_matmuls(c):
        cols = slice(c * FF_CHUNK, (c + 1) * FF_CHUNK)
        return (jnp.dot(xm_scr[...], wqk_ref[:, cols], preferred_element_type=F32),
                jnp.dot(xm, wv_ref[:, cols], preferred_element_type=F32),
                jnp.dot(xm, wo_ref[:, cols], preferred_element_type=F32))

    pending = round_matmuls(0)
    for c in range(n_rounds):
        cols = slice(c * FF_CHUNK, (c + 1) * FF_CHUNK)
        p, v_c, o_c = pending
        if c + 1 < n_rounds:
            pending = round_matmuls(c + 1)
        p_scr[...] = p
        main = p[HALO:HALO + tm]
        prev_row = p[HALO - 1:HALO] * has_prev
        next_row = p[HALO + tm:HALO + tm + 1] * has_next
        down = patched(p_scr[HALO - 1:HALO - 1 + tm, :],
                       [(0, 0, prev_row, None)] + [(e, 0, 0.0, is_ctx) for e in inner_ends])
        up = patched(p_scr[HALO + 1:HALO + 1 + tm, :],
                     [(tm - 8, 7, next_row, None)] + [(e - 8, 7, 0.0, is_ctx) for e in inner_ends])
        conv = down * cw_ref[0:1, cols] + main * cw_ref[1:2, cols] + up * cw_ref[2:3, cols] + cb_ref[:, cols]
        act = _silu(conv)
        if (c + 1) * FF_CHUNK <= hq:
            q_ref[:, cols] = act.astype(BF16)
        else:
            kact = act * (ML_QK ** -0.5)
            for rc in range(tm // lc):
                kt_ref[rc, c * FF_CHUNK - hq:(c + 1) * FF_CHUNK - hq, :] = (
                    jnp.transpose(kact[rc * lc:(rc + 1) * lc, :]).astype(BF16))
        v_ref[:, cols] = v_c.astype(BF16)
        sig_ref[:, cols] = jax.nn.sigmoid(o_c).astype(BF16)

    g = jnp.dot(xm, wg_ref[...], preferred_element_type=F32) + gb_ref[...]
    logf = jax.nn.log_sigmoid(g)
    r_i = lax.broadcasted_iota(jnp.int32, (lc, lc), 0)
    c_i = lax.broadcasted_iota(jnp.int32, (lc, lc), 1)
    tri_lo = (c_i <= r_i).astype(BF16)
    lane = lax.broadcasted_iota(jnp.int32, (lc, 128), 1)
    trow = lax.broadcasted_iota(jnp.int32, (lc, 128), 0)
    nh = ML_HEADS
    for c in range(tm // lc):
        rows = slice(c * lc, (c + 1) * lc)
        lf = logf[rows]
        hi = lf.astype(BF16)
        r1 = lf - hi.astype(F32)
        mid = r1.astype(BF16)
        lo = (r1 - mid.astype(F32)).astype(BF16)
        pre3 = jnp.dot(tri_lo, jnp.concatenate([hi, mid, lo], axis=1), preferred_element_type=F32)
        pre = pre3[:, 0:128] + pre3[:, 128:256] + pre3[:, 256:384]
        suf = pre[lc - 1:lc, :] - pre + lf
        bc = pltpu.roll(jnp.where(lane < 3 * nh, pre, suf), 128 - 2 * nh, 1)
        a = g[rows] - bc
        pf = a
        pb = a
        k = 1
        while k < lc:
            pf = jnp.maximum(pf, jnp.where(trow >= k, pltpu.roll(pf, k, 0), -jnp.inf))
            pb = jnp.maximum(pb, jnp.where(trow < lc - k, pltpu.roll(pb, lc - k, 0), -jnp.inf))
            k *= 2
        pm = jnp.where(lane < nh, pf, pb)
        gc_ref[rows, :] = jnp.where(lane < 2 * nh, bc,
                                    jnp.where(lane < 4 * nh, pltpu.roll(a, 2 * nh, 1), pltpu.roll(pm, 4 * nh, 1)))
        at_ref[c] = jnp.transpose(a)[0:ML_UNITS, :]


def _mlproj(tk, h, mods, wqk, wv, wo, wg, gb, cw, cb):
    d = h.shape[1]
    tm = TOKEN_TILE
    lc = SCAN_CHUNK
    hq = ML_HEADS * ML_QK
    blocks_per_tile = tm // HALO
    last_block = tk.rows // HALO - 1
    return pl.pallas_call(
        functools.partial(_mlproj_kernel, tk=tk),
        grid=(tk.tiles,),
        in_specs=[
            _tok_spec(d),
            pl.BlockSpec((HALO, d), lambda t: (jnp.maximum(t * blocks_per_tile - 1, 0), 0)),
            pl.BlockSpec((HALO, d), lambda t: (jnp.minimum((t + 1) * blocks_per_tile, last_block), 0)),
            tk.mod_spec(d),
            _resident(wqk.shape), _resident(wv.shape), _resident(wo.shape), _resident(wg.shape),
            _resident(gb.shape), _resident(cw.shape), _resident(cb.shape),
        ],
        out_specs=[
            _tok_spec(hq), pl.BlockSpec((tm // lc, hq, lc), lambda t: (t, 0, 0)),
            _tok_spec(d), _tok_spec(d), _tok_spec(128),
            pl.BlockSpec((tm // lc, ML_UNITS, lc), lambda t: (t, 0, 0)),
        ],
        out_shape=[
            jax.ShapeDtypeStruct((tk.rows, hq), BF16),
            jax.ShapeDtypeStruct((tk.rows // lc, hq, lc), BF16),
            jax.ShapeDtypeStruct((tk.rows, d), BF16),
            jax.ShapeDtypeStruct((tk.rows, d), BF16),
            jax.ShapeDtypeStruct((tk.rows, 128), F32),
            jax.ShapeDtypeStruct((tk.rows // lc, ML_UNITS, lc), F32),
        ],
        scratch_shapes=[pltpu.VMEM((tm + 2 * HALO, d), BF16), pltpu.VMEM((tm + 2 * HALO, FF_CHUNK), F32)],
        compiler_params=_params("parallel"),
        name="ml_proj",
    )(h, h, h, mods, wqk, wv, wo, wg, gb, cw, cb)


def _scan_kernel(ql_ref, qc_ref, ktl_ref, ktc_ref, vl_ref, vc_ref, gl_ref, gcx_ref, al_ref, ac_ref, o_ref,
                 ct_scr, m_scr, *, n_lat_chunks, n_ctx_chunks):
    lc = SCAN_CHUNK
    dk, dv = ML_QK, ML_V
    assert lc == 128 and dk == 128 and dv == 2 * 128
    ct_scr[...] = jnp.zeros_like(ct_scr)
    m_scr[...] = jnp.zeros_like(m_scr)
    t_i = lax.broadcasted_iota(jnp.int32, (lc, lc), 0)
    s_i = lax.broadcasted_iota(jnp.int32, (lc, lc), 1)
    masks = (s_i <= t_i, s_i >= t_i)
    ones_blk = jnp.ones((lc, 128), BF16)

    def step(j, mode):
        if mode == "ctx":
            q_ref, kt_ref, v_ref, g_ref, a_ref, n_seq = qc_ref, ktc_ref, vc_ref, gcx_ref, ac_ref, n_ctx_chunks
        else:
            q_ref, kt_ref, v_ref, g_ref, a_ref, n_seq = ql_ref, ktl_ref, vl_ref, gl_ref, al_ref, n_lat_chunks
        chunk = (j, n_seq - 1 - j)
        units = []
        for direction in range(2):
            c = chunk[direction]
            r0 = pl.multiple_of(c * lc, lc)
            gc = g_ref[pl.ds(r0, lc), :]
            at = a_ref[c]
            for hd in range(ML_HEADS):
                units.append((direction, hd, direction * ML_HEADS + hd, c, r0, gc, at))

        st = {}
        for direction, hd, u, c, r0, gc, at in units:
            last = lc - 1 if direction == 0 else 0
            kt = kt_ref[c, hd * dk:(hd + 1) * dk, :]
            v_aug = jnp.concatenate([v_ref[pl.ds(r0, lc), hd * dv:(hd + 1) * dv], ones_blk], axis=1)
            bcum = jnp.broadcast_to(gc[:, u:u + 1], (lc, 128))
            pmax = jnp.broadcast_to(gc[:, 2 * ML_UNITS + u:2 * ML_UNITS + u + 1], (lc, 128))
            a_row = at[u:u + 1, :]
            m = m_scr[u, 0:1, :]
            mu = jnp.maximum(m, pmax)
            mu_l = mu[last:last + 1, :]
            st[u] = dict(kt=kt, v_aug=v_aug, bcum=bcum, a_row=a_row, m=m, mu=mu, mu_l=mu_l, ct=ct_scr[u],
                         b_last=bcum[last:last + 1, :])
        if mode != "ctx":
            for direction, hd, u, c, r0, gc, at in units:
                x = st[u]
                x["qc"] = q_ref[pl.ds(r0, lc), hd * dk:(hd + 1) * dk]
                x["s"] = jnp.dot(x["qc"], x["kt"], preferred_element_type=F32)
            for direction, hd, u, c, r0, gc, at in units:
                x = st[u]
                w = jnp.where(masks[direction], jnp.exp(x["a_row"] - x["mu"]), 0.0)
                eq = jnp.exp(x["m"] - x["mu"]) * x["qc"].astype(F32)
                lhs = jnp.concatenate([(x["s"] * w).astype(BF16), eq.astype(BF16)], axis=1)
                rhs = jnp.concatenate([x["v_aug"], x["ct"].astype(BF16)], axis=0)
                x["num"] = jnp.dot(lhs, rhs, preferred_element_type=F32)
        for direction, hd, u, c, r0, gc, at in units:
            x = st[u]
            ktw = (x["kt"].astype(F32) * jnp.exp(x["a_row"] - x["mu_l"])).astype(BF16)
            x["upd"] = jnp.dot(ktw, x["v_aug"], preferred_element_type=F32)
        if mode != "ctx":
            for direction, hd, u, c, r0, gc, at in units:
                x = st[u]
                num = x["num"]
                inv = 1.0 / jnp.maximum(jnp.abs(num[:, dv:]), jnp.exp(-(x["bcum"] + x["mu"])))
                hout = num[:, :dv] * jnp.concatenate([inv, inv], axis=1)
                if mode == "store":
                    o_ref[pl.ds(r0, lc), hd * dv:(hd + 1) * dv] = hout
                else:
                    o_ref[pl.ds(r0, lc), hd * dv:(hd + 1) * dv] += hout
        for direction, hd, u, c, r0, gc, at in units:
            x = st[u]
            decay = jnp.exp(x["m"] - x["mu_l"])
            ct_scr[u] = x["ct"] * jnp.concatenate([decay, decay, decay], axis=1) + x["upd"]
            m_scr[u, 0:1, :] = x["b_last"] + x["mu_l"]

    half = n_lat_chunks // 2
    lax.fori_loop(0, n_ctx_chunks, lambda j, _: step(j, "ctx"), None)
    lax.fori_loop(0, half, lambda j, _: step(j, "store"), None)
    lax.fori_loop(half, n_lat_chunks, lambda j, _: step(j, "add"), None)


def _scan(tk, q, kt, v, gc, at):
    d = v.shape[1]
    lc = SCAN_CHUNK
    hq = ML_HEADS * ML_QK
    n_lat, n_ctx = tk.n_lat, tk.n_ctx
    n_lat_chunks, n_ctx_chunks = n_lat // lc, n_ctx // lc
    assert n_lat_chunks % 2 == 0
    ctx0 = tk.lat_rows // n_ctx
    ctx_chunk0 = tk.lat_rows // lc // n_ctx_chunks
    lat = lambda width: pl.BlockSpec((n_lat, width), lambda i: (i, 0))
    ctx = lambda width: pl.BlockSpec((n_ctx, width), lambda i: (ctx0 + i, 0))
    lat_chunks = lambda rows: pl.BlockSpec((n_lat_chunks, rows, lc), lambda i: (i, 0, 0))
    ctx_chunks = lambda rows: pl.BlockSpec((n_ctx_chunks, rows, lc), lambda i: (ctx_chunk0 + i, 0, 0))
    kern = functools.partial(_scan_kernel, n_lat_chunks=n_lat_chunks, n_ctx_chunks=n_ctx_chunks)
    return pl.pallas_call(
        kern,
        grid=(tk.bsz,),
        in_specs=[
            lat(hq), ctx(hq), lat_chunks(hq), ctx_chunks(hq), lat(d), ctx(d), lat(128), ctx(128),
            lat_chunks(ML_UNITS), ctx_chunks(ML_UNITS),
        ],
        out_specs=lat(d),
        out_shape=jax.ShapeDtypeStruct((tk.lat_rows, d), F32),
        scratch_shapes=[
            pltpu.VMEM((ML_UNITS, ML_QK, ML_V + 128), F32),
            pltpu.VMEM((ML_UNITS, 8, 128), F32),
        ],
        compiler_params=_params("parallel"),
        name="ml_scan",
    )(q, q, kt, kt, v, v, gc, gc, at, at)


def _readout_kernel(hs_ref, sig_ref, h_ref, mod_ref, ng_ref, w_ref, g_ref, b_ref, out_ref, *, alpha):
    dv = ML_V
    n_sub = hs_ref.shape[0] // SUB_TILE

    def projected(r):
        rows = slice(r * SUB_TILE, (r + 1) * SUB_TILE)
        parts = []
        for hd in range(ML_HEADS):
            x = hs_ref[rows, hd * dv:(hd + 1) * dv]
            mu = jnp.mean(x, axis=-1, keepdims=True)
            xc = x - mu
            var = jnp.mean(xc * xc, axis=-1, keepdims=True)
            parts.append(xc * lax.rsqrt(var + LN_EPS))
        hn = jnp.concatenate(parts, axis=1) * ng_ref[...]
        z = (hn * sig_ref[rows, :].astype(F32)).astype(BF16)
        return jnp.dot(z, w_ref[...], preferred_element_type=F32)

    pending = projected(0)
    for r in range(n_sub):
        rows = slice(r * SUB_TILE, (r + 1) * SUB_TILE)
        y = pending
        if r + 1 < n_sub:
            pending = projected(r + 1)
        out_ref[rows, :] = _post_norm(h_ref[rows, :], mod_ref[5:6, :], y, g_ref[...], b_ref[...], alpha)


def _readout(tk, hs, sig, h, mods, ng, w, g, b, alpha):
    d = hs.shape[1]
    return pl.pallas_call(
        functools.partial(_readout_kernel, alpha=alpha),
        grid=(tk.lat_tiles,),
        in_specs=[_tok_spec(d), _tok_spec(d), _tok_spec(d), tk.mod_spec(d),
                  _resident((1, d)), _resident(w.shape), _resident((1, d)), _resident((1, d))],
        out_specs=_tok_spec(d),
        out_shape=jax.ShapeDtypeStruct(hs.shape, F32),
        compiler_params=_params("parallel"),
        name="ml_readout",
    )(hs, sig, h, mods, ng, w, g, b)


def _rope_tables(n_lat):
    rows = n_lat // GRID_W
    row = jnp.repeat(jnp.arange(rows, dtype=jnp.int32), GRID_W).astype(F32)
    col = jnp.tile(jnp.arange(GRID_W, dtype=jnp.int32), rows).astype(F32)
    inv = ROPE_THETA ** (-jnp.arange(ROPE_PAIRS, dtype=F32) / ROPE_PAIRS)
    ar, ac = row[:, None] * inv, col[:, None] * inv
    cos = jnp.concatenate([jnp.cos(ar), jnp.cos(ac), jnp.cos(ar), jnp.cos(ac)], axis=1)
    sin = jnp.concatenate([-jnp.sin(ar), -jnp.sin(ac), jnp.sin(ar), jnp.sin(ac)], axis=1)
    cos = jnp.concatenate([cos, jnp.ones((TOKEN_TILE, ATT_HEAD_DIM), F32)], axis=0)
    sin = jnp.concatenate([sin, jnp.zeros((TOKEN_TILE, ATT_HEAD_DIM), F32)], axis=0)
    return cos, sin


def _rope_column_order():
    p = ROPE_PAIRS
    return jnp.concatenate([jnp.arange(0, p), jnp.arange(2 * p, 3 * p), jnp.arange(p, 2 * p), jnp.arange(3 * p, 4 * p)])


def kernel(x, c, ctx, c_ctx, ada_w, ada_b, ln_g, ln_b, ffn_w_in, ffn_w_out, att_w_in, att_q_gain, att_k_gain,
           att_w_out, ml_w_in, ml_gate_b, ml_conv_w, ml_conv_b, ml_norm_g, ml_w_out):
    bsz, n_lat, d = x.shape
    n_ctx = ctx.shape[1]
    depth = ada_w.shape[0]
    dff = ffn_w_out.shape[2]
    assert depth == 2, "layer 0 attention, layer 1 (last) mLSTM"
    assert dff % FF_CHUNK == 0 and TOKEN_TILE % SCAN_CHUNK == 0
    tk = _Tokens(bsz, n_lat, n_ctx)
    alpha = (2.0 * depth) ** 0.25

    mod_rows = -(-(bsz + 1) // 8) * 8
    cc = jnp.concatenate([c, c_ctx[None, :], jnp.zeros((mod_rows - bsz - 1, d), F32)], axis=0)
    mods = _ada(cc, ada_w, ada_b).reshape(depth, mod_rows, N_MOD, d)

    win = ffn_w_in.astype(BF16)
    wout = ffn_w_out.astype(BF16)
    row2 = lambda a: a.reshape(1, -1)
    cos, sin = _rope_tables(n_lat)

    m_i = mods[0]
    h = _ffn(tk, (x.reshape(bsz * n_lat, d), ctx.reshape(bsz * n_ctx, d)), m_i, tk.tiles, win[0, 0], wout[0, 0],
             row2(ln_g[0, 0]), row2(ln_b[0, 0]), 0, alpha)
    dh = ATT_HEAD_DIM
    nqk = (ATT_HEADS + ATT_KV_HEADS) * dh
    order = _rope_column_order()
    wqk = att_w_in[0][:, :nqk].reshape(d, ATT_HEADS + ATT_KV_HEADS, dh)[:, :, order].reshape(d, nqk).astype(BF16)
    wvt = att_w_in[0][:, nqk:].T.astype(BF16)
    q, k, vt = _qkv(tk, h, m_i, wqk, wvt, row2(att_q_gain[0][order]), row2(att_k_gain[0][order]), cos, sin)
    o = _attention(tk, q, k, vt)
    h = _post(tk, o, h, m_i, att_w_out[0].astype(BF16), row2(ln_g[0, 1]), row2(ln_b[0, 1]), alpha)
    h = _ffn(tk, h, m_i, tk.tiles, win[0, 1], wout[0, 1], row2(ln_g[0, 2]), row2(ln_b[0, 2]), 6, alpha)

    m_i = mods[1]
    h = _ffn(tk, h, m_i, tk.tiles, win[1, 0], wout[1, 0], row2(ln_g[1, 0]), row2(ln_b[1, 0]), 0, alpha)
    hq = ML_HEADS * ML_QK
    w = ml_w_in[0]
    wqk = w[:, :2 * hq].astype(BF16)
    wv = w[:, 2 * hq:2 * hq + d].astype(BF16)
    wo = w[:, 2 * hq + d:2 * hq + 2 * d].astype(BF16)
    perm = jnp.array([0, 2, 1, 3])
    wg = w[:, 2 * hq + 2 * d:].reshape(d, 4, ML_HEADS)[:, perm].reshape(d, 4 * ML_HEADS)
    wg = jnp.pad(wg, ((0, 0), (0, 128 - 4 * ML_HEADS))).astype(BF16)
    gb = ml_gate_b[0].reshape(4, ML_HEADS)[perm].reshape(1, 4 * ML_HEADS)
    gb = jnp.pad(gb, ((0, 0), (0, 128 - 4 * ML_HEADS)))
    q, kt, v, sig, gc, at = _mlproj(tk, h, m_i, wqk, wv, wo, wg, gb, ml_conv_w[0], row2(ml_conv_b[0]))
    hs = _scan(tk, q, kt, v, gc, at)
    h = _readout(tk, hs, sig, h, m_i, row2(ml_norm_g[0]), ml_w_out[0].astype(BF16),
                 row2(ln_g[1, 1]), row2(ln_b[1, 1]), alpha)
    h = _ffn(tk, h, m_i, tk.lat_tiles, win[1, 1], wout[1, 1], row2(ln_g[1, 2]), row2(ln_b[1, 2]), 6, alpha)
    return h.reshape(bsz, n_lat, d)
```

```python
import functools

import jax
import jax.numpy as jnp
from jax import lax
from jax.experimental import pallas as pl
from jax.experimental.pallas import tpu as pltpu

F32 = jnp.float32
BF16 = jnp.bfloat16

N_MOD = 9
GRID_W = 64
ATT_HEADS = 8
ATT_KV_HEADS = 2
ATT_GROUP = ATT_HEADS // ATT_KV_HEADS
ATT_HEAD_DIM = 128
ROPE_PAIRS = ATT_HEAD_DIM // 4
ROPE_THETA = 10000.0
ML_HEADS = 4
ML_QK = 128
ML_V = 256
ML_UNITS = 2 * ML_HEADS
LN_EPS = 1e-5
RMS_EPS = 1e-6
LOG2_E = 1.4426950408889634

TOKEN_TILE = 1024
SUB_TILE = 512
ATT_Q_TILE = 256
ATT_KEY_BLOCK = 256
FF_CHUNK = 256
FFN_HIDDEN_CHUNK = 256
FFN_PIN_ROWS = 128
SCAN_CHUNK = 128
HALO = 16
VMEM_LIMIT = 56 * 1024 * 1024


def _params(*sem):
    return pltpu.CompilerParams(dimension_semantics=sem, vmem_limit_bytes=VMEM_LIMIT)


def _resident(shape, lead=()):
    n = len(shape)
    block = (None,) * len(lead) + tuple(shape)
    return pl.BlockSpec(block, lambda *_: tuple(lead) + (0,) * n, pipeline_mode=pl.Buffered(1))


def _post_norm(h, gate, y, g, b, alpha):
    x = h + (gate * (1.0 / alpha)) * y
    mu = jnp.mean(x, axis=-1, keepdims=True)
    xc = x - mu
    var = jnp.mean(xc * xc, axis=-1, keepdims=True)
    return xc * lax.rsqrt(var + LN_EPS / (alpha * alpha)) * g + b


def _silu(x):
    return x * jax.nn.sigmoid(x)


def _modulated(h, mod_ref, k0):
    return h * (1.0 + mod_ref[k0 + 1:k0 + 2, :]) + mod_ref[k0:k0 + 1, :]


class _Tokens:
    def __init__(self, bsz, n_lat, n_ctx):
        tm = TOKEN_TILE
        assert n_lat % tm == 0 and (bsz * n_ctx) % tm == 0 and tm % n_ctx == 0
        self.bsz, self.n_lat, self.n_ctx = bsz, n_lat, n_ctx
        self.lat_rows = bsz * n_lat
        self.rows = bsz * (n_lat + n_ctx)
        self.tiles_per_seq = n_lat // tm
        self.lat_tiles = self.lat_rows // tm
        self.tiles = self.rows // tm

    def is_lat(self, t):
        return t < self.lat_tiles

    def mod_row(self, t):
        return jnp.where(t < self.lat_tiles, t // self.tiles_per_seq, self.bsz)

    def mod_spec(self, d):
        return pl.BlockSpec((None, N_MOD, d), lambda t: (self.mod_row(t), 0, 0))


def _tok_spec(width):
    return pl.BlockSpec((TOKEN_TILE, width), lambda t: (t, 0))


def _ada_kernel(c_ref, w_ref, b_ref, o_ref):
    s = _silu(c_ref[...]).astype(BF16)
    o_ref[...] = jnp.dot(s, w_ref[...].astype(BF16), preferred_element_type=F32) + b_ref[...]


def _ada(cc, ada_w, ada_b):
    depth, d, nd = ada_w.shape
    rows = cc.shape[0]
    return pl.pallas_call(
        _ada_kernel,
        grid=(depth, nd // d),
        in_specs=[
            pl.BlockSpec((rows, d), lambda i, j: (0, 0)),
            pl.BlockSpec((None, d, d), lambda i, j: (i, 0, j)),
            pl.BlockSpec((None, 1, d), lambda i, j: (i, 0, j)),
        ],
        out_specs=pl.BlockSpec((None, rows, d), lambda i, j: (i, 0, j)),
        out_shape=jax.ShapeDtypeStruct((depth, rows, nd), F32),
        compiler_params=_params("parallel", "parallel"),
        name="ada_mod",
    )(cc, ada_w, ada_b.reshape(depth, 1, nd))


def _ffn_kernel(*refs, k0, alpha, dff, lat_tiles, split_input):
    if split_input:
        hx_ref, hc_ref, mod_ref, win_ref, wout_ref, g_ref, b_ref, o_ref = refs
        is_lat = pl.program_id(0) < lat_tiles
        load = lambda rows: jnp.where(is_lat, hx_ref[rows, :], hc_ref[rows, :])
    else:
        h_ref, mod_ref, win_ref, wout_ref, g_ref, b_ref, o_ref = refs
        load = lambda rows: h_ref[rows, :]
    n_sub = o_ref.shape[0] // SUB_TILE
    sub = lambda r: slice(r * SUB_TILE, (r + 1) * SUB_TILE)

    chunks = [slice(c0, min(c0 + FFN_HIDDEN_CHUNK, dff)) for c0 in range(0, dff, FFN_HIDDEN_CHUNK)]
    gate = mod_ref[k0 + 2:k0 + 3, :]

    pieces = [slice(q0, q0 + FFN_PIN_ROWS) for q0 in range(0, SUB_TILE, FFN_PIN_ROWS)]

    def ffn_chunks(xm, acc, part, anchors=()):
        for n, cols in enumerate(part):
            a = jnp.dot(xm, win_ref[:, cols], preferred_element_type=F32)
            if n < len(anchors):
                a = a + anchors[n]
            u = jnp.dot(xm, win_ref[:, dff + cols.start:dff + cols.stop], preferred_element_type=F32)
            hid = (_silu(a) * u).astype(BF16)
            y = jnp.dot(hid, wout_ref[cols, :], preferred_element_type=F32)
            acc = y if acc is None else acc + y
        return acc

    def post_ln(r, h, acc):
        outs = []
        for rows in pieces:
            out = _post_norm(h[rows], 0.5 * gate, acc[rows], g_ref[...], b_ref[...], alpha)
            o_ref[r * SUB_TILE + rows.start:r * SUB_TILE + rows.stop, :] = out
            outs.append(out)
        return outs

    def zero_after(x):
        tile_rows = 8 * 4 // x.dtype.itemsize
        red = jnp.max(x.reshape(x.shape[0] // tile_rows, tile_rows, x.shape[1]), axis=0).astype(F32)
        red = functools.reduce(jnp.maximum, [red[:, l:l + 128] for l in range(0, x.shape[1], 128)])
        bits = pltpu.bitcast(red[0:8, :], jnp.uint32)
        zero = ((bits >> 16) >> 16).astype(F32)[0:1, :]
        return jnp.concatenate([zero] * (FFN_HIDDEN_CHUNK // 128), axis=1)

    n_pin = len(pieces)
    tail = len(chunks) - n_pin
    head = 2
    assert head + n_pin <= tail
    h = load(sub(0))
    xm = _modulated(h, mod_ref, k0).astype(BF16)
    for r in range(n_sub):
        last = r + 1 == n_sub
        acc = ffn_chunks(xm, None, chunks[:tail]) if r == 0 else acc
        if not last:
            h_next = load(sub(r + 1))
            xm_parts = [_modulated(h_next[rows], mod_ref, k0).astype(BF16) for rows in pieces]
            xm_next = jnp.concatenate(xm_parts, axis=0)
            acc = ffn_chunks(xm, acc, chunks[tail:], [zero_after(p) for p in xm_parts])
            acc_next = ffn_chunks(xm_next, None, chunks[:head])
            ln_anchors = [zero_after(p) for p in post_ln(r, h, acc)]
            acc = ffn_chunks(xm_next, acc_next, chunks[head:tail], ln_anchors)
            h, xm = h_next, xm_next
        else:
            acc = ffn_chunks(xm, acc, chunks[tail:])
            post_ln(r, h, acc)


def _ffn(tk, hs, mods, n_tiles, win, wout, which, g, b, k0, alpha):
    split = isinstance(hs, tuple)
    dff, d = wout.shape[-2:]
    tm = TOKEN_TILE
    if split:
        last_lat = tk.lat_tiles - 1
        h_specs = [pl.BlockSpec((tm, d), lambda t: (jnp.minimum(t, last_lat), 0)),
                   pl.BlockSpec((tm, d), lambda t: (jnp.maximum(t - tk.lat_tiles, 0), 0))]
        hs = list(hs)
    else:
        h_specs = [_tok_spec(d)]
        hs = [hs]
    kern = functools.partial(_ffn_kernel, k0=k0, alpha=alpha, dff=dff, lat_tiles=tk.lat_tiles, split_input=split)
    return pl.pallas_call(
        kern,
        grid=(n_tiles,),
        in_specs=h_specs + [tk.mod_spec(d), _resident(win.shape[-2:], which), _resident(wout.shape[-2:], which),
                            _resident((1, d)), _resident((1, d))],
        out_specs=_tok_spec(d),
        out_shape=jax.ShapeDtypeStruct((n_tiles * tm, d), F32),
        compiler_params=_params("parallel"),
        name="half_ffn",
    )(*hs, mods, win, wout, g, b)


def _qkv_kernel(h_ref, mod_ref, wqk_ref, wvt_ref, qg_ref, kg_ref, ct_ref, st_ref, q_ref, k_ref, vt_ref):
    xm = _modulated(h_ref[...], mod_ref, 3).astype(BF16)
    ct = ct_ref[...]
    st = st_ref[...]
    dh = ATT_HEAD_DIM

    ones_w = jnp.ones((2 * dh, dh), BF16)

    def norm_rope(x, gain):
        sq = x * x
        hi = sq.astype(BF16)
        lo = (sq - hi.astype(F32)).astype(BF16)
        ssq = jnp.dot(jnp.concatenate([hi, lo], axis=1), ones_w, preferred_element_type=F32)
        xn = x * lax.rsqrt(ssq * (1.0 / dh) + RMS_EPS) * gain
        return xn * ct + pltpu.roll(xn, dh // 2, 1) * st

    qg = qg_ref[...] * (dh ** -0.5 * LOG2_E)
    kg = kg_ref[...]
    n_pairs = (ATT_HEADS + ATT_KV_HEADS) // 2
    pair_dot = lambda i: jnp.dot(xm, wqk_ref[:, 2 * i * dh:2 * (i + 1) * dh], preferred_element_type=F32)
    pending = pair_dot(0)
    for pair in range(n_pairs):
        p = pending
        if pair + 1 < n_pairs:
            pending = pair_dot(pair + 1)
        for half in range(2):
            hd = 2 * pair + half
            x = p[:, half * dh:(half + 1) * dh]
            if hd < ATT_HEADS:
                q_ref[:, hd * dh:(hd + 1) * dh] = norm_rope(x, qg).astype(BF16)
            else:
                hk = hd - ATT_HEADS
                k_ref[:, hk * dh:(hk + 1) * dh] = norm_rope(x, kg).astype(BF16)
    vt = lax.dot_general(wvt_ref[...], xm, (((1,), (1,)), ((), ())), preferred_element_type=F32)
    vt_ref[...] = vt.astype(BF16)


def _qkv(tk, h, mods, wqk, wvt, qg, kg, ctab, stab):
    d = h.shape[1]
    tm = TOKEN_TILE
    dh = ATT_HEAD_DIM
    nq, nk = ATT_HEADS * dh, ATT_KV_HEADS * dh
    tab_spec = pl.BlockSpec((tm, dh), lambda t: (jnp.where(tk.is_lat(t), t % tk.tiles_per_seq, tk.tiles_per_seq), 0))
    return pl.pallas_call(
        _qkv_kernel,
        grid=(tk.tiles,),
        in_specs=[_tok_spec(d), tk.mod_spec(d), _resident(wqk.shape), _resident(wvt.shape),
                  _resident((1, dh)), _resident((1, dh)), tab_spec, tab_spec],
        out_specs=[_tok_spec(nq), _tok_spec(nk), pl.BlockSpec((nk, tm), lambda t: (0, t))],
        out_shape=[jax.ShapeDtypeStruct((tk.rows, nq), BF16),
                   jax.ShapeDtypeStruct((tk.rows, nk), BF16),
                   jax.ShapeDtypeStruct((nk, tk.rows), BF16)],
        compiler_params=_params("parallel"),
        name="att_qkv",
    )(h, mods, wqk, wvt, qg, kg, ctab, stab)


def _attn_kernel(q_ref, kl_ref, kc_ref, vlt_ref, vct_ref, o_ref, s_scr, *, lat_q_tiles):
    dh = ATT_HEAD_DIM
    kb = ATT_KEY_BLOCK
    tq = q_ref.shape[0]
    n_lat, n_ctx = kl_ref.shape[0], kc_ref.shape[0]
    nt = (((1,), (1,)), ((), ()))

    def key_blocks(with_lat):
        blocks = []
        if with_lat:
            blocks += [(kl_ref, vlt_ref, b * kb, b * kb) for b in range(n_lat // kb)]
        base = n_lat if with_lat else 0
        blocks += [(kc_ref, vct_ref, b * kb, base + b * kb) for b in range(n_ctx // kb)]
        return blocks

    def scores(hd, blocks):
        kv = (hd // ATT_GROUP) * dh
        q = q_ref[:, hd * dh:(hd + 1) * dh]
        m8 = None
        for k_ref, _, r0, s0 in blocks:
            s = lax.dot_general(k_ref[r0:r0 + kb, kv:kv + dh], q, nt, preferred_element_type=F32)
            s_scr[hd % 2, s0:s0 + kb, :] = s
            bm = jnp.max(s.reshape(kb // 8, 8, tq), axis=0)
            m8 = bm if m8 is None else jnp.maximum(m8, bm)
        return jnp.max(m8, axis=0, keepdims=True)

    def weighted_values(hd, blocks, m):
        kv = (hd // ATT_GROUP) * dh
        l8 = None
        ot = None
        for _, vt_ref, r0, s0 in blocks:
            p = jnp.exp2(s_scr[hd % 2, s0:s0 + kb, :] - m)
            ps = jnp.sum(p.reshape(kb // 8, 8, tq), axis=0)
            l8 = ps if l8 is None else l8 + ps
            o_blk = jnp.dot(vt_ref[kv:kv + dh, r0:r0 + kb], p.astype(BF16), preferred_element_type=F32)
            ot = o_blk if ot is None else ot + o_blk
        denom = jnp.sum(l8, axis=0, keepdims=True)
        o_ref[:, hd * dh:(hd + 1) * dh] = jnp.transpose(ot / denom).astype(BF16)

    def run(blocks):
        m = scores(0, blocks)
        for hd in range(ATT_HEADS):
            m_next = scores(hd + 1, blocks) if hd + 1 < ATT_HEADS else None
            weighted_values(hd, blocks, m)
            m = m_next

    t = pl.program_id(1)

    @pl.when(t < lat_q_tiles)
    def _():
        run(key_blocks(True))

    @pl.when(t >= lat_q_tiles)
    def _():
        run(key_blocks(False))


def _attention(tk, q, k, vt):
    tq = ATT_Q_TILE
    dh = ATT_HEAD_DIM
    nq, nk = ATT_HEADS * dh, ATT_KV_HEADS * dh
    n_lat, n_ctx = tk.n_lat, tk.n_ctx
    assert n_lat % tq == 0 and n_ctx % tq == 0
    lat_q_tiles = n_lat // tq
    q_tiles = lat_q_tiles + n_ctx // tq
    ctx_block0 = tk.lat_rows // n_ctx

    def q_row(i, t):
        return jnp.where(t < lat_q_tiles, i * lat_q_tiles + t,
                         tk.lat_rows // tq + i * (n_ctx // tq) + (t - lat_q_tiles))

    q_spec = pl.BlockSpec((tq, nq), lambda i, t: (q_row(i, t), 0))
    return pl.pallas_call(
        functools.partial(_attn_kernel, lat_q_tiles=lat_q_tiles),
        grid=(tk.bsz, q_tiles),
        in_specs=[q_spec,
                  pl.BlockSpec((n_lat, nk), lambda i, t: (i, 0)),
                  pl.BlockSpec((n_ctx, nk), lambda i, t: (ctx_block0 + i, 0)),
                  pl.BlockSpec((nk, n_lat), lambda i, t: (0, i)),
                  pl.BlockSpec((nk, n_ctx), lambda i, t: (0, ctx_block0 + i))],
        out_specs=q_spec,
        out_shape=jax.ShapeDtypeStruct(q.shape, BF16),
        scratch_shapes=[pltpu.VMEM((2, n_lat + n_ctx, tq), F32)],
        compiler_params=_params("parallel", "parallel"),
        name="att_core",
    )(q, k, k, vt, vt)


def _post_kernel(o_ref, h_ref, mod_ref, w_ref, g_ref, b_ref, out_ref, *, alpha):
    n_sub = o_ref.shape[0] // SUB_TILE
    sub = lambda r: slice(r * SUB_TILE, (r + 1) * SUB_TILE)
    proj = lambda r: jnp.dot(o_ref[sub(r), :], w_ref[...], preferred_element_type=F32)
    pending = proj(0)
    for r in range(n_sub):
        y = pending
        if r + 1 < n_sub:
            pending = proj(r + 1)
        out_ref[sub(r), :] = _post_norm(h_ref[sub(r), :], mod_ref[5:6, :], y, g_ref[...], b_ref[...], alpha)


def _post(tk, o, h, mods, w, g, b, alpha):
    d = h.shape[1]
    return pl.pallas_call(
        functools.partial(_post_kernel, alpha=alpha),
        grid=(tk.tiles,),
        in_specs=[_tok_spec(d), _tok_spec(d), tk.mod_spec(d), _resident(w.shape), _resident((1, d)), _resident((1, d))],
        out_specs=_tok_spec(d),
        out_shape=jax.ShapeDtypeStruct(h.shape, F32),
        compiler_params=_params("parallel"),
        name="att_post",
    )(o, h, mods, w, g, b)


def _mlproj_kernel(h_ref, hp_ref, hn_ref, mod_ref, wqk_ref, wv_ref, wo_ref, wg_ref, gb_ref, cw_ref, cb_ref,
                   q_ref, kt_ref, v_ref, sig_ref, gc_ref, at_ref, xm_scr, p_scr, *, tk):
    tm = TOKEN_TILE
    lc = SCAN_CHUNK
    t = pl.program_id(0)
    xm = _modulated(h_ref[...], mod_ref, 3).astype(BF16)
    xm_scr[0:HALO, :] = _modulated(hp_ref[...], mod_ref, 3).astype(BF16)
    xm_scr[HALO:HALO + tm, :] = xm
    xm_scr[HALO + tm:tm + 2 * HALO, :] = _modulated(hn_ref[...], mod_ref, 3).astype(BF16)

    pos_in_seq = t % tk.tiles_per_seq
    is_ctx = jnp.logical_not(tk.is_lat(t))
    has_prev = jnp.logical_and(tk.is_lat(t), pos_in_seq != 0).astype(F32)
    has_next = jnp.logical_and(tk.is_lat(t), pos_in_seq != tk.tiles_per_seq - 1).astype(F32)
    row8 = lax.broadcasted_iota(jnp.int32, (8, FF_CHUNK), 0)
    inner_ends = range(tk.n_ctx, tm, tk.n_ctx)

    def patched(x, fixes):
        pieces, pos = [], 0
        for slab0, r, value, cond in sorted(fixes, key=lambda f: f[0]):
            mask = row8 == r if cond is None else jnp.logical_and(row8 == r, cond)
            pieces += [x[pos:slab0], jnp.where(mask, value, x[slab0:slab0 + 8])]
            pos = slab0 + 8
        return jnp.concatenate([p for p in pieces + [x[pos:]] if p.shape[0]], axis=0)

    hq = ML_HEADS * ML_QK
    n_rounds = 2 * hq // FF_CHUNK

    def round_matmuls(c):
        cols = slice(c * FF_CHUNK, (c + 1) * FF_CHUNK)
        return (jnp.dot(xm_scr[...], wqk_ref[:, cols], preferred_element_type=F32),
                jnp.dot(xm, wv_ref[:, cols], preferred_element_type=F32),
                jnp.dot(xm, wo_ref[:, cols], preferred_element_type=F32))

    pending = round_matmuls(0)
    for c in range(n_rounds):
        cols = slice(c * FF_CHUNK, (c + 1) * FF_CHUNK)
        p, v_c, o_c = pending
        if c + 1 < n_rounds:
            pending = round_matmuls(c + 1)
        p_scr[...] = p
        main = p[HALO:HALO + tm]
        prev_row = p[HALO - 1:HALO] * has_prev
        next_row = p[HALO + tm:HALO + tm + 1] * has_next
        down = patched(p_scr[HALO - 1:HALO - 1 + tm, :],
                       [(0, 0, prev_row, None)] + [(e, 0, 0.0, is_ctx) for e in inner_ends])
        up = patched(p_scr[HALO + 1:HALO + 1 + tm, :],
                     [(tm - 8, 7, next_row, None)] + [(e - 8, 7, 0.0, is_ctx) for e in inner_ends])
        conv = down * cw_ref[0:1, cols] + main * cw_ref[1:2, cols] + up * cw_ref[2:3, cols] + cb_ref[:, cols]
        act = _silu(conv)
        if (c + 1) * FF_CHUNK <= hq:
            q_ref[:, cols] = act.astype(BF16)
        else:
            kact = act * (ML_QK ** -0.5)
            for rc in range(tm // lc):
                kt_ref[rc, c * FF_CHUNK - hq:(c + 1) * FF_CHUNK - hq, :] = (
                    jnp.transpose(kact[rc * lc:(rc + 1) * lc, :]).astype(BF16))
        v_ref[:, cols] = v_c.astype(BF16)
        sig_ref[:, cols] = jax.nn.sigmoid(o_c).astype(BF16)

    g = jnp.dot(xm, wg_ref[...], preferred_element_type=F32) + gb_ref[...]
    logf = jax.nn.log_sigmoid(g)
    r_i = lax.broadcasted_iota(jnp.int32, (lc, lc), 0)
    c_i = lax.broadcasted_iota(jnp.int32, (lc, lc), 1)
    tri_lo = (c_i <= r_i).astype(BF16)
    lane = lax.broadcasted_iota(jnp.int32, (lc, 128), 1)
    trow = lax.broadcasted_iota(jnp.int32, (lc, 128), 0)
    nh = ML_HEADS
    for c in range(tm // lc):
        rows = slice(c * lc, (c + 1) * lc)
        lf = logf[rows]
        hi = lf.astype(BF16)
        r1 = lf - hi.astype(F32)
        mid = r1.astype(BF16)
        lo = (r1 - mid.astype(F32)).astype(BF16)
        pre3 = jnp.dot(tri_lo, jnp.concatenate([hi, mid, lo], axis=1), preferred_element_type=F32)
        pre = pre3[:, 0:128] + pre3[:, 128:256] + pre3[:, 256:384]
        suf = pre[lc - 1:lc, :] - pre + lf
        bc = pltpu.roll(jnp.where(lane < 3 * nh, pre, suf), 128 - 2 * nh, 1)
        a = g[rows] - bc
        pf = a
        pb = a
        k = 1
        while k < lc:
            pf = jnp.maximum(pf, jnp.where(trow >= k, pltpu.roll(pf, k, 0), -jnp.inf))
            pb = jnp.maximum(pb, jnp.where(trow < lc - k, pltpu.roll(pb, lc - k, 0), -jnp.inf))
            k *= 2
        pm = jnp.where(lane < nh, pf, pb)
        gc_ref[rows, :] = jnp.where(lane < 2 * nh, bc,
                                    jnp.where(lane < 4 * nh, pltpu.roll(a, 2 * nh, 1), pltpu.roll(pm, 4 * nh, 1)))
        at_ref[c] = jnp.transpose(a)[0:ML_UNITS, :]


def _mlproj(tk, h, mods, wqk, wv, wo, wg, gb, cw, cb):
    d = h.shape[1]
    tm = TOKEN_TILE
    lc = SCAN_CHUNK
    hq = ML_HEADS * ML_QK
    blocks_per_tile = tm // HALO
    last_block = tk.rows // HALO - 1
    return pl.pallas_call(
        functools.partial(_mlproj_kernel, tk=tk),
        grid=(tk.tiles,),
        in_specs=[
            _tok_spec(d),
            pl.BlockSpec((HALO, d), lambda t: (jnp.maximum(t * blocks_per_tile - 1, 0), 0)),
            pl.BlockSpec((HALO, d), lambda t: (jnp.minimum((t + 1) * blocks_per_tile, last_block), 0)),
            tk.mod_spec(d),
            _resident(wqk.shape), _resident(wv.shape), _resident(wo.shape), _resident(wg.shape),
            _resident(gb.shape), _resident(cw.shape), _resident(cb.shape),
        ],
        out_specs=[
            _tok_spec(hq), pl.BlockSpec((tm // lc, hq, lc), lambda t: (t, 0, 0)),
            _tok_spec(d), _tok_spec(d), _tok_spec(128),
            pl.BlockSpec((tm // lc, ML_UNITS, lc), lambda t: (t, 0, 0)),
        ],
        out_shape=[
            jax.ShapeDtypeStruct((tk.rows, hq), BF16),
            jax.ShapeDtypeStruct((tk.rows // lc, hq, lc), BF16),
            jax.ShapeDtypeStruct((tk.rows, d), BF16),
            jax.ShapeDtypeStruct((tk.rows, d), BF16),
            jax.ShapeDtypeStruct((tk.rows, 128), F32),
            jax.ShapeDtypeStruct((tk.rows // lc, ML_UNITS, lc), F32),
        ],
        scratch_shapes=[pltpu.VMEM((tm + 2 * HALO, d), BF16), pltpu.VMEM((tm + 2 * HALO, FF_CHUNK), F32)],
        compiler_params=_params("parallel"),
        name="ml_proj",
    )(h, h, h, mods, wqk, wv, wo, wg, gb, cw, cb)


def _scan_kernel(ql_ref, qc_ref, ktl_ref, ktc_ref, vl_ref, vc_ref, gl_ref, gcx_ref, al_ref, ac_ref, o_ref,
                 ct_scr, m_scr, *, n_lat_chunks, n_ctx_chunks):
    lc = SCAN_CHUNK
    dk, dv = ML_QK, ML_V
    assert lc == 128 and dk == 128 and dv == 2 * 128
    ct_scr[...] = jnp.zeros_like(ct_scr)
    m_scr[...] = jnp.zeros_like(m_scr)
    t_i = lax.broadcasted_iota(jnp.int32, (lc, lc), 0)
    s_i = lax.broadcasted_iota(jnp.int32, (lc, lc), 1)
    masks = (s_i <= t_i, s_i >= t_i)
    ones_blk = jnp.ones((lc, 128), BF16)

    def step(j, mode):
        if mode == "ctx":
            q_ref, kt_ref, v_ref, g_ref, a_ref, n_seq = qc_ref, ktc_ref, vc_ref, gcx_ref, ac_ref, n_ctx_chunks
        else:
            q_ref, kt_ref, v_ref, g_ref, a_ref, n_seq = ql_ref, ktl_ref, vl_ref, gl_ref, al_ref, n_lat_chunks
        chunk = (j, n_seq - 1 - j)
        units = []
        for direction in range(2):
            c = chunk[direction]
            r0 = pl.multiple_of(c * lc, lc)
            gc = g_ref[pl.ds(r0, lc), :]
            at = a_ref[c]
            for hd in range(ML_HEADS):
                units.append((direction, hd, direction * ML_HEADS + hd, c, r0, gc, at))

        st = {}
        for direction, hd, u, c, r0, gc, at in units:
            last = lc - 1 if direction == 0 else 0
            kt = kt_ref[c, hd * dk:(hd + 1) * dk, :]
            v_aug = jnp.concatenate([v_ref[pl.ds(r0, lc), hd * dv:(hd + 1) * dv], ones_blk], axis=1)
            bcum = jnp.broadcast_to(gc[:, u:u + 1], (lc, 128))
            pmax = jnp.broadcast_to(gc[:, 2 * ML_UNITS + u:2 * ML_UNITS + u + 1], (lc, 128))
            a_row = at[u:u + 1, :]
            m = m_scr[u, 0:1, :]
            mu = jnp.maximum(m, pmax)
            mu_l = mu[last:last + 1, :]
            st[u] = dict(kt=kt, v_aug=v_aug, bcum=bcum, a_row=a_row, m=m, mu=mu, mu_l=mu_l, ct=ct_scr[u],
                         b_last=bcum[last:last + 1, :])
        if mode != "ctx":
            for direction, hd, u, c, r0, gc, at in units:
                x = st[u]
                x["qc"] = q_ref[pl.ds(r0, lc), hd * dk:(hd + 1) * dk]
                x["s"] = jnp.dot(x["qc"], x["kt"], preferred_element_type=F32)
            for direction, hd, u, c, r0, gc, at in units:
                x = st[u]
                w = jnp.where(masks[direction], jnp.exp(x["a_row"] - x["mu"]), 0.0)
                eq = jnp.exp(x["m"] - x["mu"]) * x["qc"].astype(F32)
                lhs = jnp.concatenate([(x["s"] * w).astype(BF16), eq.astype(BF16)], axis=1)
                rhs = jnp.concatenate([x["v_aug"], x["ct"].astype(BF16)], axis=0)
                x["num"] = jnp.dot(lhs, rhs, preferred_element_type=F32)
        for direction, hd, u, c, r0, gc, at in units:
            x = st[u]
            ktw = (x["kt"].astype(F32) * jnp.exp(x["a_row"] - x["mu_l"])).astype(BF16)
            x["upd"] = jnp.dot(ktw, x["v_aug"], preferred_element_type=F32)
        if mode != "ctx":
            for direction, hd, u, c, r0, gc, at in units:
                x = st[u]
                num = x["num"]
                inv = 1.0 / jnp.maximum(jnp.abs(num[:, dv:]), jnp.exp(-(x["bcum"] + x["mu"])))
                hout = num[:, :dv] * jnp.concatenate([inv, inv], axis=1)
                if mode == "store":
                    o_ref[pl.ds(r0, lc), hd * dv:(hd + 1) * dv] = hout
                else:
                    o_ref[pl.ds(r0, lc), hd * dv:(hd + 1) * dv] += hout
        for direction, hd, u, c, r0, gc, at in units:
            x = st[u]
            decay = jnp.exp(x["m"] - x["mu_l"])
            ct_scr[u] = x["ct"] * jnp.concatenate([decay, decay, decay], axis=1) + x["upd"]
            m_scr[u, 0:1, :] = x["b_last"] + x["mu_l"]

    half = n_lat_chunks // 2
    lax.fori_loop(0, n_ctx_chunks, lambda j, _: step(j, "ctx"), None)
    lax.fori_loop(0, half, lambda j, _: step(j, "store"), None)
    lax.fori_loop(half, n_lat_chunks, lambda j, _: step(j, "add"), None)


def _scan(tk, q, kt, v, gc, at):
    d = v.shape[1]
    lc = SCAN_CHUNK
    hq = ML_HEADS * ML_QK
    n_lat, n_ctx = tk.n_lat, tk.n_ctx
    n_lat_chunks, n_ctx_chunks = n_lat // lc, n_ctx // lc
    assert n_lat_chunks % 2 == 0
    ctx0 = tk.lat_rows // n_ctx
    ctx_chunk0 = tk.lat_rows // lc // n_ctx_chunks
    lat = lambda width: pl.BlockSpec((n_lat, width), lambda i: (i, 0))
    ctx = lambda width: pl.BlockSpec((n_ctx, width), lambda i: (ctx0 + i, 0))
    lat_chunks = lambda rows: pl.BlockSpec((n_lat_chunks, rows, lc), lambda i: (i, 0, 0))
    ctx_chunks = lambda rows: pl.BlockSpec((n_ctx_chunks, rows, lc), lambda i: (ctx_chunk0 + i, 0, 0))
    kern = functools.partial(_scan_kernel, n_lat_chunks=n_lat_chunks, n_ctx_chunks=n_ctx_chunks)
    return pl.pallas_call(
        kern,
        grid=(tk.bsz,),
        in_specs=[
            lat(hq), ctx(hq), lat_chunks(hq), ctx_chunks(hq), lat(d), ctx(d), lat(128), ctx(128),
            lat_chunks(ML_UNITS), ctx_chunks(ML_UNITS),
        ],
        out_specs=lat(d),
        out_shape=jax.ShapeDtypeStruct((tk.lat_rows, d), F32),
        scratch_shapes=[
            pltpu.VMEM((ML_UNITS, ML_QK, ML_V + 128), F32),
            pltpu.VMEM((ML_UNITS, 8, 128), F32),
        ],
        compiler_params=_params("parallel"),
        name="ml_scan",
    )(q, q, kt, kt, v, v, gc, gc, at, at)


def _readout_kernel(hs_ref, sig_ref, h_ref, mod_ref, ng_ref, w_ref, g_ref, b_ref, out_ref, *, alpha):
    dv = ML_V
    n_sub = hs_ref.shape[0] // SUB_TILE

    def projected(r):
        rows = slice(r * SUB_TILE, (r + 1) * SUB_TILE)
        parts = []
        for hd in range(ML_HEADS):
            x = hs_ref[rows, hd * dv:(hd + 1) * dv]
            mu = jnp.mean(x, axis=-1, keepdims=True)
            xc = x - mu
            var = jnp.mean(xc * xc, axis=-1, keepdims=True)
            parts.append(xc * lax.rsqrt(var + LN_EPS))
        hn = jnp.concatenate(parts, axis=1) * ng_ref[...]
        z = (hn * sig_ref[rows, :].astype(F32)).astype(BF16)
        return jnp.dot(z, w_ref[...], preferred_element_type=F32)

    pending = projected(0)
    for r in range(n_sub):
        rows = slice(r * SUB_TILE, (r + 1) * SUB_TILE)
        y = pending
        if r + 1 < n_sub:
            pending = projected(r + 1)
        out_ref[rows, :] = _post_norm(h_ref[rows, :], mod_ref[5:6, :], y, g_ref[...], b_ref[...], alpha)


def _readout(tk, hs, sig, h, mods, ng, w, g, b, alpha):
    d = hs.shape[1]
    return pl.pallas_call(
        functools.partial(_readout_kernel, alpha=alpha),
        grid=(tk.lat_tiles,),
        in_specs=[_tok_spec(d), _tok_spec(d), _tok_spec(d), tk.mod_spec(d),
                  _resident((1, d)), _resident(w.shape), _resident((1, d)), _resident((1, d))],
        out_specs=_tok_spec(d),
        out_shape=jax.ShapeDtypeStruct(hs.shape, F32),
        compiler_params=_params("parallel"),
        name="ml_readout",
    )(hs, sig, h, mods, ng, w, g, b)


def _rope_tables(n_lat):
    rows = n_lat // GRID_W
    row = jnp.repeat(jnp.arange(rows, dtype=jnp.int32), GRID_W).astype(F32)
    col = jnp.tile(jnp.arange(GRID_W, dtype=jnp.int32), rows).astype(F32)
    inv = ROPE_THETA ** (-jnp.arange(ROPE_PAIRS, dtype=F32) / ROPE_PAIRS)
    ar, ac = row[:, None] * inv, col[:, None] * inv
    cos = jnp.concatenate([jnp.cos(ar), jnp.cos(ac), jnp.cos(ar), jnp.cos(ac)], axis=1)
    sin = jnp.concatenate([-jnp.sin(ar), -jnp.sin(ac), jnp.sin(ar), jnp.sin(ac)], axis=1)
    cos = jnp.concatenate([cos, jnp.ones((TOKEN_TILE, ATT_HEAD_DIM), F32)], axis=0)
    sin = jnp.concatenate([sin, jnp.zeros((TOKEN_TILE, ATT_HEAD_DIM), F32)], axis=0)
    return cos, sin


def _rope_column_order():
    p = ROPE_PAIRS
    return jnp.concatenate([jnp.arange(0, p), jnp.arange(2 * p, 3 * p), jnp.arange(p, 2 * p), jnp.arange(3 * p, 4 * p)])


def kernel(x, c, ctx, c_ctx, ada_w, ada_b, ln_g, ln_b, ffn_w_in, ffn_w_out, att_w_in, att_q_gain, att_k_gain,
           att_w_out, ml_w_in, ml_gate_b, ml_conv_w, ml_conv_b, ml_norm_g, ml_w_out):
    bsz, n_lat, d = x.shape
    n_ctx = ctx.shape[1]
    depth = ada_w.shape[0]
    dff = ffn_w_out.shape[2]
    assert depth == 2, "layer 0 attention, layer 1 (last) mLSTM"
    assert dff % FF_CHUNK == 0 and TOKEN_TILE % SCAN_CHUNK == 0
    tk = _Tokens(bsz, n_lat, n_ctx)
    alpha = (2.0 * depth) ** 0.25

    mod_rows = -(-(bsz + 1) // 8) * 8
    cc = jnp.concatenate([c, c_ctx[None, :], jnp.zeros((mod_rows - bsz - 1, d), F32)], axis=0)
    mods = _ada(cc, ada_w, ada_b).reshape(depth, mod_rows, N_MOD, d)

    win = ffn_w_in.astype(BF16)
    wout = ffn_w_out.astype(BF16)
    row2 = lambda a: a.reshape(1, -1)
    cos, sin = _rope_tables(n_lat)

    m_i = mods[0]
    h = _ffn(tk, (x.reshape(bsz * n_lat, d), ctx.reshape(bsz * n_ctx, d)), m_i, tk.tiles, win, wout, (0, 0),
             row2(ln_g[0, 0]), row2(ln_b[0, 0]), 0, alpha)
    dh = ATT_HEAD_DIM
    nqk = (ATT_HEADS + ATT_KV_HEADS) * dh
    order = _rope_column_order()
    wqk = att_w_in[0][:, :nqk].reshape(d, ATT_HEADS + ATT_KV_HEADS, dh)[:, :, order].reshape(d, nqk).astype(BF16)
    wvt = att_w_in[0][:, nqk:].T.astype(BF16)
    q, k, vt = _qkv(tk, h, m_i, wqk, wvt, row2(att_q_gain[0][order]), row2(att_k_gain[0][order]), cos, sin)
    o = _attention(tk, q, k, vt)
    h = _post(tk, o, h, m_i, att_w_out[0].astype(BF16), row2(ln_g[0, 1]), row2(ln_b[0, 1]), alpha)
    h = _ffn(tk, h, m_i, tk.tiles, win, wout, (0, 1), row2(ln_g[0, 2]), row2(ln_b[0, 2]), 6, alpha)

    m_i = mods[1]
    h = _ffn(tk, h, m_i, tk.tiles, win, wout, (1, 0), row2(ln_g[1, 0]), row2(ln_b[1, 0]), 0, alpha)
    hq = ML_HEADS * ML_QK
    w = ml_w_in[0]
    wqk = w[:, :2 * hq].astype(BF16)
    wv = w[:, 2 * hq:2 * hq + d].astype(BF16)
    wo = w[:, 2 * hq + d:2 * hq + 2 * d].astype(BF16)
    perm = jnp.array([0, 2, 1, 3])
    wg = w[:, 2 * hq + 2 * d:].reshape(d, 4, ML_HEADS)[:, perm].reshape(d, 4 * ML_HEADS)
    wg = jnp.pad(wg, ((0, 0), (0, 128 - 4 * ML_HEADS))).astype(BF16)
    gb = ml_gate_b[0].reshape(4, ML_HEADS)[perm].reshape(1, 4 * ML_HEADS)
    gb = jnp.pad(gb, ((0, 0), (0, 128 - 4 * ML_HEADS)))
    q, kt, v, sig, gc, at = _mlproj(tk, h, m_i, wqk, wv, wo, wg, gb, ml_conv_w[0], row2(ml_conv_b[0]))
    hs = _scan(tk, q, kt, v, gc, at)
    h = _readout(tk, hs, sig, h, m_i, row2(ml_norm_g[0]), ml_w_out[0].astype(BF16),
                 row2(ln_g[1, 1]), row2(ln_b[1, 1]), alpha)
    h = _ffn(tk, h, m_i, tk.lat_tiles, win, wout, (1, 1), row2(ln_g[1, 2]), row2(ln_b[1, 2]), 6, alpha)
    return h.reshape(bsz, n_lat, d)
```

```python
import functools

import jax
import jax.numpy as jnp
from jax import lax
from jax.experimental import pallas as pl
from jax.experimental.pallas import tpu as pltpu

F32 = jnp.float32
BF16 = jnp.bfloat16

N_MOD = 9
GRID_W = 64
ATT_HEADS = 8
ATT_KV_HEADS = 2
ATT_GROUP = ATT_HEADS // ATT_KV_HEADS
ATT_HEAD_DIM = 128
ROPE_PAIRS = ATT_HEAD_DIM // 4
ROPE_THETA = 10000.0
ML_HEADS = 4
ML_QK = 128
ML_V = 256
ML_UNITS = 2 * ML_HEADS
LN_EPS = 1e-5
RMS_EPS = 1e-6
LOG2_E = 1.4426950408889634

TOKEN_TILE = 1024
SUB_TILE = 512
ATT_Q_TILE = 256
ATT_KEY_BLOCK = 256
FF_CHUNK = 256
FFN_HIDDEN_CHUNK = 256
FFN_PIN_ROWS = 128
SCAN_CHUNK = 128
HALO = 16
VMEM_LIMIT = 56 * 1024 * 1024


def _params(*sem):
    return pltpu.CompilerParams(dimension_semantics=sem, vmem_limit_bytes=VMEM_LIMIT)


def _resident(shape, lead=()):
    n = len(shape)
    block = (None,) * len(lead) + tuple(shape)
    return pl.BlockSpec(block, lambda *_: tuple(lead) + (0,) * n, pipeline_mode=pl.Buffered(1))


def _post_norm(h, gate, y, g, b, alpha):
    x = h + (gate * (1.0 / alpha)) * y
    mu = jnp.mean(x, axis=-1, keepdims=True)
    xc = x - mu
    var = jnp.mean(xc * xc, axis=-1, keepdims=True)
    return xc * lax.rsqrt(var + LN_EPS / (alpha * alpha)) * g + b


def _sigmoid(x):
    return 0.5 * jnp.tanh(0.5 * x) + 0.5


def _silu(x):
    return x * _sigmoid(x)


def _modulated(h, mod_ref, k0):
    return h * (1.0 + mod_ref[k0 + 1:k0 + 2, :]) + mod_ref[k0:k0 + 1, :]


class _Tokens:
    def __init__(self, bsz, n_lat, n_ctx):
        tm = TOKEN_TILE
        assert n_lat % tm == 0 and (bsz * n_ctx) % tm == 0 and tm % n_ctx == 0
        self.bsz, self.n_lat, self.n_ctx = bsz, n_lat, n_ctx
        self.lat_rows = bsz * n_lat
        self.rows = bsz * (n_lat + n_ctx)
        self.tiles_per_seq = n_lat // tm
        self.lat_tiles = self.lat_rows // tm
        self.tiles = self.rows // tm

    def is_lat(self, t):
        return t < self.lat_tiles

    def mod_row(self, t):
        return jnp.where(t < self.lat_tiles, t // self.tiles_per_seq, self.bsz)

    def mod_spec(self, d):
        return pl.BlockSpec((None, N_MOD, d), lambda t: (self.mod_row(t), 0, 0))


def _tok_spec(width):
    return pl.BlockSpec((TOKEN_TILE, width), lambda t: (t, 0))


def _ada_kernel(c_ref, w_ref, b_ref, o_ref):
    s = _silu(c_ref[...]).astype(BF16)
    o_ref[...] = jnp.dot(s, w_ref[...].astype(BF16), preferred_element_type=F32) + b_ref[...]


def _ada(cc, ada_w, ada_b):
    depth, d, nd = ada_w.shape
    rows = cc.shape[0]
    return pl.pallas_call(
        _ada_kernel,
        grid=(depth, nd // d),
        in_specs=[
            pl.BlockSpec((rows, d), lambda i, j: (0, 0)),
            pl.BlockSpec((None, d, d), lambda i, j: (i, 0, j)),
            pl.BlockSpec((None, 1, d), lambda i, j: (i, 0, j)),
        ],
        out_specs=pl.BlockSpec((None, rows, d), lambda i, j: (i, 0, j)),
        out_shape=jax.ShapeDtypeStruct((depth, rows, nd), F32),
        compiler_params=_params("parallel", "parallel"),
        name="ada_mod",
    )(cc, ada_w, ada_b.reshape(depth, 1, nd))


def _ffn_kernel(*refs, k0, alpha, dff, lat_tiles, source):
    if source == "split":
        hx_ref, hc_ref, mod_ref, win_ref, wout_ref, g_ref, b_ref, o_ref = refs
        is_lat = pl.program_id(0) < lat_tiles
        load = lambda rows: jnp.where(is_lat, hx_ref[rows, :], hc_ref[rows, :])
    elif source == "mixer":
        z_ref, h_ref, mod_ref, wp_ref, gp_ref, bp_ref, win_ref, wout_ref, g_ref, b_ref, o_ref = refs

        def load(rows):
            y = jnp.dot(z_ref[rows, :], wp_ref[...], preferred_element_type=F32)
            return _post_norm(h_ref[rows, :], mod_ref[5:6, :], y, gp_ref[...], bp_ref[...], alpha)
    elif source == "mlstm":
        hs_ref, sig_ref, h_ref, mod_ref, ng_ref, wp_ref, gp_ref, bp_ref, win_ref, wout_ref, g_ref, b_ref, o_ref = refs

        def load(rows):
            parts = []
            for hd in range(ML_HEADS):
                x = hs_ref[rows, hd * ML_V:(hd + 1) * ML_V]
                mu = jnp.mean(x, axis=-1, keepdims=True)
                xc = x - mu
                var = jnp.mean(xc * xc, axis=-1, keepdims=True)
                parts.append(xc * lax.rsqrt(var + LN_EPS))
            hn = jnp.concatenate(parts, axis=1) * ng_ref[...]
            z = (hn * sig_ref[rows, :].astype(F32)).astype(BF16)
            y = jnp.dot(z, wp_ref[...], preferred_element_type=F32)
            return _post_norm(h_ref[rows, :], mod_ref[5:6, :], y, gp_ref[...], bp_ref[...], alpha)
    else:
        h_ref, mod_ref, win_ref, wout_ref, g_ref, b_ref, o_ref = refs
        load = lambda rows: h_ref[rows, :]
    n_sub = o_ref.shape[0] // SUB_TILE
    sub = lambda r: slice(r * SUB_TILE, (r + 1) * SUB_TILE)

    chunks = [slice(c0, min(c0 + FFN_HIDDEN_CHUNK, dff)) for c0 in range(0, dff, FFN_HIDDEN_CHUNK)]
    gate = mod_ref[k0 + 2:k0 + 3, :]

    pieces = [slice(q0, q0 + FFN_PIN_ROWS) for q0 in range(0, SUB_TILE, FFN_PIN_ROWS)]

    def ffn_chunks(xm, acc, part, anchors=()):
        for n, cols in enumerate(part):
            a = jnp.dot(xm, win_ref[:, cols], preferred_element_type=F32)
            if n < len(anchors):
                a = a + anchors[n]
            u = jnp.dot(xm, win_ref[:, dff + cols.start:dff + cols.stop], preferred_element_type=F32)
            hid = (_silu(a) * u).astype(BF16)
            y = jnp.dot(hid, wout_ref[cols, :], preferred_element_type=F32)
            acc = y if acc is None else acc + y
        return acc

    def post_ln(r, h, acc):
        outs = []
        for rows in pieces:
            out = _post_norm(h[rows], 0.5 * gate, acc[rows], g_ref[...], b_ref[...], alpha)
            o_ref[r * SUB_TILE + rows.start:r * SUB_TILE + rows.stop, :] = out
            outs.append(out)
        return outs

    def zero_after(x):
        tile_rows = 8 * 4 // x.dtype.itemsize
        red = jnp.max(x.reshape(x.shape[0] // tile_rows, tile_rows, x.shape[1]), axis=0).astype(F32)
        red = functools.reduce(jnp.maximum, [red[:, l:l + 128] for l in range(0, x.shape[1], 128)])
        bits = pltpu.bitcast(red[0:8, :], jnp.uint32)
        zero = ((bits >> 16) >> 16).astype(F32)[0:1, :]
        return jnp.concatenate([zero] * (FFN_HIDDEN_CHUNK // 128), axis=1)

    n_pin = len(pieces)
    tail = len(chunks) - n_pin
    head = 2
    assert head + n_pin <= tail
    h = load(sub(0))
    xm = _modulated(h, mod_ref, k0).astype(BF16)
    for r in range(n_sub):
        last = r + 1 == n_sub
        acc = ffn_chunks(xm, None, chunks[:tail]) if r == 0 else acc
        if not last:
            h_next = load(sub(r + 1))
            xm_parts = [_modulated(h_next[rows], mod_ref, k0).astype(BF16) for rows in pieces]
            xm_next = jnp.concatenate(xm_parts, axis=0)
            acc = ffn_chunks(xm, acc, chunks[tail:], [zero_after(p) for p in xm_parts])
            acc_next = ffn_chunks(xm_next, None, chunks[:head])
            ln_anchors = [zero_after(p) for p in post_ln(r, h, acc)]
            acc = ffn_chunks(xm_next, acc_next, chunks[head:tail], ln_anchors)
            h, xm = h_next, xm_next
        else:
            acc = ffn_chunks(xm, acc, chunks[tail:])
            post_ln(r, h, acc)


def _ffn(tk, hs, mods, n_tiles, win, wout, which, g, b, k0, alpha, mixer=None):
    dff, d = wout.shape[-2:]
    tm = TOKEN_TILE
    extra_specs, extra = [], []
    if isinstance(hs, tuple):
        source = "split"
        last_lat = tk.lat_tiles - 1
        h_specs = [pl.BlockSpec((tm, d), lambda t: (jnp.minimum(t, last_lat), 0)),
                   pl.BlockSpec((tm, d), lambda t: (jnp.maximum(t - tk.lat_tiles, 0), 0))]
        hs = list(hs)
    elif mixer is not None and len(mixer) == 4:
        source = "mixer"
        z, wp, gp, bp = mixer
        h_specs = [_tok_spec(z.shape[1]), _tok_spec(d)]
        hs = [z, hs]
        extra_specs = [_resident(wp.shape), _resident((1, d)), _resident((1, d))]
        extra = [wp, gp, bp]
    elif mixer is not None:
        source = "mlstm"
        scan_out, sig, ng, wp, gp, bp = mixer
        h_specs = [_tok_spec(d), _tok_spec(d), _tok_spec(d)]
        hs = [scan_out, sig, hs]
        extra_specs = [_resident((1, d)), _resident(wp.shape), _resident((1, d)), _resident((1, d))]
        extra = [ng, wp, gp, bp]
    else:
        source = "plain"
        h_specs = [_tok_spec(d)]
        hs = [hs]
    kern = functools.partial(_ffn_kernel, k0=k0, alpha=alpha, dff=dff, lat_tiles=tk.lat_tiles, source=source)
    return pl.pallas_call(
        kern,
        grid=(n_tiles,),
        in_specs=h_specs + [tk.mod_spec(d)] + extra_specs + [
            _resident(win.shape[-2:], which), _resident(wout.shape[-2:], which), _resident((1, d)), _resident((1, d))],
        out_specs=_tok_spec(d),
        out_shape=jax.ShapeDtypeStruct((n_tiles * tm, d), F32),
        compiler_params=_params("parallel"),
        name="half_ffn",
    )(*hs, mods, *extra, win, wout, g, b)


def _qkv_kernel(h_ref, mod_ref, wqk_ref, wvt_ref, qg_ref, kg_ref, ct_ref, st_ref, q_ref, k_ref, vt_ref):
    xm = _modulated(h_ref[...], mod_ref, 3).astype(BF16)
    ct = ct_ref[...]
    st = st_ref[...]
    dh = ATT_HEAD_DIM

    ones_w = jnp.ones((2 * dh, dh), BF16)

    def norm_rope(x, gain):
        sq = x * x
        hi = sq.astype(BF16)
        lo = (sq - hi.astype(F32)).astype(BF16)
        ssq = jnp.dot(jnp.concatenate([hi, lo], axis=1), ones_w, preferred_element_type=F32)
        xn = x * lax.rsqrt(ssq * (1.0 / dh) + RMS_EPS) * gain
        return xn * ct + pltpu.roll(xn, dh // 2, 1) * st

    qg = qg_ref[...] * (dh ** -0.5 * LOG2_E)
    kg = kg_ref[...]
    n_pairs = (ATT_HEADS + ATT_KV_HEADS) // 2
    pair_dot = lambda i: jnp.dot(xm, wqk_ref[:, 2 * i * dh:2 * (i + 1) * dh], preferred_element_type=F32)
    pending = pair_dot(0)
    for pair in range(n_pairs):
        p = pending
        if pair + 1 < n_pairs:
            pending = pair_dot(pair + 1)
        for half in range(2):
            hd = 2 * pair + half
            x = p[:, half * dh:(half + 1) * dh]
            if hd < ATT_HEADS:
                q_ref[:, hd * dh:(hd + 1) * dh] = norm_rope(x, qg).astype(BF16)
            else:
                hk = hd - ATT_HEADS
                k_ref[:, hk * dh:(hk + 1) * dh] = norm_rope(x, kg).astype(BF16)
    vt = lax.dot_general(wvt_ref[...], xm, (((1,), (1,)), ((), ())), preferred_element_type=F32)
    vt_ref[...] = vt.astype(BF16)


def _qkv(tk, h, mods, wqk, wvt, qg, kg, ctab, stab):
    d = h.shape[1]
    tm = TOKEN_TILE
    dh = ATT_HEAD_DIM
    nq, nk = ATT_HEADS * dh, ATT_KV_HEADS * dh
    tab_spec = pl.BlockSpec((tm, dh), lambda t: (jnp.where(tk.is_lat(t), t % tk.tiles_per_seq, tk.tiles_per_seq), 0))
    return pl.pallas_call(
        _qkv_kernel,
        grid=(tk.tiles,),
        in_specs=[_tok_spec(d), tk.mod_spec(d), _resident(wqk.shape), _resident(wvt.shape),
                  _resident((1, dh)), _resident((1, dh)), tab_spec, tab_spec],
        out_specs=[_tok_spec(nq), _tok_spec(nk), pl.BlockSpec((nk, tm), lambda t: (0, t))],
        out_shape=[jax.ShapeDtypeStruct((tk.rows, nq), BF16),
                   jax.ShapeDtypeStruct((tk.rows, nk), BF16),
                   jax.ShapeDtypeStruct((nk, tk.rows), BF16)],
        compiler_params=_params("parallel"),
        name="att_qkv",
    )(h, mods, wqk, wvt, qg, kg, ctab, stab)


def _attn_kernel(q_ref, kl_ref, kc_ref, vlt_ref, vct_ref, o_ref, s_scr, *, lat_q_tiles):
    dh = ATT_HEAD_DIM
    kb = ATT_KEY_BLOCK
    tq = q_ref.shape[0]
    n_lat, n_ctx = kl_ref.shape[0], kc_ref.shape[0]
    nt = (((1,), (1,)), ((), ()))

    def key_blocks(with_lat):
        blocks = []
        if with_lat:
            blocks += [(kl_ref, vlt_ref, b * kb, b * kb) for b in range(n_lat // kb)]
        base = n_lat if with_lat else 0
        blocks += [(kc_ref, vct_ref, b * kb, base + b * kb) for b in range(n_ctx // kb)]
        return blocks

    def scores(hd, blocks):
        kv = (hd // ATT_GROUP) * dh
        q = q_ref[:, hd * dh:(hd + 1) * dh]
        m8 = None
        for k_ref, _, r0, s0 in blocks:
            s = lax.dot_general(k_ref[r0:r0 + kb, kv:kv + dh], q, nt, preferred_element_type=F32)
            s_scr[hd % 2, s0:s0 + kb, :] = s
            bm = jnp.max(s.reshape(kb // 8, 8, tq), axis=0)
            m8 = bm if m8 is None else jnp.maximum(m8, bm)
        return jnp.max(m8, axis=0, keepdims=True)

    def weighted_values(hd, blocks, m):
        kv = (hd // ATT_GROUP) * dh
        l8 = None
        ot = None
        for _, vt_ref, r0, s0 in blocks:
            p = jnp.exp2(s_scr[hd % 2, s0:s0 + kb, :] - m)
            ps = jnp.sum(p.reshape(kb // 8, 8, tq), axis=0)
            l8 = ps if l8 is None else l8 + ps
            o_blk = jnp.dot(vt_ref[kv:kv + dh, r0:r0 + kb], p.astype(BF16), preferred_element_type=F32)
            ot = o_blk if ot is None else ot + o_blk
        denom = jnp.sum(l8, axis=0, keepdims=True)
        o_ref[:, hd * dh:(hd + 1) * dh] = jnp.transpose(ot / denom).astype(BF16)

    def run(blocks):
        m = scores(0, blocks)
        for hd in range(ATT_HEADS):
            m_next = scores(hd + 1, blocks) if hd + 1 < ATT_HEADS else None
            weighted_values(hd, blocks, m)
            m = m_next

    t = pl.program_id(1)

    @pl.when(t < lat_q_tiles)
    def _():
        run(key_blocks(True))

    @pl.when(t >= lat_q_tiles)
    def _():
        run(key_blocks(False))


def _attention(tk, q, k, vt):
    tq = ATT_Q_TILE
    dh = ATT_HEAD_DIM
    nq, nk = ATT_HEADS * dh, ATT_KV_HEADS * dh
    n_lat, n_ctx = tk.n_lat, tk.n_ctx
    assert n_lat % tq == 0 and n_ctx % tq == 0
    lat_q_tiles = n_lat // tq
    q_tiles = lat_q_tiles + n_ctx // tq
    ctx_block0 = tk.lat_rows // n_ctx

    def q_row(i, t):
        return jnp.where(t < lat_q_tiles, i * lat_q_tiles + t,
                         tk.lat_rows // tq + i * (n_ctx // tq) + (t - lat_q_tiles))

    q_spec = pl.BlockSpec((tq, nq), lambda i, t: (q_row(i, t), 0))
    return pl.pallas_call(
        functools.partial(_attn_kernel, lat_q_tiles=lat_q_tiles),
        grid=(tk.bsz, q_tiles),
        in_specs=[q_spec,
                  pl.BlockSpec((n_lat, nk), lambda i, t: (i, 0)),
                  pl.BlockSpec((n_ctx, nk), lambda i, t: (ctx_block0 + i, 0)),
                  pl.BlockSpec((nk, n_lat), lambda i, t: (0, i)),
                  pl.BlockSpec((nk, n_ctx), lambda i, t: (0, ctx_block0 + i))],
        out_specs=q_spec,
        out_shape=jax.ShapeDtypeStruct(q.shape, BF16),
        scratch_shapes=[pltpu.VMEM((2, n_lat + n_ctx, tq), F32)],
        compiler_params=_params("parallel", "parallel"),
        name="att_core",
    )(q, k, k, vt, vt)


def _mlproj_kernel(h_ref, hp_ref, hn_ref, mod_ref, wqk_ref, wv_ref, wo_ref, wg_ref, gb_ref, cw_ref, cb_ref,
                   q_ref, kt_ref, v_ref, sig_ref, gc_ref, at_ref, xm_scr, p_scr, *, tk):
    tm = TOKEN_TILE
    lc = SCAN_CHUNK
    t = pl.program_id(0)
    xm = _modulated(h_ref[...], mod_ref, 3).astype(BF16)
    xm_scr[0:HALO, :] = _modulated(hp_ref[...], mod_ref, 3).astype(BF16)
    xm_scr[HALO:HALO + tm, :] = xm
    xm_scr[HALO + tm:tm + 2 * HALO, :] = _modulated(hn_ref[...], mod_ref, 3).astype(BF16)

    pos_in_seq = t % tk.tiles_per_seq
    is_ctx = jnp.logical_not(tk.is_lat(t))
    has_prev = jnp.logical_and(tk.is_lat(t), pos_in_seq != 0).astype(F32)
    has_next = jnp.logical_and(tk.is_lat(t), pos_in_seq != tk.tiles_per_seq - 1).astype(F32)
    row8 = lax.broadcasted_iota(jnp.int32, (8, FF_CHUNK), 0)
    inner_ends = range(tk.n_ctx, tm, tk.n_ctx)

    def patched(x, fixes):
        pieces, pos = [], 0
        for slab0, r, value, cond in sorted(fixes, key=lambda f: f[0]):
            mask = row8 == r if cond is None else jnp.logical_and(row8 == r, cond)
            pieces += [x[pos:slab0], jnp.where(mask, value, x[slab0:slab0 + 8])]
            pos = slab0 + 8
        return jnp.concatenate([p for p in pieces + [x[pos:]] if p.shape[0]], axis=0)

    hq = ML_HEADS * ML_QK
    n_rounds = 2 * hq // FF_CHUNK

    def round_matmuls(c):
        cols = slice(c * FF_CHUNK, (c + 1) * FF_CHUNK)
        return (jnp.dot(xm_scr[...], wqk_ref[:, cols], preferred_element_type=F32),
                jnp.dot(xm, wv_ref[:, cols], preferred_element_type=F32),
                jnp.dot(xm, wo_ref[:, cols], preferred_element_type=F32))

    pending = round_matmuls(0)
    for c in range(n_rounds):
        cols = slice(c * FF_CHUNK, (c + 1) * FF_CHUNK)
        p, v_c, o_c = pending
        if c + 1 < n_rounds:
            pending = round_matmuls(c + 1)
        p_scr[...] = p
        main = p[HALO:HALO + tm]
        prev_row = p[HALO - 1:HALO] * has_prev
        next_row = p[HALO + tm:HALO + tm + 1] * has_next
        down = patched(p_scr[HALO - 1:HALO - 1 + tm, :],
                       [(0, 0, prev_row, None)] + [(e, 0, 0.0, is_ctx) for e in inner_ends])
        up = patched(p_scr[HALO + 1:HALO + 1 + tm, :],
                     [(tm - 8, 7, next_row, None)] + [(e - 8, 7, 0.0, is_ctx) for e in inner_ends])
        conv = down * cw_ref[0:1, cols] + main * cw_ref[1:2, cols] + up * cw_ref[2:3, cols] + cb_ref[:, cols]
        act = _silu(conv)
        if (c + 1) * FF_CHUNK <= hq:
            q_ref[:, cols] = act.astype(BF16)
        else:
            kact = act * (ML_QK ** -0.5)
            for rc in range(tm // lc):
                kt_ref[rc, c * FF_CHUNK - hq:(c + 1) * FF_CHUNK - hq, :] = (
                    jnp.transpose(kact[rc * lc:(rc + 1) * lc, :]).astype(BF16))
        v_ref[:, cols] = v_c.astype(BF16)
        sig_ref[:, cols] = _sigmoid(o_c).astype(BF16)

    g = jnp.dot(xm, wg_ref[...], preferred_element_type=F32) + gb_ref[...]
    logf = jax.nn.log_sigmoid(g)
    r_i = lax.broadcasted_iota(jnp.int32, (lc, lc), 0)
    c_i = lax.broadcasted_iota(jnp.int32, (lc, lc), 1)
    tri_lo = (c_i <= r_i).astype(BF16)
    lane = lax.broadcasted_iota(jnp.int32, (lc, 128), 1)
    trow = lax.broadcasted_iota(jnp.int32, (lc, 128), 0)
    nh = ML_HEADS
    for c in range(tm // lc):
        rows = slice(c * lc, (c + 1) * lc)
        lf = logf[rows]
        hi = lf.astype(BF16)
        r1 = lf - hi.astype(F32)
        mid = r1.astype(BF16)
        lo = (r1 - mid.astype(F32)).astype(BF16)
        pre3 = jnp.dot(tri_lo, jnp.concatenate([hi, mid, lo], axis=1), preferred_element_type=F32)
        pre = pre3[:, 0:128] + pre3[:, 128:256] + pre3[:, 256:384]
        suf = pre[lc - 1:lc, :] - pre + lf
        bc = pltpu.roll(jnp.where(lane < 3 * nh, pre, suf), 128 - 2 * nh, 1)
        a = g[rows] - bc
        pf = a
        pb = a
        k = 1
        while k < lc:
            pf = jnp.maximum(pf, jnp.where(trow >= k, pltpu.roll(pf, k, 0), -jnp.inf))
            pb = jnp.maximum(pb, jnp.where(trow < lc - k, pltpu.roll(pb, lc - k, 0), -jnp.inf))
            k *= 2
        pm = jnp.where(lane < nh, pf, pb)
        gc_ref[rows, :] = jnp.where(lane < 2 * nh, bc,
                                    jnp.where(lane < 4 * nh, pltpu.roll(a, 2 * nh, 1), pltpu.roll(pm, 4 * nh, 1)))
        at_ref[c] = jnp.transpose(a)[0:ML_UNITS, :]


def _mlproj(tk, h, mods, wqk, wv, wo, wg, gb, cw, cb):
    d = h.shape[1]
    tm = TOKEN_TILE
    lc = SCAN_CHUNK
    hq = ML_HEADS * ML_QK
    blocks_per_tile = tm // HALO
    last_block = tk.rows // HALO - 1
    return pl.pallas_call(
        functools.partial(_mlproj_kernel, tk=tk),
        grid=(tk.tiles,),
        in_specs=[
            _tok_spec(d),
            pl.BlockSpec((HALO, d), lambda t: (jnp.maximum(t * blocks_per_tile - 1, 0), 0)),
            pl.BlockSpec((HALO, d), lambda t: (jnp.minimum((t + 1) * blocks_per_tile, last_block), 0)),
            tk.mod_spec(d),
            _resident(wqk.shape), _resident(wv.shape), _resident(wo.shape), _resident(wg.shape),
            _resident(gb.shape), _resident(cw.shape), _resident(cb.shape),
        ],
        out_specs=[
            _tok_spec(hq), pl.BlockSpec((tm // lc, hq, lc), lambda t: (t, 0, 0)),
            _tok_spec(d), _tok_spec(d), _tok_spec(128),
            pl.BlockSpec((tm // lc, ML_UNITS, lc), lambda t: (t, 0, 0)),
        ],
        out_shape=[
            jax.ShapeDtypeStruct((tk.rows, hq), BF16),
            jax.ShapeDtypeStruct((tk.rows // lc, hq, lc), BF16),
            jax.ShapeDtypeStruct((tk.rows, d), BF16),
            jax.ShapeDtypeStruct((tk.rows, d), BF16),
            jax.ShapeDtypeStruct((tk.rows, 128), F32),
            jax.ShapeDtypeStruct((tk.rows // lc, ML_UNITS, lc), F32),
        ],
        scratch_shapes=[pltpu.VMEM((tm + 2 * HALO, d), BF16), pltpu.VMEM((tm + 2 * HALO, FF_CHUNK), F32)],
        compiler_params=_params("parallel"),
        name="ml_proj",
    )(h, h, h, mods, wqk, wv, wo, wg, gb, cw, cb)


def _scan_kernel(ql_ref, qc_ref, ktl_ref, ktc_ref, vl_ref, vc_ref, gl_ref, gcx_ref, al_ref, ac_ref, o_ref,
                 ct_scr, m_scr, *, n_lat_chunks, n_ctx_chunks):
    lc = SCAN_CHUNK
    dk, dv = ML_QK, ML_V
    assert lc == 128 and dk == 128 and dv == 2 * 128
    ct_scr[...] = jnp.zeros_like(ct_scr)
    m_scr[...] = jnp.zeros_like(m_scr)
    t_i = lax.broadcasted_iota(jnp.int32, (lc, lc), 0)
    s_i = lax.broadcasted_iota(jnp.int32, (lc, lc), 1)
    masks = (s_i <= t_i, s_i >= t_i)
    ones_blk = jnp.ones((lc, 128), BF16)

    def step(j, mode):
        if mode == "ctx":
            q_ref, kt_ref, v_ref, g_ref, a_ref, n_seq = qc_ref, ktc_ref, vc_ref, gcx_ref, ac_ref, n_ctx_chunks
        else:
            q_ref, kt_ref, v_ref, g_ref, a_ref, n_seq = ql_ref, ktl_ref, vl_ref, gl_ref, al_ref, n_lat_chunks
        chunk = (j, n_seq - 1 - j)
        units = []
        for direction in range(2):
            c = chunk[direction]
            r0 = pl.multiple_of(c * lc, lc)
            gc = g_ref[pl.ds(r0, lc), :]
            at = a_ref[c]
            for hd in range(ML_HEADS):
                units.append((direction, hd, direction * ML_HEADS + hd, c, r0, gc, at))

        st = {}
        for direction, hd, u, c, r0, gc, at in units:
            last = lc - 1 if direction == 0 else 0
            kt = kt_ref[c, hd * dk:(hd + 1) * dk, :]
            v_aug = jnp.concatenate([v_ref[pl.ds(r0, lc), hd * dv:(hd + 1) * dv], ones_blk], axis=1)
            bcum = jnp.broadcast_to(gc[:, u:u + 1], (lc, 128))
            pmax = jnp.broadcast_to(gc[:, 2 * ML_UNITS + u:2 * ML_UNITS + u + 1], (lc, 128))
            a_row = at[u:u + 1, :]
            m = m_scr[u, 0:1, :]
            mu = jnp.maximum(m, pmax)
            mu_l = mu[last:last + 1, :]
            st[u] = dict(kt=kt, v_aug=v_aug, bcum=bcum, a_row=a_row, m=m, mu=mu, mu_l=mu_l, ct=ct_scr[u],
                         b_last=bcum[last:last + 1, :])
        if mode != "ctx":
            for direction, hd, u, c, r0, gc, at in units:
                x = st[u]
                x["qc"] = q_ref[pl.ds(r0, lc), hd * dk:(hd + 1) * dk]
                x["s"] = jnp.dot(x["qc"], x["kt"], preferred_element_type=F32)
            for direction, hd, u, c, r0, gc, at in units:
                x = st[u]
                w = jnp.where(masks[direction], jnp.exp(x["a_row"] - x["mu"]), 0.0)
                eq = jnp.exp(x["m"] - x["mu"]) * x["qc"].astype(F32)
                lhs = jnp.concatenate([(x["s"] * w).astype(BF16), eq.astype(BF16)], axis=1)
                rhs = jnp.concatenate([x["v_aug"], x["ct"].astype(BF16)], axis=0)
                x["num"] = jnp.dot(lhs, rhs, preferred_element_type=F32)
        for direction, hd, u, c, r0, gc, at in units:
            x = st[u]
            ktw = (x["kt"].astype(F32) * jnp.exp(x["a_row"] - x["mu_l"])).astype(BF16)
            x["upd"] = jnp.dot(ktw, x["v_aug"], preferred_element_type=F32)
        if mode != "ctx":
            for direction, hd, u, c, r0, gc, at in units:
                x = st[u]
                num = x["num"]
                inv = 1.0 / jnp.maximum(jnp.abs(num[:, dv:]), jnp.exp(-(x["bcum"] + x["mu"])))
                hout = num[:, :dv] * jnp.concatenate([inv, inv], axis=1)
                if mode == "store":
                    o_ref[pl.ds(r0, lc), hd * dv:(hd + 1) * dv] = hout
                else:
                    o_ref[pl.ds(r0, lc), hd * dv:(hd + 1) * dv] += hout
        for direction, hd, u, c, r0, gc, at in units:
            x = st[u]
            decay = jnp.exp(x["m"] - x["mu_l"])
            ct_scr[u] = x["ct"] * jnp.concatenate([decay, decay, decay], axis=1) + x["upd"]
            m_scr[u, 0:1, :] = x["b_last"] + x["mu_l"]

    half = n_lat_chunks // 2
    lax.fori_loop(0, n_ctx_chunks, lambda j, _: step(j, "ctx"), None)
    lax.fori_loop(0, half, lambda j, _: step(j, "store"), None)
    lax.fori_loop(half, n_lat_chunks, lambda j, _: step(j, "add"), None)


def _scan(tk, q, kt, v, gc, at):
    d = v.shape[1]
    lc = SCAN_CHUNK
    hq = ML_HEADS * ML_QK
    n_lat, n_ctx = tk.n_lat, tk.n_ctx
    n_lat_chunks, n_ctx_chunks = n_lat // lc, n_ctx // lc
    assert n_lat_chunks % 2 == 0
    ctx0 = tk.lat_rows // n_ctx
    ctx_chunk0 = tk.lat_rows // lc // n_ctx_chunks
    lat = lambda width: pl.BlockSpec((n_lat, width), lambda i: (i, 0))
    ctx = lambda width: pl.BlockSpec((n_ctx, width), lambda i: (ctx0 + i, 0))
    lat_chunks = lambda rows: pl.BlockSpec((n_lat_chunks, rows, lc), lambda i: (i, 0, 0))
    ctx_chunks = lambda rows: pl.BlockSpec((n_ctx_chunks, rows, lc), lambda i: (ctx_chunk0 + i, 0, 0))
    kern = functools.partial(_scan_kernel, n_lat_chunks=n_lat_chunks, n_ctx_chunks=n_ctx_chunks)
    return pl.pallas_call(
        kern,
        grid=(tk.bsz,),
        in_specs=[
            lat(hq), ctx(hq), lat_chunks(hq), ctx_chunks(hq), lat(d), ctx(d), lat(128), ctx(128),
            lat_chunks(ML_UNITS), ctx_chunks(ML_UNITS),
        ],
        out_specs=lat(d),
        out_shape=jax.ShapeDtypeStruct((tk.lat_rows, d), F32),
        scratch_shapes=[
            pltpu.VMEM((ML_UNITS, ML_QK, ML_V + 128), F32),
            pltpu.VMEM((ML_UNITS, 8, 128), F32),
        ],
        compiler_params=_params("parallel"),
        name="ml_scan",
    )(q, q, kt, kt, v, v, gc, gc, at, at)


def _rope_tables(n_lat):
    rows = n_lat // GRID_W
    row = jnp.repeat(jnp.arange(rows, dtype=jnp.int32), GRID_W).astype(F32)
    col = jnp.tile(jnp.arange(GRID_W, dtype=jnp.int32), rows).astype(F32)
    inv = ROPE_THETA ** (-jnp.arange(ROPE_PAIRS, dtype=F32) / ROPE_PAIRS)
    ar, ac = row[:, None] * inv, col[:, None] * inv
    cos = jnp.concatenate([jnp.cos(ar), jnp.cos(ac), jnp.cos(ar), jnp.cos(ac)], axis=1)
    sin = jnp.concatenate([-jnp.sin(ar), -jnp.sin(ac), jnp.sin(ar), jnp.sin(ac)], axis=1)
    cos = jnp.concatenate([cos, jnp.ones((TOKEN_TILE, ATT_HEAD_DIM), F32)], axis=0)
    sin = jnp.concatenate([sin, jnp.zeros((TOKEN_TILE, ATT_HEAD_DIM), F32)], axis=0)
    return cos, sin


def _rope_column_order():
    p = ROPE_PAIRS
    return jnp.concatenate([jnp.arange(0, p), jnp.arange(2 * p, 3 * p), jnp.arange(p, 2 * p), jnp.arange(3 * p, 4 * p)])


def kernel(x, c, ctx, c_ctx, ada_w, ada_b, ln_g, ln_b, ffn_w_in, ffn_w_out, att_w_in, att_q_gain, att_k_gain,
           att_w_out, ml_w_in, ml_gate_b, ml_conv_w, ml_conv_b, ml_norm_g, ml_w_out):
    bsz, n_lat, d = x.shape
    n_ctx = ctx.shape[1]
    depth = ada_w.shape[0]
    dff = ffn_w_out.shape[2]
    assert depth == 2, "layer 0 attention, layer 1 (last) mLSTM"
    assert dff % FF_CHUNK == 0 and TOKEN_TILE % SCAN_CHUNK == 0
    tk = _Tokens(bsz, n_lat, n_ctx)
    alpha = (2.0 * depth) ** 0.25

    mod_rows = -(-(bsz + 1) // 8) * 8
    cc = jnp.concatenate([c, c_ctx[None, :], jnp.zeros((mod_rows - bsz - 1, d), F32)], axis=0)
    mods = _ada(cc, ada_w, ada_b).reshape(depth, mod_rows, N_MOD, d)

    win = ffn_w_in.astype(BF16)
    wout = ffn_w_out.astype(BF16)
    row2 = lambda a: a.reshape(1, -1)
    cos, sin = _rope_tables(n_lat)

    m_i = mods[0]
    h = _ffn(tk, (x.reshape(bsz * n_lat, d), ctx.reshape(bsz * n_ctx, d)), m_i, tk.tiles, win, wout, (0, 0),
             row2(ln_g[0, 0]), row2(ln_b[0, 0]), 0, alpha)
    dh = ATT_HEAD_DIM
    nqk = (ATT_HEADS + ATT_KV_HEADS) * dh
    order = _rope_column_order()
    wqk = att_w_in[0][:, :nqk].reshape(d, ATT_HEADS + ATT_KV_HEADS, dh)[:, :, order].reshape(d, nqk).astype(BF16)
    wvt = att_w_in[0][:, nqk:].T.astype(BF16)
    q, k, vt = _qkv(tk, h, m_i, wqk, wvt, row2(att_q_gain[0][order]), row2(att_k_gain[0][order]), cos, sin)
    o = _attention(tk, q, k, vt)
    h = _ffn(tk, h, m_i, tk.tiles, win, wout, (0, 1), row2(ln_g[0, 2]), row2(ln_b[0, 2]), 6, alpha,
             mixer=(o, att_w_out[0].astype(BF16), row2(ln_g[0, 1]), row2(ln_b[0, 1])))

    m_i = mods[1]
    h = _ffn(tk, h, m_i, tk.tiles, win, wout, (1, 0), row2(ln_g[1, 0]), row2(ln_b[1, 0]), 0, alpha)
    hq = ML_HEADS * ML_QK
    w = ml_w_in[0]
    wqk = w[:, :2 * hq].astype(BF16)
    wv = w[:, 2 * hq:2 * hq + d].astype(BF16)
    wo = w[:, 2 * hq + d:2 * hq + 2 * d].astype(BF16)
    perm = jnp.array([0, 2, 1, 3])
    wg = w[:, 2 * hq + 2 * d:].reshape(d, 4, ML_HEADS)[:, perm].reshape(d, 4 * ML_HEADS)
    wg = jnp.pad(wg, ((0, 0), (0, 128 - 4 * ML_HEADS))).astype(BF16)
    gb = ml_gate_b[0].reshape(4, ML_HEADS)[perm].reshape(1, 4 * ML_HEADS)
    gb = jnp.pad(gb, ((0, 0), (0, 128 - 4 * ML_HEADS)))
    q, kt, v, sig, gc, at = _mlproj(tk, h, m_i, wqk, wv, wo, wg, gb, ml_conv_w[0], row2(ml_conv_b[0]))
    hs = _scan(tk, q, kt, v, gc, at)
    h = _ffn(tk, h, m_i, tk.lat_tiles, win, wout, (1, 1), row2(ln_g[1, 2]), row2(ln_b[1, 2]), 6, alpha,
             mixer=(hs, sig, row2(ml_norm_g[0]), ml_w_out[0].astype(BF16), row2(ln_g[1, 1]), row2(ln_b[1, 1])))
    return h.reshape(bsz, n_lat, d)
```

```python
import functools

import jax
import jax.numpy as jnp
from jax import lax
from jax.experimental import pallas as pl
from jax.experimental.pallas import tpu as pltpu

F32 = jnp.float32
BF16 = jnp.bfloat16

N_MOD = 9
GRID_W = 64
ATT_HEADS = 8
ATT_KV_HEADS = 2
ATT_GROUP = ATT_HEADS // ATT_KV_HEADS
ATT_HEAD_DIM = 128
ROPE_PAIRS = ATT_HEAD_DIM // 4
ROPE_THETA = 10000.0
ML_HEADS = 4
ML_QK = 128
ML_V = 256
ML_UNITS = 2 * ML_HEADS
LN_EPS = 1e-5
RMS_EPS = 1e-6
LOG2_E = 1.4426950408889634

TOKEN_TILE = 1024
WIDE_TOKEN_TILE = 2048
SUB_TILE = 512
ATT_Q_TILE = 256
ATT_KEY_BLOCK = 256
FF_CHUNK = 256
FFN_HIDDEN_CHUNK = 256
FFN_PIN_ROWS = 128
SCAN_CHUNK = 128
HALO = 16
VMEM_LIMIT = 58 * 1024 * 1024


def _params(*sem):
    return pltpu.CompilerParams(dimension_semantics=sem, vmem_limit_bytes=VMEM_LIMIT)


def _resident(shape, lead=()):
    n = len(shape)
    block = (None,) * len(lead) + tuple(shape)
    return pl.BlockSpec(block, lambda *_: tuple(lead) + (0,) * n, pipeline_mode=pl.Buffered(1))


def _post_norm(h, gate, y, g, b, alpha):
    x = h + (gate * (1.0 / alpha)) * y
    mu = jnp.mean(x, axis=-1, keepdims=True)
    xc = x - mu
    var = jnp.mean(xc * xc, axis=-1, keepdims=True)
    return xc * lax.rsqrt(var + LN_EPS / (alpha * alpha)) * g + b


def _sigmoid(x):
    return 0.5 * jnp.tanh(0.5 * x) + 0.5


def _silu(x):
    return x * _sigmoid(x)


def _modulated(h, mod_ref, k0):
    return h * (1.0 + mod_ref[k0 + 1:k0 + 2, :]) + mod_ref[k0:k0 + 1, :]


class _Tokens:
    def __init__(self, bsz, n_lat, n_ctx, tile=TOKEN_TILE):
        tm = tile
        assert n_lat % tm == 0 and (bsz * n_ctx) % tm == 0 and tm % n_ctx == 0
        self.tile = tile
        self.bsz, self.n_lat, self.n_ctx = bsz, n_lat, n_ctx
        self.lat_rows = bsz * n_lat
        self.rows = bsz * (n_lat + n_ctx)
        self.tiles_per_seq = n_lat // tm
        self.lat_tiles = self.lat_rows // tm
        self.tiles = self.rows // tm

    def is_lat(self, t):
        return t < self.lat_tiles

    def mod_row(self, t):
        return jnp.where(t < self.lat_tiles, t // self.tiles_per_seq, self.bsz)

    def mod_spec(self, d):
        return pl.BlockSpec((None, N_MOD, d), lambda t: (self.mod_row(t), 0, 0))

    def tok(self, width):
        return pl.BlockSpec((self.tile, width), lambda t: (t, 0))


def _tok_spec(width):
    return pl.BlockSpec((TOKEN_TILE, width), lambda t: (t, 0))


def _ada_kernel(c_ref, w_ref, b_ref, o_ref):
    s = _silu(c_ref[...]).astype(BF16)
    o_ref[...] = jnp.dot(s, w_ref[...].astype(BF16), preferred_element_type=F32) + b_ref[...]


def _ada(cc, ada_w, ada_b):
    depth, d, nd = ada_w.shape
    rows = cc.shape[0]
    return pl.pallas_call(
        _ada_kernel,
        grid=(depth, nd // d),
        in_specs=[
            pl.BlockSpec((rows, d), lambda i, j: (0, 0)),
            pl.BlockSpec((None, d, d), lambda i, j: (i, 0, j)),
            pl.BlockSpec((None, 1, d), lambda i, j: (i, 0, j)),
        ],
        out_specs=pl.BlockSpec((None, rows, d), lambda i, j: (i, 0, j)),
        out_shape=jax.ShapeDtypeStruct((depth, rows, nd), F32),
        compiler_params=_params("parallel", "parallel"),
        name="ada_mod",
    )(cc, ada_w, ada_b.reshape(depth, 1, nd))


def _ffn_kernel(*refs, k0, alpha, dff, lat_tiles, source):
    if source == "split":
        hx_ref, hc_ref, mod_ref, win_ref, wout_ref, g_ref, b_ref, o_ref = refs
        is_lat = pl.program_id(0) < lat_tiles
        load = lambda rows: jnp.where(is_lat, hx_ref[rows, :], hc_ref[rows, :])
    elif source == "mixer":
        z_ref, h_ref, mod_ref, wp_ref, gp_ref, bp_ref, win_ref, wout_ref, g_ref, b_ref, o_ref = refs

        def load(rows):
            y = jnp.dot(z_ref[rows, :], wp_ref[...], preferred_element_type=F32)
            return _post_norm(h_ref[rows, :], mod_ref[5:6, :], y, gp_ref[...], bp_ref[...], alpha)
    elif source == "mlstm":
        hs_ref, sig_ref, h_ref, mod_ref, ng_ref, wp_ref, gp_ref, bp_ref, win_ref, wout_ref, g_ref, b_ref, o_ref = refs

        def load(rows):
            parts = []
            for hd in range(ML_HEADS):
                x = hs_ref[rows, hd * ML_V:(hd + 1) * ML_V]
                mu = jnp.mean(x, axis=-1, keepdims=True)
                xc = x - mu
                var = jnp.mean(xc * xc, axis=-1, keepdims=True)
                parts.append(xc * lax.rsqrt(var + LN_EPS))
            hn = jnp.concatenate(parts, axis=1) * ng_ref[...]
            z = (hn * sig_ref[rows, :].astype(F32)).astype(BF16)
            y = jnp.dot(z, wp_ref[...], preferred_element_type=F32)
            return _post_norm(h_ref[rows, :], mod_ref[5:6, :], y, gp_ref[...], bp_ref[...], alpha)
    else:
        h_ref, mod_ref, win_ref, wout_ref, g_ref, b_ref, o_ref = refs
        load = lambda rows: h_ref[rows, :]
    n_sub = o_ref.shape[0] // SUB_TILE
    sub = lambda r: slice(r * SUB_TILE, (r + 1) * SUB_TILE)

    chunks = [slice(c0, min(c0 + FFN_HIDDEN_CHUNK, dff)) for c0 in range(0, dff, FFN_HIDDEN_CHUNK)]
    gate = mod_ref[k0 + 2:k0 + 3, :]

    pieces = [slice(q0, q0 + FFN_PIN_ROWS) for q0 in range(0, SUB_TILE, FFN_PIN_ROWS)]

    def ffn_chunks(xm, acc, part, anchors=()):
        for n, cols in enumerate(part):
            a = jnp.dot(xm, win_ref[:, cols], preferred_element_type=F32)
            if n < len(anchors):
                a = a + anchors[n]
            u = jnp.dot(xm, win_ref[:, dff + cols.start:dff + cols.stop], preferred_element_type=F32)
            hid = (_silu(a) * u).astype(BF16)
            y = jnp.dot(hid, wout_ref[cols, :], preferred_element_type=F32)
            acc = y if acc is None else acc + y
        return acc

    def post_ln(r, h, acc):
        outs = []
        for rows in pieces:
            out = _post_norm(h[rows], 0.5 * gate, acc[rows], g_ref[...], b_ref[...], alpha)
            o_ref[r * SUB_TILE + rows.start:r * SUB_TILE + rows.stop, :] = out
            outs.append(out)
        return outs

    def zero_after(x):
        tile_rows = 8 * 4 // x.dtype.itemsize
        red = jnp.max(x.reshape(x.shape[0] // tile_rows, tile_rows, x.shape[1]), axis=0).astype(F32)
        red = functools.reduce(jnp.maximum, [red[:, l:l + 128] for l in range(0, x.shape[1], 128)])
        bits = pltpu.bitcast(red[0:8, :], jnp.uint32)
        zero = ((bits >> 16) >> 16).astype(F32)[0:1, :]
        return jnp.concatenate([zero] * (FFN_HIDDEN_CHUNK // 128), axis=1)

    n_pin = len(pieces)
    tail = len(chunks) - n_pin
    head = 2
    assert head + n_pin <= tail
    h = load(sub(0))
    xm = _modulated(h, mod_ref, k0).astype(BF16)
    for r in range(n_sub):
        last = r + 1 == n_sub
        acc = ffn_chunks(xm, None, chunks[:tail]) if r == 0 else acc
        if not last:
            h_next = load(sub(r + 1))
            xm_parts = [_modulated(h_next[rows], mod_ref, k0).astype(BF16) for rows in pieces]
            xm_next = jnp.concatenate(xm_parts, axis=0)
            acc = ffn_chunks(xm, acc, chunks[tail:], [zero_after(p) for p in xm_parts])
            acc_next = ffn_chunks(xm_next, None, chunks[:head])
            ln_anchors = [zero_after(p) for p in post_ln(r, h, acc)]
            acc = ffn_chunks(xm_next, acc_next, chunks[head:tail], ln_anchors)
            h, xm = h_next, xm_next
        else:
            acc = ffn_chunks(xm, acc, chunks[tail:])
            post_ln(r, h, acc)


def _ffn(tk, hs, mods, n_tiles, win, wout, which, g, b, k0, alpha, mixer=None):
    dff, d = wout.shape[-2:]
    tm = tk.tile
    _tok_spec = tk.tok
    extra_specs, extra = [], []
    if isinstance(hs, tuple):
        source = "split"
        last_lat = tk.lat_tiles - 1
        h_specs = [pl.BlockSpec((tm, d), lambda t: (jnp.minimum(t, last_lat), 0)),
                   pl.BlockSpec((tm, d), lambda t: (jnp.maximum(t - tk.lat_tiles, 0), 0))]
        hs = list(hs)
    elif mixer is not None and len(mixer) == 4:
        source = "mixer"
        z, wp, gp, bp = mixer
        h_specs = [_tok_spec(z.shape[1]), _tok_spec(d)]
        hs = [z, hs]
        extra_specs = [_resident(wp.shape), _resident((1, d)), _resident((1, d))]
        extra = [wp, gp, bp]
    elif mixer is not None:
        source = "mlstm"
        scan_out, sig, ng, wp, gp, bp = mixer
        h_specs = [_tok_spec(d), _tok_spec(d), _tok_spec(d)]
        hs = [scan_out, sig, hs]
        extra_specs = [_resident((1, d)), _resident(wp.shape), _resident((1, d)), _resident((1, d))]
        extra = [ng, wp, gp, bp]
    else:
        source = "plain"
        h_specs = [_tok_spec(d)]
        hs = [hs]
    kern = functools.partial(_ffn_kernel, k0=k0, alpha=alpha, dff=dff, lat_tiles=tk.lat_tiles, source=source)
    return pl.pallas_call(
        kern,
        grid=(n_tiles,),
        in_specs=h_specs + [tk.mod_spec(d)] + extra_specs + [
            _resident(win.shape[-2:], which), _resident(wout.shape[-2:], which), _resident((1, d)), _resident((1, d))],
        out_specs=_tok_spec(d),
        out_shape=jax.ShapeDtypeStruct((n_tiles * tm, d), F32),
        compiler_params=_params("parallel"),
        name="half_ffn",
    )(*hs, mods, *extra, win, wout, g, b)


def _qkv_kernel(h_ref, mod_ref, wqk_ref, wvt_ref, qg_ref, kg_ref, ct_ref, st_ref, q_ref, k_ref, vt_ref):
    xm = _modulated(h_ref[...], mod_ref, 3).astype(BF16)
    ct = ct_ref[...]
    st = st_ref[...]
    dh = ATT_HEAD_DIM

    ones_w = jnp.ones((2 * dh, dh), BF16)

    def norm_rope(x, gain):
        sq = x * x
        hi = sq.astype(BF16)
        lo = (sq - hi.astype(F32)).astype(BF16)
        ssq = jnp.dot(jnp.concatenate([hi, lo], axis=1), ones_w, preferred_element_type=F32)
        xn = x * lax.rsqrt(ssq * (1.0 / dh) + RMS_EPS) * gain
        return xn * ct + pltpu.roll(xn, dh // 2, 1) * st

    qg = qg_ref[...] * (dh ** -0.5 * LOG2_E)
    kg = kg_ref[...]
    n_pairs = (ATT_HEADS + ATT_KV_HEADS) // 2
    pair_dot = lambda i: jnp.dot(xm, wqk_ref[:, 2 * i * dh:2 * (i + 1) * dh], preferred_element_type=F32)
    pending = pair_dot(0)
    for pair in range(n_pairs):
        p = pending
        if pair + 1 < n_pairs:
            pending = pair_dot(pair + 1)
        for half in range(2):
            hd = 2 * pair + half
            x = p[:, half * dh:(half + 1) * dh]
            if hd < ATT_HEADS:
                q_ref[:, hd * dh:(hd + 1) * dh] = norm_rope(x, qg).astype(BF16)
            else:
                hk = hd - ATT_HEADS
                k_ref[:, hk * dh:(hk + 1) * dh] = norm_rope(x, kg).astype(BF16)
    vt = lax.dot_general(wvt_ref[...], xm, (((1,), (1,)), ((), ())), preferred_element_type=F32)
    vt_ref[...] = vt.astype(BF16)


def _qkv(tk, h, mods, wqk, wvt, qg, kg, ctab, stab):
    d = h.shape[1]
    tm = tk.tile
    _tok_spec = tk.tok
    dh = ATT_HEAD_DIM
    nq, nk = ATT_HEADS * dh, ATT_KV_HEADS * dh
    tab_spec = pl.BlockSpec((tm, dh), lambda t: (jnp.where(tk.is_lat(t), t % tk.tiles_per_seq, tk.tiles_per_seq), 0))
    return pl.pallas_call(
        _qkv_kernel,
        grid=(tk.tiles,),
        in_specs=[_tok_spec(d), tk.mod_spec(d), _resident(wqk.shape), _resident(wvt.shape),
                  _resident((1, dh)), _resident((1, dh)), tab_spec, tab_spec],
        out_specs=[_tok_spec(nq), _tok_spec(nk), pl.BlockSpec((nk, tm), lambda t: (0, t))],
        out_shape=[jax.ShapeDtypeStruct((tk.rows, nq), BF16),
                   jax.ShapeDtypeStruct((tk.rows, nk), BF16),
                   jax.ShapeDtypeStruct((nk, tk.rows), BF16)],
        compiler_params=_params("parallel"),
        name="att_qkv",
    )(h, mods, wqk, wvt, qg, kg, ctab, stab)


def _attn_kernel(q_ref, kl_ref, kc_ref, vlt_ref, vct_ref, o_ref, s_scr, *, lat_q_tiles):
    dh = ATT_HEAD_DIM
    kb = ATT_KEY_BLOCK
    tq = q_ref.shape[0]
    n_lat, n_ctx = kl_ref.shape[0], kc_ref.shape[0]
    nt = (((1,), (1,)), ((), ()))

    def key_blocks(with_lat):
        blocks = []
        if with_lat:
            blocks += [(kl_ref, vlt_ref, b * kb, b * kb) for b in range(n_lat // kb)]
        base = n_lat if with_lat else 0
        blocks += [(kc_ref, vct_ref, b * kb, base + b * kb) for b in range(n_ctx // kb)]
        return blocks

    def scores(hd, blocks):
        kv = (hd // ATT_GROUP) * dh
        q = q_ref[:, hd * dh:(hd + 1) * dh]
        m8 = None
        for k_ref, _, r0, s0 in blocks:
            s = lax.dot_general(k_ref[r0:r0 + kb, kv:kv + dh], q, nt, preferred_element_type=F32)
            s_scr[hd % 2, s0:s0 + kb, :] = s
            bm = jnp.max(s.reshape(kb // 8, 8, tq), axis=0)
            m8 = bm if m8 is None else jnp.maximum(m8, bm)
        return jnp.max(m8, axis=0, keepdims=True)

    def weighted_values(hd, blocks, m):
        kv = (hd // ATT_GROUP) * dh
        l8 = None
        ot = None
        for _, vt_ref, r0, s0 in blocks:
            p = jnp.exp2(s_scr[hd % 2, s0:s0 + kb, :] - m)
            ps = jnp.sum(p.reshape(kb // 8, 8, tq), axis=0)
            l8 = ps if l8 is None else l8 + ps
            o_blk = jnp.dot(vt_ref[kv:kv + dh, r0:r0 + kb], p.astype(BF16), preferred_element_type=F32)
            ot = o_blk if ot is None else ot + o_blk
        denom = jnp.sum(l8, axis=0, keepdims=True)
        o_ref[:, hd * dh:(hd + 1) * dh] = jnp.transpose(ot / denom).astype(BF16)

    def run(blocks):
        m = scores(0, blocks)
        for hd in range(ATT_HEADS):
            m_next = scores(hd + 1, blocks) if hd + 1 < ATT_HEADS else None
            weighted_values(hd, blocks, m)
            m = m_next

    t = pl.program_id(1)

    @pl.when(t < lat_q_tiles)
    def _():
        run(key_blocks(True))

    @pl.when(t >= lat_q_tiles)
    def _():
        run(key_blocks(False))


def _attention(tk, q, k, vt):
    tq = ATT_Q_TILE
    dh = ATT_HEAD_DIM
    nq, nk = ATT_HEADS * dh, ATT_KV_HEADS * dh
    n_lat, n_ctx = tk.n_lat, tk.n_ctx
    assert n_lat % tq == 0 and n_ctx % tq == 0
    lat_q_tiles = n_lat // tq
    q_tiles = lat_q_tiles + n_ctx // tq
    ctx_block0 = tk.lat_rows // n_ctx

    def q_row(i, t):
        return jnp.where(t < lat_q_tiles, i * lat_q_tiles + t,
                         tk.lat_rows // tq + i * (n_ctx // tq) + (t - lat_q_tiles))

    q_spec = pl.BlockSpec((tq, nq), lambda i, t: (q_row(i, t), 0))
    return pl.pallas_call(
        functools.partial(_attn_kernel, lat_q_tiles=lat_q_tiles),
        grid=(tk.bsz, q_tiles),
        in_specs=[q_spec,
                  pl.BlockSpec((n_lat, nk), lambda i, t: (i, 0)),
                  pl.BlockSpec((n_ctx, nk), lambda i, t: (ctx_block0 + i, 0)),
                  pl.BlockSpec((nk, n_lat), lambda i, t: (0, i)),
                  pl.BlockSpec((nk, n_ctx), lambda i, t: (0, ctx_block0 + i))],
        out_specs=q_spec,
        out_shape=jax.ShapeDtypeStruct(q.shape, BF16),
        scratch_shapes=[pltpu.VMEM((2, n_lat + n_ctx, tq), F32)],
        compiler_params=_params("parallel", "parallel"),
        name="att_core",
    )(q, k, k, vt, vt)


def _mlproj_kernel(h_ref, hp_ref, hn_ref, mod_ref, wqk_ref, wv_ref, wo_ref, wg_ref, gb_ref, cw_ref, cb_ref,
                   q_ref, kt_ref, v_ref, sig_ref, gc_ref, at_ref, xm_scr, p_scr, *, tk):
    tm = TOKEN_TILE
    lc = SCAN_CHUNK
    t = pl.program_id(0)
    xm = _modulated(h_ref[...], mod_ref, 3).astype(BF16)
    xm_scr[0:HALO, :] = _modulated(hp_ref[...], mod_ref, 3).astype(BF16)
    xm_scr[HALO:HALO + tm, :] = xm
    xm_scr[HALO + tm:tm + 2 * HALO, :] = _modulated(hn_ref[...], mod_ref, 3).astype(BF16)

    pos_in_seq = t % tk.tiles_per_seq
    is_ctx = jnp.logical_not(tk.is_lat(t))
    has_prev = jnp.logical_and(tk.is_lat(t), pos_in_seq != 0).astype(F32)
    has_next = jnp.logical_and(tk.is_lat(t), pos_in_seq != tk.tiles_per_seq - 1).astype(F32)
    row8 = lax.broadcasted_iota(jnp.int32, (8, FF_CHUNK), 0)
    inner_ends = range(tk.n_ctx, tm, tk.n_ctx)

    def patched(x, fixes):
        pieces, pos = [], 0
        for slab0, r, value, cond in sorted(fixes, key=lambda f: f[0]):
            mask = row8 == r if cond is None else jnp.logical_and(row8 == r, cond)
            pieces += [x[pos:slab0], jnp.where(mask, value, x[slab0:slab0 + 8])]
            pos = slab0 + 8
        return jnp.concatenate([p for p in pieces + [x[pos:]] if p.shape[0]], axis=0)

    hq = ML_HEADS * ML_QK
    n_rounds = 2 * hq // FF_CHUNK

    def round_matmuls(c):
        cols = slice(c * FF_CHUNK, (c + 1) * FF_CHUNK)
        return (jnp.dot(xm_scr[...], wqk_ref[:, cols], preferred_element_type=F32),
                jnp.dot(xm, wv_ref[:, cols], preferred_element_type=F32),
                jnp.dot(xm, wo_ref[:, cols], preferred_element_type=F32))

    pending = round_matmuls(0)
    for c in range(n_rounds):
        cols = slice(c * FF_CHUNK, (c + 1) * FF_CHUNK)
        p, v_c, o_c = pending
        if c + 1 < n_rounds:
            pending = round_matmuls(c + 1)
        p_scr[...] = p
        main = p[HALO:HALO + tm]
        prev_row = p[HALO - 1:HALO] * has_prev
        next_row = p[HALO + tm:HALO + tm + 1] * has_next
        down = patched(p_scr[HALO - 1:HALO - 1 + tm, :],
                       [(0, 0, prev_row, None)] + [(e, 0, 0.0, is_ctx) for e in inner_ends])
        up = patched(p_scr[HALO + 1:HALO + 1 + tm, :],
                     [(tm - 8, 7, next_row, None)] + [(e - 8, 7, 0.0, is_ctx) for e in inner_ends])
        conv = down * cw_ref[0:1, cols] + main * cw_ref[1:2, cols] + up * cw_ref[2:3, cols] + cb_ref[:, cols]
        act = _silu(conv)
        if (c + 1) * FF_CHUNK <= hq:
            q_ref[:, cols] = act.astype(BF16)
        else:
            kact = act * (ML_QK ** -0.5)
            for rc in range(tm // lc):
                kt_ref[rc, c * FF_CHUNK - hq:(c + 1) * FF_CHUNK - hq, :] = (
                    jnp.transpose(kact[rc * lc:(rc + 1) * lc, :]).astype(BF16))
        v_ref[:, cols] = v_c.astype(BF16)
        sig_ref[:, cols] = _sigmoid(o_c).astype(BF16)

    g = jnp.dot(xm, wg_ref[...], preferred_element_type=F32) + gb_ref[...]
    logf = jax.nn.log_sigmoid(g)
    r_i = lax.broadcasted_iota(jnp.int32, (lc, lc), 0)
    c_i = lax.broadcasted_iota(jnp.int32, (lc, lc), 1)
    tri_lo = (c_i <= r_i).astype(BF16)
    lane = lax.broadcasted_iota(jnp.int32, (lc, 128), 1)
    trow = lax.broadcasted_iota(jnp.int32, (lc, 128), 0)
    nh = ML_HEADS
    for c in range(tm // lc):
        rows = slice(c * lc, (c + 1) * lc)
        lf = logf[rows]
        hi = lf.astype(BF16)
        r1 = lf - hi.astype(F32)
        mid = r1.astype(BF16)
        lo = (r1 - mid.astype(F32)).astype(BF16)
        pre3 = jnp.dot(tri_lo, jnp.concatenate([hi, mid, lo], axis=1), preferred_element_type=F32)
        pre = pre3[:, 0:128] + pre3[:, 128:256] + pre3[:, 256:384]
        suf = pre[lc - 1:lc, :] - pre + lf
        bc = pltpu.roll(jnp.where(lane < 3 * nh, pre, suf), 128 - 2 * nh, 1)
        a = g[rows] - bc
        pf = a
        pb = a
        k = 1
        while k < lc:
            pf = jnp.maximum(pf, jnp.where(trow >= k, pltpu.roll(pf, k, 0), -jnp.inf))
            pb = jnp.maximum(pb, jnp.where(trow < lc - k, pltpu.roll(pb, lc - k, 0), -jnp.inf))
            k *= 2
        pm = jnp.where(lane < nh, pf, pb)
        gc_ref[rows, :] = jnp.where(lane < 2 * nh, bc,
                                    jnp.where(lane < 4 * nh, pltpu.roll(a, 2 * nh, 1), pltpu.roll(pm, 4 * nh, 1)))
        at_ref[c] = jnp.transpose(a)[0:ML_UNITS, :]


def _mlproj(tk, h, mods, wqk, wv, wo, wg, gb, cw, cb):
    d = h.shape[1]
    tm = TOKEN_TILE
    lc = SCAN_CHUNK
    hq = ML_HEADS * ML_QK
    blocks_per_tile = tm // HALO
    last_block = tk.rows // HALO - 1
    return pl.pallas_call(
        functools.partial(_mlproj_kernel, tk=tk),
        grid=(tk.tiles,),
        in_specs=[
            _tok_spec(d),
            pl.BlockSpec((HALO, d), lambda t: (jnp.maximum(t * blocks_per_tile - 1, 0), 0)),
            pl.BlockSpec((HALO, d), lambda t: (jnp.minimum((t + 1) * blocks_per_tile, last_block), 0)),
            tk.mod_spec(d),
            _resident(wqk.shape), _resident(wv.shape), _resident(wo.shape), _resident(wg.shape),
            _resident(gb.shape), _resident(cw.shape), _resident(cb.shape),
        ],
        out_specs=[
            _tok_spec(hq), pl.BlockSpec((tm // lc, hq, lc), lambda t: (t, 0, 0)),
            _tok_spec(d), _tok_spec(d), _tok_spec(128),
            pl.BlockSpec((tm // lc, ML_UNITS, lc), lambda t: (t, 0, 0)),
        ],
        out_shape=[
            jax.ShapeDtypeStruct((tk.rows, hq), BF16),
            jax.ShapeDtypeStruct((tk.rows // lc, hq, lc), BF16),
            jax.ShapeDtypeStruct((tk.rows, d), BF16),
            jax.ShapeDtypeStruct((tk.rows, d), BF16),
            jax.ShapeDtypeStruct((tk.rows, 128), F32),
            jax.ShapeDtypeStruct((tk.rows // lc, ML_UNITS, lc), F32),
        ],
        scratch_shapes=[pltpu.VMEM((tm + 2 * HALO, d), BF16), pltpu.VMEM((tm + 2 * HALO, FF_CHUNK), F32)],
        compiler_params=_params("parallel"),
        name="ml_proj",
    )(h, h, h, mods, wqk, wv, wo, wg, gb, cw, cb)


def _scan_kernel(ql_ref, qc_ref, ktl_ref, ktc_ref, vl_ref, vc_ref, gl_ref, gcx_ref, al_ref, ac_ref, o_ref,
                 ct_scr, m_scr, *, n_lat_chunks, n_ctx_chunks):
    lc = SCAN_CHUNK
    dk, dv = ML_QK, ML_V
    assert lc == 128 and dk == 128 and dv == 2 * 128
    ct_scr[...] = jnp.zeros_like(ct_scr)
    m_scr[...] = jnp.zeros_like(m_scr)
    t_i = lax.broadcasted_iota(jnp.int32, (lc, lc), 0)
    s_i = lax.broadcasted_iota(jnp.int32, (lc, lc), 1)
    masks = (s_i <= t_i, s_i >= t_i)
    ones_blk = jnp.ones((lc, 128), BF16)

    def step(j, mode):
        if mode == "ctx":
            q_ref, kt_ref, v_ref, g_ref, a_ref, n_seq = qc_ref, ktc_ref, vc_ref, gcx_ref, ac_ref, n_ctx_chunks
        else:
            q_ref, kt_ref, v_ref, g_ref, a_ref, n_seq = ql_ref, ktl_ref, vl_ref, gl_ref, al_ref, n_lat_chunks
        chunk = (j, n_seq - 1 - j)
        units = []
        for direction in range(2):
            c = chunk[direction]
            r0 = pl.multiple_of(c * lc, lc)
            gc = g_ref[pl.ds(r0, lc), :]
            at = a_ref[c]
            for hd in range(ML_HEADS):
                units.append((direction, hd, direction * ML_HEADS + hd, c, r0, gc, at))

        st = {}
        for direction, hd, u, c, r0, gc, at in units:
            last = lc - 1 if direction == 0 else 0
            kt = kt_ref[c, hd * dk:(hd + 1) * dk, :]
            v_aug = jnp.concatenate([v_ref[pl.ds(r0, lc), hd * dv:(hd + 1) * dv], ones_blk], axis=1)
            bcum = jnp.broadcast_to(gc[:, u:u + 1], (lc, 128))
            pmax = jnp.broadcast_to(gc[:, 2 * ML_UNITS + u:2 * ML_UNITS + u + 1], (lc, 128))
            a_row = at[u:u + 1, :]
            m = m_scr[u, 0:1, :]
            mu = jnp.maximum(m, pmax)
            mu_l = mu[last:last + 1, :]
            st[u] = dict(kt=kt, v_aug=v_aug, bcum=bcum, a_row=a_row, m=m, mu=mu, mu_l=mu_l, ct=ct_scr[u],
                         b_last=bcum[last:last + 1, :])
        if mode != "ctx":
            for direction, hd, u, c, r0, gc, at in units:
                x = st[u]
                x["qc"] = q_ref[pl.ds(r0, lc), hd * dk:(hd + 1) * dk]
                x["s"] = jnp.dot(x["qc"], x["kt"], preferred_element_type=F32)
            for direction, hd, u, c, r0, gc, at in units:
                x = st[u]
                w = jnp.where(masks[direction], jnp.exp(x["a_row"] - x["mu"]), 0.0)
                eq = jnp.exp(x["m"] - x["mu"]) * x["qc"].astype(F32)
                lhs = jnp.concatenate([(x["s"] * w).astype(BF16), eq.astype(BF16)], axis=1)
                rhs = jnp.concatenate([x["v_aug"], x["ct"].astype(BF16)], axis=0)
                x["num"] = jnp.dot(lhs, rhs, preferred_element_type=F32)
        for direction, hd, u, c, r0, gc, at in units:
            x = st[u]
            ktw = (x["kt"].astype(F32) * jnp.exp(x["a_row"] - x["mu_l"])).astype(BF16)
            x["upd"] = jnp.dot(ktw, x["v_aug"], preferred_element_type=F32)
        if mode != "ctx":
            for direction, hd, u, c, r0, gc, at in units:
                x = st[u]
                num = x["num"]
                inv = 1.0 / jnp.maximum(jnp.abs(num[:, dv:]), jnp.exp(-(x["bcum"] + x["mu"])))
                hout = num[:, :dv] * jnp.concatenate([inv, inv], axis=1)
                if mode == "store":
                    o_ref[pl.ds(r0, lc), hd * dv:(hd + 1) * dv] = hout
                else:
                    o_ref[pl.ds(r0, lc), hd * dv:(hd + 1) * dv] += hout
        for direction, hd, u, c, r0, gc, at in units:
            x = st[u]
            decay = jnp.exp(x["m"] - x["mu_l"])
            ct_scr[u] = x["ct"] * jnp.concatenate([decay, decay, decay], axis=1) + x["upd"]
            m_scr[u, 0:1, :] = x["b_last"] + x["mu_l"]

    half = n_lat_chunks // 2
    lax.fori_loop(0, n_ctx_chunks, lambda j, _: step(j, "ctx"), None)
    lax.fori_loop(0, half, lambda j, _: step(j, "store"), None)
    lax.fori_loop(half, n_lat_chunks, lambda j, _: step(j, "add"), None)


def _scan(tk, q, kt, v, gc, at):
    d = v.shape[1]
    lc = SCAN_CHUNK
    hq = ML_HEADS * ML_QK
    n_lat, n_ctx = tk.n_lat, tk.n_ctx
    n_lat_chunks, n_ctx_chunks = n_lat // lc, n_ctx // lc
    assert n_lat_chunks % 2 == 0
    ctx0 = tk.lat_rows // n_ctx
    ctx_chunk0 = tk.lat_rows // lc // n_ctx_chunks
    lat = lambda width: pl.BlockSpec((n_lat, width), lambda i: (i, 0))
    ctx = lambda width: pl.BlockSpec((n_ctx, width), lambda i: (ctx0 + i, 0))
    lat_chunks = lambda rows: pl.BlockSpec((n_lat_chunks, rows, lc), lambda i: (i, 0, 0))
    ctx_chunks = lambda rows: pl.BlockSpec((n_ctx_chunks, rows, lc), lambda i: (ctx_chunk0 + i, 0, 0))
    kern = functools.partial(_scan_kernel, n_lat_chunks=n_lat_chunks, n_ctx_chunks=n_ctx_chunks)
    return pl.pallas_call(
        kern,
        grid=(tk.bsz,),
        in_specs=[
            lat(hq), ctx(hq), lat_chunks(hq), ctx_chunks(hq), lat(d), ctx(d), lat(128), ctx(128),
            lat_chunks(ML_UNITS), ctx_chunks(ML_UNITS),
        ],
        out_specs=lat(d),
        out_shape=jax.ShapeDtypeStruct((tk.lat_rows, d), F32),
        scratch_shapes=[
            pltpu.VMEM((ML_UNITS, ML_QK, ML_V + 128), F32),
            pltpu.VMEM((ML_UNITS, 8, 128), F32),
        ],
        compiler_params=_params("parallel"),
        name="ml_scan",
    )(q, q, kt, kt, v, v, gc, gc, at, at)


def _rope_tables(n_lat, tile):
    rows = n_lat // GRID_W
    row = jnp.repeat(jnp.arange(rows, dtype=jnp.int32), GRID_W).astype(F32)
    col = jnp.tile(jnp.arange(GRID_W, dtype=jnp.int32), rows).astype(F32)
    inv = ROPE_THETA ** (-jnp.arange(ROPE_PAIRS, dtype=F32) / ROPE_PAIRS)
    ar, ac = row[:, None] * inv, col[:, None] * inv
    cos = jnp.concatenate([jnp.cos(ar), jnp.cos(ac), jnp.cos(ar), jnp.cos(ac)], axis=1)
    sin = jnp.concatenate([-jnp.sin(ar), -jnp.sin(ac), jnp.sin(ar), jnp.sin(ac)], axis=1)
    cos = jnp.concatenate([cos, jnp.ones((tile, ATT_HEAD_DIM), F32)], axis=0)
    sin = jnp.concatenate([sin, jnp.zeros((tile, ATT_HEAD_DIM), F32)], axis=0)
    return cos, sin


def _rope_column_order():
    p = ROPE_PAIRS
    return jnp.concatenate([jnp.arange(0, p), jnp.arange(2 * p, 3 * p), jnp.arange(p, 2 * p), jnp.arange(3 * p, 4 * p)])


def kernel(x, c, ctx, c_ctx, ada_w, ada_b, ln_g, ln_b, ffn_w_in, ffn_w_out, att_w_in, att_q_gain, att_k_gain,
           att_w_out, ml_w_in, ml_gate_b, ml_conv_w, ml_conv_b, ml_norm_g, ml_w_out):
    bsz, n_lat, d = x.shape
    n_ctx = ctx.shape[1]
    depth = ada_w.shape[0]
    dff = ffn_w_out.shape[2]
    assert depth == 2, "layer 0 attention, layer 1 (last) mLSTM"
    assert dff % FF_CHUNK == 0 and TOKEN_TILE % SCAN_CHUNK == 0
    tk = _Tokens(bsz, n_lat, n_ctx)
    alpha = (2.0 * depth) ** 0.25

    mod_rows = -(-(bsz + 1) // 8) * 8
    cc = jnp.concatenate([c, c_ctx[None, :], jnp.zeros((mod_rows - bsz - 1, d), F32)], axis=0)
    mods = _ada(cc, ada_w, ada_b).reshape(depth, mod_rows, N_MOD, d)

    win = ffn_w_in.astype(BF16)
    wout = ffn_w_out.astype(BF16)
    row2 = lambda a: a.reshape(1, -1)
    tk_wide = _Tokens(bsz, n_lat, n_ctx, WIDE_TOKEN_TILE)
    cos, sin = _rope_tables(n_lat, WIDE_TOKEN_TILE)

    m_i = mods[0]
    h = _ffn(tk, (x.reshape(bsz * n_lat, d), ctx.reshape(bsz * n_ctx, d)), m_i, tk.tiles, win, wout, (0, 0),
             row2(ln_g[0, 0]), row2(ln_b[0, 0]), 0, alpha)
    dh = ATT_HEAD_DIM
    nqk = (ATT_HEADS + ATT_KV_HEADS) * dh
    order = _rope_column_order()
    wqk = att_w_in[0][:, :nqk].reshape(d, ATT_HEADS + ATT_KV_HEADS, dh)[:, :, order].reshape(d, nqk).astype(BF16)
    wvt = att_w_in[0][:, nqk:].T.astype(BF16)
    q, k, vt = _qkv(tk_wide, h, m_i, wqk, wvt, row2(att_q_gain[0][order]), row2(att_k_gain[0][order]), cos, sin)
    o = _attention(tk, q, k, vt)
    h = _ffn(tk, h, m_i, tk.tiles, win, wout, (0, 1), row2(ln_g[0, 2]), row2(ln_b[0, 2]), 6, alpha,
             mixer=(o, att_w_out[0].astype(BF16), row2(ln_g[0, 1]), row2(ln_b[0, 1])))

    m_i = mods[1]
    h = _ffn(tk_wide, h, m_i, tk_wide.tiles, win, wout, (1, 0), row2(ln_g[1, 0]), row2(ln_b[1, 0]), 0, alpha)
    hq = ML_HEADS * ML_QK
    w = ml_w_in[0]
    wqk = w[:, :2 * hq].astype(BF16)
    wv = w[:, 2 * hq:2 * hq + d].astype(BF16)
    wo = w[:, 2 * hq + d:2 * hq + 2 * d].astype(BF16)
    perm = jnp.array([0, 2, 1, 3])
    wg = w[:, 2 * hq + 2 * d:].reshape(d, 4, ML_HEADS)[:, perm].reshape(d, 4 * ML_HEADS)
    wg = jnp.pad(wg, ((0, 0), (0, 128 - 4 * ML_HEADS))).astype(BF16)
    gb = ml_gate_b[0].reshape(4, ML_HEADS)[perm].reshape(1, 4 * ML_HEADS)
    gb = jnp.pad(gb, ((0, 0), (0, 128 - 4 * ML_HEADS)))
    q, kt, v, sig, gc, at = _mlproj(tk, h, m_i, wqk, wv, wo, wg, gb, ml_conv_w[0], row2(ml_conv_b[0]))
    hs = _scan(tk, q, kt, v, gc, at)
    h = _ffn(tk, h, m_i, tk.lat_tiles, win, wout, (1, 1), row2(ln_g[1, 2]), row2(ln_b[1, 2]), 6, alpha,
             mixer=(hs, sig, row2(ml_norm_g[0]), ml_w_out[0].astype(BF16), row2(ln_g[1, 1]), row2(ln_b[1, 1])))
    return h.reshape(bsz, n_lat, d)
```

```python
import functools

import jax
import jax.numpy as jnp
from jax import lax
from jax.experimental import pallas as pl
from jax.experimental.pallas import tpu as pltpu

F32 = jnp.float32
BF16 = jnp.bfloat16

N_MOD = 9
GRID_W = 64
ATT_HEADS = 8
ATT_KV_HEADS = 2
ATT_GROUP = ATT_HEADS // ATT_KV_HEADS
ATT_HEAD_DIM = 128
ROPE_PAIRS = ATT_HEAD_DIM // 4
ROPE_THETA = 10000.0
ML_HEADS = 4
ML_QK = 128
ML_V = 256
ML_UNITS = 2 * ML_HEADS
LN_EPS = 1e-5
RMS_EPS = 1e-6
LOG2_E = 1.4426950408889634

TOKEN_TILE = 1024
WIDE_TOKEN_TILE = 2048
SUB_TILE = 512
ATT_Q_TILE = 256
ATT_KEY_BLOCK = 256
FF_CHUNK = 256
FFN_HIDDEN_CHUNK = 256
FFN_PIN_ROWS = 128
SCAN_CHUNK = 128
HALO = 16
VMEM_LIMIT = 56 * 1024 * 1024


def _params(*sem):
    return pltpu.CompilerParams(dimension_semantics=sem, vmem_limit_bytes=VMEM_LIMIT)


def _resident(shape, lead=()):
    n = len(shape)
    block = (None,) * len(lead) + tuple(shape)
    return pl.BlockSpec(block, lambda *_: tuple(lead) + (0,) * n, pipeline_mode=pl.Buffered(1))


def _post_norm(h, gate, y, g, b, alpha):
    x = h + (gate * (1.0 / alpha)) * y
    mu = jnp.mean(x, axis=-1, keepdims=True)
    xc = x - mu
    var = jnp.mean(xc * xc, axis=-1, keepdims=True)
    return xc * lax.rsqrt(var + LN_EPS / (alpha * alpha)) * g + b


def _sigmoid(x):
    return 0.5 * jnp.tanh(0.5 * x) + 0.5


def _silu(x):
    return x * _sigmoid(x)


def _modulated(h, mod_ref, k0):
    return h * (1.0 + mod_ref[k0 + 1:k0 + 2, :]) + mod_ref[k0:k0 + 1, :]


class _Tokens:
    def __init__(self, bsz, n_lat, n_ctx, tile=TOKEN_TILE):
        tm = tile
        assert n_lat % tm == 0 and (bsz * n_ctx) % tm == 0 and tm % n_ctx == 0
        self.tile = tile
        self.bsz, self.n_lat, self.n_ctx = bsz, n_lat, n_ctx
        self.lat_rows = bsz * n_lat
        self.rows = bsz * (n_lat + n_ctx)
        self.tiles_per_seq = n_lat // tm
        self.lat_tiles = self.lat_rows // tm
        self.tiles = self.rows // tm

    def is_lat(self, t):
        return t < self.lat_tiles

    def mod_row(self, t):
        return jnp.where(t < self.lat_tiles, t // self.tiles_per_seq, self.bsz)

    def mod_spec(self, d):
        return pl.BlockSpec((None, N_MOD, d), lambda t: (self.mod_row(t), 0, 0))

    def tok(self, width):
        return pl.BlockSpec((self.tile, width), lambda t: (t, 0))


def _tok_spec(width):
    return pl.BlockSpec((TOKEN_TILE, width), lambda t: (t, 0))


def _ada_kernel(c_ref, w_ref, b_ref, o_ref):
    s = _silu(c_ref[...]).astype(BF16)
    o_ref[...] = jnp.dot(s, w_ref[...].astype(BF16), preferred_element_type=F32) + b_ref[...]


def _ada(cc, ada_w, ada_b):
    depth, d, nd = ada_w.shape
    rows = cc.shape[0]
    return pl.pallas_call(
        _ada_kernel,
        grid=(depth, nd // d),
        in_specs=[
            pl.BlockSpec((rows, d), lambda i, j: (0, 0)),
            pl.BlockSpec((None, d, d), lambda i, j: (i, 0, j)),
            pl.BlockSpec((None, 1, d), lambda i, j: (i, 0, j)),
        ],
        out_specs=pl.BlockSpec((None, rows, d), lambda i, j: (i, 0, j)),
        out_shape=jax.ShapeDtypeStruct((depth, rows, nd), F32),
        compiler_params=_params("parallel", "parallel"),
        name="ada_mod",
    )(cc, ada_w, ada_b.reshape(depth, 1, nd))


def _ffn_kernel(*refs, k0, alpha, dff, lat_tiles, source):
    if source == "split":
        hx_ref, hc_ref, mod_ref, win_ref, wout_ref, g_ref, b_ref, o_ref = refs
        is_lat = pl.program_id(0) < lat_tiles
        load = lambda rows: jnp.where(is_lat, hx_ref[rows, :], hc_ref[rows, :])
    elif source == "mixer":
        z_ref, h_ref, mod_ref, wp_ref, gp_ref, bp_ref, win_ref, wout_ref, g_ref, b_ref, o_ref = refs

        def load(rows):
            y = jnp.dot(z_ref[rows, :], wp_ref[...], preferred_element_type=F32)
            return _post_norm(h_ref[rows, :], mod_ref[5:6, :], y, gp_ref[...], bp_ref[...], alpha)
    elif source == "mlstm":
        hs_ref, sig_ref, h_ref, mod_ref, ng_ref, wp_ref, gp_ref, bp_ref, win_ref, wout_ref, g_ref, b_ref, o_ref = refs

        def load(rows):
            parts = []
            for hd in range(ML_HEADS):
                x = hs_ref[rows, hd * ML_V:(hd + 1) * ML_V]
                mu = jnp.mean(x, axis=-1, keepdims=True)
                xc = x - mu
                var = jnp.mean(xc * xc, axis=-1, keepdims=True)
                parts.append(xc * lax.rsqrt(var + LN_EPS))
            hn = jnp.concatenate(parts, axis=1) * ng_ref[...]
            z = (hn * sig_ref[rows, :].astype(F32)).astype(BF16)
            y = jnp.dot(z, wp_ref[...], preferred_element_type=F32)
            return _post_norm(h_ref[rows, :], mod_ref[5:6, :], y, gp_ref[...], bp_ref[...], alpha)
    else:
        h_ref, mod_ref, win_ref, wout_ref, g_ref, b_ref, o_ref = refs
        load = lambda rows: h_ref[rows, :]
    n_sub = o_ref.shape[0] // SUB_TILE
    sub = lambda r: slice(r * SUB_TILE, (r + 1) * SUB_TILE)

    chunks = [slice(c0, min(c0 + FFN_HIDDEN_CHUNK, dff)) for c0 in range(0, dff, FFN_HIDDEN_CHUNK)]
    gate = mod_ref[k0 + 2:k0 + 3, :]

    pieces = [slice(q0, q0 + FFN_PIN_ROWS) for q0 in range(0, SUB_TILE, FFN_PIN_ROWS)]

    def ffn_chunks(xm, acc, part, anchors=()):
        for n, cols in enumerate(part):
            a = jnp.dot(xm, win_ref[:, cols], preferred_element_type=F32)
            if n < len(anchors):
                a = a + anchors[n]
            u = jnp.dot(xm, win_ref[:, dff + cols.start:dff + cols.stop], preferred_element_type=F32)
            hid = (_silu(a) * u).astype(BF16)
            y = jnp.dot(hid, wout_ref[cols, :], preferred_element_type=F32)
            acc = y if acc is None else acc + y
        return acc

    def post_ln(r, h, acc):
        outs = []
        for rows in pieces:
            out = _post_norm(h[rows], 0.5 * gate, acc[rows], g_ref[...], b_ref[...], alpha)
            o_ref[r * SUB_TILE + rows.start:r * SUB_TILE + rows.stop, :] = out
            outs.append(out)
        return outs

    def zero_after(x):
        tile_rows = 8 * 4 // x.dtype.itemsize
        red = jnp.max(x.reshape(x.shape[0] // tile_rows, tile_rows, x.shape[1]), axis=0).astype(F32)
        red = functools.reduce(jnp.maximum, [red[:, l:l + 128] for l in range(0, x.shape[1], 128)])
        bits = pltpu.bitcast(red[0:8, :], jnp.uint32)
        zero = ((bits >> 16) >> 16).astype(F32)[0:1, :]
        return jnp.concatenate([zero] * (FFN_HIDDEN_CHUNK // 128), axis=1)

    n_pin = len(pieces)
    tail = len(chunks) - n_pin
    head = 2
    assert head + n_pin <= tail
    h = load(sub(0))
    xm = _modulated(h, mod_ref, k0).astype(BF16)
    for r in range(n_sub):
        last = r + 1 == n_sub
        acc = ffn_chunks(xm, None, chunks[:tail]) if r == 0 else acc
        if not last:
            h_next = load(sub(r + 1))
            xm_parts = [_modulated(h_next[rows], mod_ref, k0).astype(BF16) for rows in pieces]
            xm_next = jnp.concatenate(xm_parts, axis=0)
            acc = ffn_chunks(xm, acc, chunks[tail:], [zero_after(p) for p in xm_parts])
            acc_next = ffn_chunks(xm_next, None, chunks[:head])
            ln_anchors = [zero_after(p) for p in post_ln(r, h, acc)]
            acc = ffn_chunks(xm_next, acc_next, chunks[head:tail], ln_anchors)
            h, xm = h_next, xm_next
        else:
            acc = ffn_chunks(xm, acc, chunks[tail:])
            post_ln(r, h, acc)


def _ffn(tk, hs, mods, n_tiles, win, wout, which, g, b, k0, alpha, mixer=None):
    dff, d = wout.shape[-2:]
    tm = tk.tile
    _tok_spec = tk.tok
    extra_specs, extra = [], []
    if isinstance(hs, tuple):
        source = "split"
        last_lat = tk.lat_tiles - 1
        h_specs = [pl.BlockSpec((tm, d), lambda t: (jnp.minimum(t, last_lat), 0)),
                   pl.BlockSpec((tm, d), lambda t: (jnp.maximum(t - tk.lat_tiles, 0), 0))]
        hs = list(hs)
    elif mixer is not None and len(mixer) == 4:
        source = "mixer"
        z, wp, gp, bp = mixer
        h_specs = [_tok_spec(z.shape[1]), _tok_spec(d)]
        hs = [z, hs]
        extra_specs = [_resident(wp.shape), _resident((1, d)), _resident((1, d))]
        extra = [wp, gp, bp]
    elif mixer is not None:
        source = "mlstm"
        scan_out, sig, ng, wp, gp, bp = mixer
        h_specs = [_tok_spec(d), _tok_spec(d), _tok_spec(d)]
        hs = [scan_out, sig, hs]
        extra_specs = [_resident((1, d)), _resident(wp.shape), _resident((1, d)), _resident((1, d))]
        extra = [ng, wp, gp, bp]
    else:
        source = "plain"
        h_specs = [_tok_spec(d)]
        hs = [hs]
    kern = functools.partial(_ffn_kernel, k0=k0, alpha=alpha, dff=dff, lat_tiles=tk.lat_tiles, source=source)
    return pl.pallas_call(
        kern,
        grid=(n_tiles,),
        in_specs=h_specs + [tk.mod_spec(d)] + extra_specs + [
            _resident(win.shape[-2:], which), _resident(wout.shape[-2:], which), _resident((1, d)), _resident((1, d))],
        out_specs=_tok_spec(d),
        out_shape=jax.ShapeDtypeStruct((n_tiles * tm, d), F32),
        compiler_params=_params("parallel"),
        name="half_ffn",
    )(*hs, mods, *extra, win, wout, g, b)


def _qkv_kernel(h_ref, mod_ref, wqk_ref, wvt_ref, qg_ref, kg_ref, ct_ref, st_ref, q_ref, k_ref, vt_ref):
    xm = _modulated(h_ref[...], mod_ref, 3).astype(BF16)
    ct = ct_ref[...]
    st = st_ref[...]
    dh = ATT_HEAD_DIM

    ones_w = jnp.ones((2 * dh, dh), BF16)

    def norm_rope(x, gain):
        sq = x * x
        hi = sq.astype(BF16)
        lo = (sq - hi.astype(F32)).astype(BF16)
        ssq = jnp.dot(jnp.concatenate([hi, lo], axis=1), ones_w, preferred_element_type=F32)
        xn = x * lax.rsqrt(ssq * (1.0 / dh) + RMS_EPS) * gain
        return xn * ct + pltpu.roll(xn, dh // 2, 1) * st

    qg = qg_ref[...] * (dh ** -0.5 * LOG2_E)
    kg = kg_ref[...]
    n_pairs = (ATT_HEADS + ATT_KV_HEADS) // 2
    pair_dot = lambda i: jnp.dot(xm, wqk_ref[:, 2 * i * dh:2 * (i + 1) * dh], preferred_element_type=F32)
    pending = pair_dot(0)
    for pair in range(n_pairs):
        p = pending
        if pair + 1 < n_pairs:
            pending = pair_dot(pair + 1)
        for half in range(2):
            hd = 2 * pair + half
            x = p[:, half * dh:(half + 1) * dh]
            if hd < ATT_HEADS:
                q_ref[:, hd * dh:(hd + 1) * dh] = norm_rope(x, qg).astype(BF16)
            else:
                hk = hd - ATT_HEADS
                k_ref[:, hk * dh:(hk + 1) * dh] = norm_rope(x, kg).astype(BF16)
    vt = lax.dot_general(wvt_ref[...], xm, (((1,), (1,)), ((), ())), preferred_element_type=F32)
    vt_ref[...] = vt.astype(BF16)


def _qkv(tk, h, mods, wqk, wvt, qg, kg, ctab, stab):
    d = h.shape[1]
    tm = tk.tile
    _tok_spec = tk.tok
    dh = ATT_HEAD_DIM
    nq, nk = ATT_HEADS * dh, ATT_KV_HEADS * dh
    tab_spec = pl.BlockSpec((tm, dh), lambda t: (jnp.where(tk.is_lat(t), t % tk.tiles_per_seq, tk.tiles_per_seq), 0))
    return pl.pallas_call(
        _qkv_kernel,
        grid=(tk.tiles,),
        in_specs=[_tok_spec(d), tk.mod_spec(d), _resident(wqk.shape), _resident(wvt.shape),
                  _resident((1, dh)), _resident((1, dh)), tab_spec, tab_spec],
        out_specs=[_tok_spec(nq), _tok_spec(nk), pl.BlockSpec((nk, tm), lambda t: (0, t))],
        out_shape=[jax.ShapeDtypeStruct((tk.rows, nq), BF16),
                   jax.ShapeDtypeStruct((tk.rows, nk), BF16),
                   jax.ShapeDtypeStruct((nk, tk.rows), BF16)],
        compiler_params=_params("parallel"),
        name="att_qkv",
    )(h, mods, wqk, wvt, qg, kg, ctab, stab)


def _attn_kernel(q_ref, kl_ref, kc_ref, vlt_ref, vct_ref, o_ref, s_scr, *, lat_q_tiles):
    dh = ATT_HEAD_DIM
    kb = ATT_KEY_BLOCK
    tq = q_ref.shape[0]
    n_lat, n_ctx = kl_ref.shape[0], kc_ref.shape[0]
    nt = (((1,), (1,)), ((), ()))

    def key_blocks(with_lat):
        blocks = []
        if with_lat:
            blocks += [(kl_ref, vlt_ref, b * kb, b * kb) for b in range(n_lat // kb)]
        base = n_lat if with_lat else 0
        blocks += [(kc_ref, vct_ref, b * kb, base + b * kb) for b in range(n_ctx // kb)]
        return blocks

    def scores(hd, blocks):
        kv = (hd // ATT_GROUP) * dh
        q = q_ref[:, hd * dh:(hd + 1) * dh]
        m8 = None
        for k_ref, _, r0, s0 in blocks:
            s = lax.dot_general(k_ref[r0:r0 + kb, kv:kv + dh], q, nt, preferred_element_type=F32)
            s_scr[hd % 2, s0:s0 + kb, :] = s
            bm = jnp.max(s.reshape(kb // 8, 8, tq), axis=0)
            m8 = bm if m8 is None else jnp.maximum(m8, bm)
        return jnp.max(m8, axis=0, keepdims=True)

    def weighted_values(hd, blocks, m):
        kv = (hd // ATT_GROUP) * dh
        l8 = None
        ot = None
        for _, vt_ref, r0, s0 in blocks:
            p = jnp.exp2(s_scr[hd % 2, s0:s0 + kb, :] - m)
            ps = jnp.sum(p.reshape(kb // 8, 8, tq), axis=0)
            l8 = ps if l8 is None else l8 + ps
            o_blk = jnp.dot(vt_ref[kv:kv + dh, r0:r0 + kb], p.astype(BF16), preferred_element_type=F32)
            ot = o_blk if ot is None else ot + o_blk
        denom = jnp.sum(l8, axis=0, keepdims=True)
        o_ref[:, hd * dh:(hd + 1) * dh] = jnp.transpose(ot / denom).astype(BF16)

    def run(blocks):
        m = scores(0, blocks)
        for hd in range(ATT_HEADS):
            m_next = scores(hd + 1, blocks) if hd + 1 < ATT_HEADS else None
            weighted_values(hd, blocks, m)
            m = m_next

    t = pl.program_id(1)

    @pl.when(t < lat_q_tiles)
    def _():
        run(key_blocks(True))

    @pl.when(t >= lat_q_tiles)
    def _():
        run(key_blocks(False))


def _attention(tk, q, k, vt):
    tq = ATT_Q_TILE
    dh = ATT_HEAD_DIM
    nq, nk = ATT_HEADS * dh, ATT_KV_HEADS * dh
    n_lat, n_ctx = tk.n_lat, tk.n_ctx
    assert n_lat % tq == 0 and n_ctx % tq == 0
    lat_q_tiles = n_lat // tq
    q_tiles = lat_q_tiles + n_ctx // tq
    ctx_block0 = tk.lat_rows // n_ctx

    def q_row(i, t):
        return jnp.where(t < lat_q_tiles, i * lat_q_tiles + t,
                         tk.lat_rows // tq + i * (n_ctx // tq) + (t - lat_q_tiles))

    q_spec = pl.BlockSpec((tq, nq), lambda i, t: (q_row(i, t), 0))
    return pl.pallas_call(
        functools.partial(_attn_kernel, lat_q_tiles=lat_q_tiles),
        grid=(tk.bsz, q_tiles),
        in_specs=[q_spec,
                  pl.BlockSpec((n_lat, nk), lambda i, t: (i, 0)),
                  pl.BlockSpec((n_ctx, nk), lambda i, t: (ctx_block0 + i, 0)),
                  pl.BlockSpec((nk, n_lat), lambda i, t: (0, i)),
                  pl.BlockSpec((nk, n_ctx), lambda i, t: (0, ctx_block0 + i))],
        out_specs=q_spec,
        out_shape=jax.ShapeDtypeStruct(q.shape, BF16),
        scratch_shapes=[pltpu.VMEM((2, n_lat + n_ctx, tq), F32)],
        compiler_params=_params("parallel", "parallel"),
        name="att_core",
    )(q, k, k, vt, vt)


def _mlproj_kernel(h_ref, hp_ref, hn_ref, mod_ref, wqk_ref, wv_ref, wo_ref, wg_ref, gb_ref, cw_ref, cb_ref,
                   q_ref, kt_ref, v_ref, sig_ref, gc_ref, at_ref, xm_scr, p_scr, *, tk):
    tm = TOKEN_TILE
    lc = SCAN_CHUNK
    t = pl.program_id(0)
    xm = _modulated(h_ref[...], mod_ref, 3).astype(BF16)
    xm_scr[0:HALO, :] = _modulated(hp_ref[...], mod_ref, 3).astype(BF16)
    xm_scr[HALO:HALO + tm, :] = xm
    xm_scr[HALO + tm:tm + 2 * HALO, :] = _modulated(hn_ref[...], mod_ref, 3).astype(BF16)

    pos_in_seq = t % tk.tiles_per_seq
    is_ctx = jnp.logical_not(tk.is_lat(t))
    has_prev = jnp.logical_and(tk.is_lat(t), pos_in_seq != 0).astype(F32)
    has_next = jnp.logical_and(tk.is_lat(t), pos_in_seq != tk.tiles_per_seq - 1).astype(F32)
    row8 = lax.broadcasted_iota(jnp.int32, (8, FF_CHUNK), 0)
    inner_ends = range(tk.n_ctx, tm, tk.n_ctx)

    def patched(x, fixes):
        pieces, pos = [], 0
        for slab0, r, value, cond in sorted(fixes, key=lambda f: f[0]):
            mask = row8 == r if cond is None else jnp.logical_and(row8 == r, cond)
            pieces += [x[pos:slab0], jnp.where(mask, value, x[slab0:slab0 + 8])]
            pos = slab0 + 8
        return jnp.concatenate([p for p in pieces + [x[pos:]] if p.shape[0]], axis=0)

    hq = ML_HEADS * ML_QK
    n_rounds = 2 * hq // FF_CHUNK

    def round_matmuls(c):
        cols = slice(c * FF_CHUNK, (c + 1) * FF_CHUNK)
        return (jnp.dot(xm_scr[...], wqk_ref[:, cols], preferred_element_type=F32),
                jnp.dot(xm, wv_ref[:, cols], preferred_element_type=F32),
                jnp.dot(xm, wo_ref[:, cols], preferred_element_type=F32))

    pending = round_matmuls(0)
    for c in range(n_rounds):
        cols = slice(c * FF_CHUNK, (c + 1) * FF_CHUNK)
        p, v_c, o_c = pending
        if c + 1 < n_rounds:
            pending = round_matmuls(c + 1)
        p_scr[...] = p
        main = p[HALO:HALO + tm]
        prev_row = p[HALO - 1:HALO] * has_prev
        next_row = p[HALO + tm:HALO + tm + 1] * has_next
        down = patched(p_scr[HALO - 1:HALO - 1 + tm, :],
                       [(0, 0, prev_row, None)] + [(e, 0, 0.0, is_ctx) for e in inner_ends])
        up = patched(p_scr[HALO + 1:HALO + 1 + tm, :],
                     [(tm - 8, 7, next_row, None)] + [(e - 8, 7, 0.0, is_ctx) for e in inner_ends])
        conv = down * cw_ref[0:1, cols] + main * cw_ref[1:2, cols] + up * cw_ref[2:3, cols] + cb_ref[:, cols]
        act = _silu(conv)
        if (c + 1) * FF_CHUNK <= hq:
            q_ref[:, cols] = act.astype(BF16)
        else:
            kact = act * (ML_QK ** -0.5)
            for rc in range(tm // lc):
                kt_ref[rc, c * FF_CHUNK - hq:(c + 1) * FF_CHUNK - hq, :] = (
                    jnp.transpose(kact[rc * lc:(rc + 1) * lc, :]).astype(BF16))
        v_ref[:, cols] = v_c.astype(BF16)
        sig_ref[:, cols] = _sigmoid(o_c).astype(BF16)

    g = jnp.dot(xm, wg_ref[...], preferred_element_type=F32) + gb_ref[...]
    logf = jax.nn.log_sigmoid(g)
    r_i = lax.broadcasted_iota(jnp.int32, (lc, lc), 0)
    c_i = lax.broadcasted_iota(jnp.int32, (lc, lc), 1)
    tri_lo = (c_i <= r_i).astype(BF16)
    lane = lax.broadcasted_iota(jnp.int32, (lc, 128), 1)
    trow = lax.broadcasted_iota(jnp.int32, (lc, 128), 0)
    nh = ML_HEADS
    for c in range(tm // lc):
        rows = slice(c * lc, (c + 1) * lc)
        lf = logf[rows]
        hi = lf.astype(BF16)
        r1 = lf - hi.astype(F32)
        mid = r1.astype(BF16)
        lo = (r1 - mid.astype(F32)).astype(BF16)
        pre3 = jnp.dot(tri_lo, jnp.concatenate([hi, mid, lo], axis=1), preferred_element_type=F32)
        pre = pre3[:, 0:128] + pre3[:, 128:256] + pre3[:, 256:384]
        suf = pre[lc - 1:lc, :] - pre + lf
        bc = pltpu.roll(jnp.where(lane < 3 * nh, pre, suf), 128 - 2 * nh, 1)
        a = g[rows] - bc
        pf = a
        pb = a
        k = 1
        while k < lc:
            pf = jnp.maximum(pf, jnp.where(trow >= k, pltpu.roll(pf, k, 0), -jnp.inf))
            pb = jnp.maximum(pb, jnp.where(trow < lc - k, pltpu.roll(pb, lc - k, 0), -jnp.inf))
            k *= 2
        pm = jnp.where(lane < nh, pf, pb)
        gc_ref[rows, :] = jnp.where(lane < 2 * nh, bc,
                                    jnp.where(lane < 4 * nh, pltpu.roll(a, 2 * nh, 1), pltpu.roll(pm, 4 * nh, 1)))
        at_ref[c] = jnp.transpose(a)[0:ML_UNITS, :]


def _mlproj(tk, h, mods, wqk, wv, wo, wg, gb, cw, cb):
    d = h.shape[1]
    tm = TOKEN_TILE
    lc = SCAN_CHUNK
    hq = ML_HEADS * ML_QK
    blocks_per_tile = tm // HALO
    last_block = tk.rows // HALO - 1
    return pl.pallas_call(
        functools.partial(_mlproj_kernel, tk=tk),
        grid=(tk.tiles,),
        in_specs=[
            _tok_spec(d),
            pl.BlockSpec((HALO, d), lambda t: (jnp.maximum(t * blocks_per_tile - 1, 0), 0)),
            pl.BlockSpec((HALO, d), lambda t: (jnp.minimum((t + 1) * blocks_per_tile, last_block), 0)),
            tk.mod_spec(d),
            _resident(wqk.shape), _resident(wv.shape), _resident(wo.shape), _resident(wg.shape),
            _resident(gb.shape), _resident(cw.shape), _resident(cb.shape),
        ],
        out_specs=[
            _tok_spec(hq), pl.BlockSpec((tm // lc, hq, lc), lambda t: (t, 0, 0)),
            _tok_spec(d), _tok_spec(d), _tok_spec(128),
            pl.BlockSpec((tm // lc, ML_UNITS, lc), lambda t: (t, 0, 0)),
        ],
        out_shape=[
            jax.ShapeDtypeStruct((tk.rows, hq), BF16),
            jax.ShapeDtypeStruct((tk.rows // lc, hq, lc), BF16),
            jax.ShapeDtypeStruct((tk.rows, d), BF16),
            jax.ShapeDtypeStruct((tk.rows, d), BF16),
            jax.ShapeDtypeStruct((tk.rows, 128), F32),
            jax.ShapeDtypeStruct((tk.rows // lc, ML_UNITS, lc), F32),
        ],
        scratch_shapes=[pltpu.VMEM((tm + 2 * HALO, d), BF16), pltpu.VMEM((tm + 2 * HALO, FF_CHUNK), F32)],
        compiler_params=_params("parallel"),
        name="ml_proj",
    )(h, h, h, mods, wqk, wv, wo, wg, gb, cw, cb)


def _scan_kernel(ql_ref, qc_ref, ktl_ref, ktc_ref, vl_ref, vc_ref, gl_ref, gcx_ref, al_ref, ac_ref, o_ref,
                 ct_scr, m_scr, *, n_lat_chunks, n_ctx_chunks):
    lc = SCAN_CHUNK
    dk, dv = ML_QK, ML_V
    assert lc == 128 and dk == 128 and dv == 2 * 128
    ct_scr[...] = jnp.zeros_like(ct_scr)
    m_scr[...] = jnp.zeros_like(m_scr)
    t_i = lax.broadcasted_iota(jnp.int32, (lc, lc), 0)
    s_i = lax.broadcasted_iota(jnp.int32, (lc, lc), 1)
    masks = (s_i <= t_i, s_i >= t_i)
    ones_blk = jnp.ones((lc, 128), BF16)

    def step(j, mode):
        if mode == "ctx":
            q_ref, kt_ref, v_ref, g_ref, a_ref, n_seq = qc_ref, ktc_ref, vc_ref, gcx_ref, ac_ref, n_ctx_chunks
        else:
            q_ref, kt_ref, v_ref, g_ref, a_ref, n_seq = ql_ref, ktl_ref, vl_ref, gl_ref, al_ref, n_lat_chunks
        chunk = (j, n_seq - 1 - j)
        units = []
        for direction in range(2):
            c = chunk[direction]
            r0 = pl.multiple_of(c * lc, lc)
            gc = g_ref[pl.ds(r0, lc), :]
            at = a_ref[c]
            for hd in range(ML_HEADS):
                units.append((direction, hd, direction * ML_HEADS + hd, c, r0, gc, at))

        st = {}
        for direction, hd, u, c, r0, gc, at in units:
            last = lc - 1 if direction == 0 else 0
            kt = kt_ref[c, hd * dk:(hd + 1) * dk, :]
            v_aug = jnp.concatenate([v_ref[pl.ds(r0, lc), hd * dv:(hd + 1) * dv], ones_blk], axis=1)
            bcum = jnp.broadcast_to(gc[:, u:u + 1], (lc, 128))
            pmax = jnp.broadcast_to(gc[:, 2 * ML_UNITS + u:2 * ML_UNITS + u + 1], (lc, 128))
            a_row = at[u:u + 1, :]
            m = m_scr[u, 0:1, :]
            mu = jnp.maximum(m, pmax)
            mu_l = mu[last:last + 1, :]
            st[u] = dict(kt=kt, v_aug=v_aug, bcum=bcum, a_row=a_row, m=m, mu=mu, mu_l=mu_l, ct=ct_scr[u],
                         b_last=bcum[last:last + 1, :])
        if mode != "ctx":
            for direction, hd, u, c, r0, gc, at in units:
                x = st[u]
                x["qc"] = q_ref[pl.ds(r0, lc), hd * dk:(hd + 1) * dk]
                x["s"] = jnp.dot(x["qc"], x["kt"], preferred_element_type=F32)
            for direction, hd, u, c, r0, gc, at in units:
                x = st[u]
                w = jnp.where(masks[direction], jnp.exp(x["a_row"] - x["mu"]), 0.0)
                eq = jnp.exp(x["m"] - x["mu"]) * x["qc"].astype(F32)
                lhs = jnp.concatenate([(x["s"] * w).astype(BF16), eq.astype(BF16)], axis=1)
                rhs = jnp.concatenate([x["v_aug"], x["ct"].astype(BF16)], axis=0)
                x["num"] = jnp.dot(lhs, rhs, preferred_element_type=F32)
        for direction, hd, u, c, r0, gc, at in units:
            x = st[u]
            ktw = (x["kt"].astype(F32) * jnp.exp(x["a_row"] - x["mu_l"])).astype(BF16)
            x["upd"] = jnp.dot(ktw, x["v_aug"], preferred_element_type=F32)
        if mode != "ctx":
            for direction, hd, u, c, r0, gc, at in units:
                x = st[u]
                num = x["num"]
                inv = 1.0 / jnp.maximum(jnp.abs(num[:, dv:]), jnp.exp(-(x["bcum"] + x["mu"])))
                hout = num[:, :dv] * jnp.concatenate([inv, inv], axis=1)
                if mode == "store":
                    o_ref[pl.ds(r0, lc), hd * dv:(hd + 1) * dv] = hout
                else:
                    o_ref[pl.ds(r0, lc), hd * dv:(hd + 1) * dv] += hout
        for direction, hd, u, c, r0, gc, at in units:
            x = st[u]
            decay = jnp.exp(x["m"] - x["mu_l"])
            ct_scr[u] = x["ct"] * jnp.concatenate([decay, decay, decay], axis=1) + x["upd"]
            m_scr[u, 0:1, :] = x["b_last"] + x["mu_l"]

    half = n_lat_chunks // 2
    lax.fori_loop(0, n_ctx_chunks, lambda j, _: step(j, "ctx"), None)
    lax.fori_loop(0, half, lambda j, _: step(j, "store"), None)
    lax.fori_loop(half, n_lat_chunks, lambda j, _: step(j, "add"), None)


def _scan(tk, q, kt, v, gc, at):
    d = v.shape[1]
    lc = SCAN_CHUNK
    hq = ML_HEADS * ML_QK
    n_lat, n_ctx = tk.n_lat, tk.n_ctx
    n_lat_chunks, n_ctx_chunks = n_lat // lc, n_ctx // lc
    assert n_lat_chunks % 2 == 0
    ctx0 = tk.lat_rows // n_ctx
    ctx_chunk0 = tk.lat_rows // lc // n_ctx_chunks
    lat = lambda width: pl.BlockSpec((n_lat, width), lambda i: (i, 0))
    ctx = lambda width: pl.BlockSpec((n_ctx, width), lambda i: (ctx0 + i, 0))
    lat_chunks = lambda rows: pl.BlockSpec((n_lat_chunks, rows, lc), lambda i: (i, 0, 0))
    ctx_chunks = lambda rows: pl.BlockSpec((n_ctx_chunks, rows, lc), lambda i: (ctx_chunk0 + i, 0, 0))
    kern = functools.partial(_scan_kernel, n_lat_chunks=n_lat_chunks, n_ctx_chunks=n_ctx_chunks)
    return pl.pallas_call(
        kern,
        grid=(tk.bsz,),
        in_specs=[
            lat(hq), ctx(hq), lat_chunks(hq), ctx_chunks(hq), lat(d), ctx(d), lat(128), ctx(128),
            lat_chunks(ML_UNITS), ctx_chunks(ML_UNITS),
        ],
        out_specs=lat(d),
        out_shape=jax.ShapeDtypeStruct((tk.lat_rows, d), F32),
        scratch_shapes=[
            pltpu.VMEM((ML_UNITS, ML_QK, ML_V + 128), F32),
            pltpu.VMEM((ML_UNITS, 8, 128), F32),
        ],
        compiler_params=_params("parallel"),
        name="ml_scan",
    )(q, q, kt, kt, v, v, gc, gc, at, at)


def _rope_tables(n_lat, tile):
    rows = n_lat // GRID_W
    row = jnp.repeat(jnp.arange(rows, dtype=jnp.int32), GRID_W).astype(F32)
    col = jnp.tile(jnp.arange(GRID_W, dtype=jnp.int32), rows).astype(F32)
    inv = ROPE_THETA ** (-jnp.arange(ROPE_PAIRS, dtype=F32) / ROPE_PAIRS)
    ar, ac = row[:, None] * inv, col[:, None] * inv
    cos = jnp.concatenate([jnp.cos(ar), jnp.cos(ac), jnp.cos(ar), jnp.cos(ac)], axis=1)
    sin = jnp.concatenate([-jnp.sin(ar), -jnp.sin(ac), jnp.sin(ar), jnp.sin(ac)], axis=1)
    cos = jnp.concatenate([cos, jnp.ones((tile, ATT_HEAD_DIM), F32)], axis=0)
    sin = jnp.concatenate([sin, jnp.zeros((tile, ATT_HEAD_DIM), F32)], axis=0)
    return cos, sin


def _rope_column_order():
    p = ROPE_PAIRS
    return jnp.concatenate([jnp.arange(0, p), jnp.arange(2 * p, 3 * p), jnp.arange(p, 2 * p), jnp.arange(3 * p, 4 * p)])


def kernel(x, c, ctx, c_ctx, ada_w, ada_b, ln_g, ln_b, ffn_w_in, ffn_w_out, att_w_in, att_q_gain, att_k_gain,
           att_w_out, ml_w_in, ml_gate_b, ml_conv_w, ml_conv_b, ml_norm_g, ml_w_out):
    bsz, n_lat, d = x.shape
    n_ctx = ctx.shape[1]
    depth = ada_w.shape[0]
    dff = ffn_w_out.shape[2]
    assert depth == 2, "layer 0 attention, layer 1 (last) mLSTM"
    assert dff % FF_CHUNK == 0 and TOKEN_TILE % SCAN_CHUNK == 0
    tk = _Tokens(bsz, n_lat, n_ctx)
    alpha = (2.0 * depth) ** 0.25

    mod_rows = -(-(bsz + 1) // 8) * 8
    cc = jnp.concatenate([c, c_ctx[None, :], jnp.zeros((mod_rows - bsz - 1, d), F32)], axis=0)
    mods = _ada(cc, ada_w, ada_b).reshape(depth, mod_rows, N_MOD, d)

    win = ffn_w_in.astype(BF16)
    wout = ffn_w_out.astype(BF16)
    row2 = lambda a: a.reshape(1, -1)
    tk_wide = _Tokens(bsz, n_lat, n_ctx, WIDE_TOKEN_TILE)
    cos, sin = _rope_tables(n_lat, WIDE_TOKEN_TILE)

    m_i = mods[0]
    h = _ffn(tk, (x.reshape(bsz * n_lat, d), ctx.reshape(bsz * n_ctx, d)), m_i, tk.tiles, win, wout, (0, 0),
             row2(ln_g[0, 0]), row2(ln_b[0, 0]), 0, alpha)
    dh = ATT_HEAD_DIM
    nqk = (ATT_HEADS + ATT_KV_HEADS) * dh
    order = _rope_column_order()
    wqk = att_w_in[0][:, :nqk].reshape(d, ATT_HEADS + ATT_KV_HEADS, dh)[:, :, order].reshape(d, nqk).astype(BF16)
    wvt = att_w_in[0][:, nqk:].T.astype(BF16)
    q, k, vt = _qkv(tk_wide, h, m_i, wqk, wvt, row2(att_q_gain[0][order]), row2(att_k_gain[0][order]), cos, sin)
    o = _attention(tk, q, k, vt)
    h = _ffn(tk, h, m_i, tk.tiles, win, wout, (0, 1), row2(ln_g[0, 2]), row2(ln_b[0, 2]), 6, alpha,
             mixer=(o, att_w_out[0].astype(BF16), row2(ln_g[0, 1]), row2(ln_b[0, 1])))

    m_i = mods[1]
    h = _ffn(tk, h, m_i, tk.tiles, win, wout, (1, 0), row2(ln_g[1, 0]), row2(ln_b[1, 0]), 0, alpha)
    hq = ML_HEADS * ML_QK
    w = ml_w_in[0]
    wqk = w[:, :2 * hq].astype(BF16)
    wv = w[:, 2 * hq:2 * hq + d].astype(BF16)
    wo = w[:, 2 * hq + d:2 * hq + 2 * d].astype(BF16)
    perm = jnp.array([0, 2, 1, 3])
    wg = w[:, 2 * hq + 2 * d:].reshape(d, 4, ML_HEADS)[:, perm].reshape(d, 4 * ML_HEADS)
    wg = jnp.pad(wg, ((0, 0), (0, 128 - 4 * ML_HEADS))).astype(BF16)
    gb = ml_gate_b[0].reshape(4, ML_HEADS)[perm].reshape(1, 4 * ML_HEADS)
    gb = jnp.pad(gb, ((0, 0), (0, 128 - 4 * ML_HEADS)))
    q, kt, v, sig, gc, at = _mlproj(tk, h, m_i, wqk, wv, wo, wg, gb, ml_conv_w[0], row2(ml_conv_b[0]))
    hs = _scan(tk, q, kt, v, gc, at)
    h = _ffn(tk, h, m_i, tk.lat_tiles, win, wout, (1, 1), row2(ln_g[1, 2]), row2(ln_b[1, 2]), 6, alpha,
             mixer=(hs, sig, row2(ml_norm_g[0]), ml_w_out[0].astype(BF16), row2(ln_g[1, 1]), row2(ln_b[1, 1])))
    return h.reshape(bsz, n_lat, d)
```

```python
import functools

import jax
import jax.numpy as jnp
from jax import lax
from jax.experimental import pallas as pl
from jax.experimental.pallas import tpu as pltpu

F32 = jnp.float32
BF16 = jnp.bfloat16

N_MOD = 9
GRID_W = 64
ATT_HEADS = 8
ATT_KV_HEADS = 2
ATT_GROUP = ATT_HEADS // ATT_KV_HEADS
ATT_HEAD_DIM = 128
ROPE_PAIRS = ATT_HEAD_DIM // 4
ROPE_THETA = 10000.0
ML_HEADS = 4
ML_QK = 128
ML_V = 256
ML_UNITS = 2 * ML_HEADS
LN_EPS = 1e-5
RMS_EPS = 1e-6
LOG2_E = 1.4426950408889634

TOKEN_TILE = 1024
WIDE_TOKEN_TILE = 2048
SUB_TILE = 512
ATT_Q_TILE = 256
ATT_KEY_BLOCK = 512
FF_CHUNK = 256
FFN_HIDDEN_CHUNK = 256
FFN_PIN_ROWS = 128
SCAN_CHUNK = 128
HALO = 16
VMEM_LIMIT = 56 * 1024 * 1024


def _params(*sem):
    return pltpu.CompilerParams(dimension_semantics=sem, vmem_limit_bytes=VMEM_LIMIT)


def _resident(shape, lead=()):
    n = len(shape)
    block = (None,) * len(lead) + tuple(shape)
    return pl.BlockSpec(block, lambda *_: tuple(lead) + (0,) * n, pipeline_mode=pl.Buffered(1))


def _post_norm(h, gate, y, g, b, alpha):
    x = h + (gate * (1.0 / alpha)) * y
    mu = jnp.mean(x, axis=-1, keepdims=True)
    xc = x - mu
    var = jnp.mean(xc * xc, axis=-1, keepdims=True)
    return xc * lax.rsqrt(var + LN_EPS / (alpha * alpha)) * g + b


def _sigmoid(x):
    return 0.5 * jnp.tanh(0.5 * x) + 0.5


def _silu(x):
    return x * _sigmoid(x)


def _modulated(h, mod_ref, k0):
    return h * (1.0 + mod_ref[k0 + 1:k0 + 2, :]) + mod_ref[k0:k0 + 1, :]


class _Tokens:
    def __init__(self, bsz, n_lat, n_ctx, tile=TOKEN_TILE):
        tm = tile
        assert n_lat % tm == 0 and (bsz * n_ctx) % tm == 0 and tm % n_ctx == 0
        self.tile = tile
        self.bsz, self.n_lat, self.n_ctx = bsz, n_lat, n_ctx
        self.lat_rows = bsz * n_lat
        self.rows = bsz * (n_lat + n_ctx)
        self.tiles_per_seq = n_lat // tm
        self.lat_tiles = self.lat_rows // tm
        self.tiles = self.rows // tm

    def is_lat(self, t):
        return t < self.lat_tiles

    def mod_row(self, t):
        return jnp.where(t < self.lat_tiles, t // self.tiles_per_seq, self.bsz)

    def mod_spec(self, d):
        return pl.BlockSpec((None, N_MOD, d), lambda t: (self.mod_row(t), 0, 0))

    def tok(self, width):
        return pl.BlockSpec((self.tile, width), lambda t: (t, 0))


def _tok_spec(width):
    return pl.BlockSpec((TOKEN_TILE, width), lambda t: (t, 0))


def _ada_kernel(c_ref, w_ref, b_ref, o_ref):
    s = _silu(c_ref[...]).astype(BF16)
    o_ref[...] = jnp.dot(s, w_ref[...].astype(BF16), preferred_element_type=F32) + b_ref[...]


def _ada(cc, ada_w, ada_b):
    depth, d, nd = ada_w.shape
    rows = cc.shape[0]
    return pl.pallas_call(
        _ada_kernel,
        grid=(depth, nd // d),
        in_specs=[
            pl.BlockSpec((rows, d), lambda i, j: (0, 0)),
            pl.BlockSpec((None, d, d), lambda i, j: (i, 0, j)),
            pl.BlockSpec((None, 1, d), lambda i, j: (i, 0, j)),
        ],
        out_specs=pl.BlockSpec((None, rows, d), lambda i, j: (i, 0, j)),
        out_shape=jax.ShapeDtypeStruct((depth, rows, nd), F32),
        compiler_params=_params("parallel", "parallel"),
        name="ada_mod",
    )(cc, ada_w, ada_b.reshape(depth, 1, nd))


def _ffn_kernel(*refs, k0, alpha, dff, lat_tiles, source):
    if source == "split":
        hx_ref, hc_ref, mod_ref, win_ref, wout_ref, g_ref, b_ref, o_ref = refs
        is_lat = pl.program_id(0) < lat_tiles
        load = lambda rows: jnp.where(is_lat, hx_ref[rows, :], hc_ref[rows, :])
    elif source == "mixer":
        z_ref, h_ref, mod_ref, wp_ref, gp_ref, bp_ref, win_ref, wout_ref, g_ref, b_ref, o_ref = refs

        def load(rows):
            y = jnp.dot(z_ref[rows, :], wp_ref[...], preferred_element_type=F32)
            return _post_norm(h_ref[rows, :], mod_ref[5:6, :], y, gp_ref[...], bp_ref[...], alpha)
    elif source == "mlstm":
        hs_ref, sig_ref, h_ref, mod_ref, ng_ref, wp_ref, gp_ref, bp_ref, win_ref, wout_ref, g_ref, b_ref, o_ref = refs

        def load(rows):
            parts = []
            for hd in range(ML_HEADS):
                x = hs_ref[rows, hd * ML_V:(hd + 1) * ML_V]
                mu = jnp.mean(x, axis=-1, keepdims=True)
                xc = x - mu
                var = jnp.mean(xc * xc, axis=-1, keepdims=True)
                parts.append(xc * lax.rsqrt(var + LN_EPS))
            hn = jnp.concatenate(parts, axis=1) * ng_ref[...]
            z = (hn * sig_ref[rows, :].astype(F32)).astype(BF16)
            y = jnp.dot(z, wp_ref[...], preferred_element_type=F32)
            return _post_norm(h_ref[rows, :], mod_ref[5:6, :], y, gp_ref[...], bp_ref[...], alpha)
    else:
        h_ref, mod_ref, win_ref, wout_ref, g_ref, b_ref, o_ref = refs
        load = lambda rows: h_ref[rows, :]
    n_sub = o_ref.shape[0] // SUB_TILE
    sub = lambda r: slice(r * SUB_TILE, (r + 1) * SUB_TILE)

    chunks = [slice(c0, min(c0 + FFN_HIDDEN_CHUNK, dff)) for c0 in range(0, dff, FFN_HIDDEN_CHUNK)]
    gate = mod_ref[k0 + 2:k0 + 3, :]

    pieces = [slice(q0, q0 + FFN_PIN_ROWS) for q0 in range(0, SUB_TILE, FFN_PIN_ROWS)]

    def ffn_chunks(xm, acc, part, anchors=()):
        for n, cols in enumerate(part):
            a = jnp.dot(xm, win_ref[:, cols], preferred_element_type=F32)
            if n < len(anchors):
                a = a + anchors[n]
            u = jnp.dot(xm, win_ref[:, dff + cols.start:dff + cols.stop], preferred_element_type=F32)
            hid = (_silu(a) * u).astype(BF16)
            y = jnp.dot(hid, wout_ref[cols, :], preferred_element_type=F32)
            acc = y if acc is None else acc + y
        return acc

    def post_ln(r, h, acc):
        outs = []
        for rows in pieces:
            out = _post_norm(h[rows], 0.5 * gate, acc[rows], g_ref[...], b_ref[...], alpha)
            o_ref[r * SUB_TILE + rows.start:r * SUB_TILE + rows.stop, :] = out
            outs.append(out)
        return outs

    def zero_after(x):
        tile_rows = 8 * 4 // x.dtype.itemsize
        red = jnp.max(x.reshape(x.shape[0] // tile_rows, tile_rows, x.shape[1]), axis=0).astype(F32)
        red = functools.reduce(jnp.maximum, [red[:, l:l + 128] for l in range(0, x.shape[1], 128)])
        bits = pltpu.bitcast(red[0:8, :], jnp.uint32)
        zero = ((bits >> 16) >> 16).astype(F32)[0:1, :]
        return jnp.concatenate([zero] * (FFN_HIDDEN_CHUNK // 128), axis=1)

    n_pin = len(pieces)
    tail = len(chunks) - n_pin
    head = 2
    assert head + n_pin <= tail
    h = load(sub(0))
    xm = _modulated(h, mod_ref, k0).astype(BF16)
    for r in range(n_sub):
        last = r + 1 == n_sub
        acc = ffn_chunks(xm, None, chunks[:tail]) if r == 0 else acc
        if not last:
            h_next = load(sub(r + 1))
            xm_parts = [_modulated(h_next[rows], mod_ref, k0).astype(BF16) for rows in pieces]
            xm_next = jnp.concatenate(xm_parts, axis=0)
            acc = ffn_chunks(xm, acc, chunks[tail:], [zero_after(p) for p in xm_parts])
            acc_next = ffn_chunks(xm_next, None, chunks[:head])
            ln_anchors = [zero_after(p) for p in post_ln(r, h, acc)]
            acc = ffn_chunks(xm_next, acc_next, chunks[head:tail], ln_anchors)
            h, xm = h_next, xm_next
        else:
            acc = ffn_chunks(xm, acc, chunks[tail:])
            post_ln(r, h, acc)


def _ffn(tk, hs, mods, n_tiles, win, wout, which, g, b, k0, alpha, mixer=None):
    dff, d = wout.shape[-2:]
    tm = tk.tile
    _tok_spec = tk.tok
    extra_specs, extra = [], []
    if isinstance(hs, tuple):
        source = "split"
        last_lat = tk.lat_tiles - 1
        h_specs = [pl.BlockSpec((tm, d), lambda t: (jnp.minimum(t, last_lat), 0)),
                   pl.BlockSpec((tm, d), lambda t: (jnp.maximum(t - tk.lat_tiles, 0), 0))]
        hs = list(hs)
    elif mixer is not None and len(mixer) == 4:
        source = "mixer"
        z, wp, gp, bp = mixer
        h_specs = [_tok_spec(z.shape[1]), _tok_spec(d)]
        hs = [z, hs]
        extra_specs = [_resident(wp.shape), _resident((1, d)), _resident((1, d))]
        extra = [wp, gp, bp]
    elif mixer is not None:
        source = "mlstm"
        scan_out, sig, ng, wp, gp, bp = mixer
        h_specs = [_tok_spec(d), _tok_spec(d), _tok_spec(d)]
        hs = [scan_out, sig, hs]
        extra_specs = [_resident((1, d)), _resident(wp.shape), _resident((1, d)), _resident((1, d))]
        extra = [ng, wp, gp, bp]
    else:
        source = "plain"
        h_specs = [_tok_spec(d)]
        hs = [hs]
    kern = functools.partial(_ffn_kernel, k0=k0, alpha=alpha, dff=dff, lat_tiles=tk.lat_tiles, source=source)
    return pl.pallas_call(
        kern,
        grid=(n_tiles,),
        in_specs=h_specs + [tk.mod_spec(d)] + extra_specs + [
            _resident(win.shape[-2:], which), _resident(wout.shape[-2:], which), _resident((1, d)), _resident((1, d))],
        out_specs=_tok_spec(d),
        out_shape=jax.ShapeDtypeStruct((n_tiles * tm, d), F32),
        compiler_params=_params("parallel"),
        name="half_ffn",
    )(*hs, mods, *extra, win, wout, g, b)


def _qkv_kernel(h_ref, mod_ref, wqk_ref, wvt_ref, qg_ref, kg_ref, ct_ref, st_ref, q_ref, k_ref, vt_ref):
    xm = _modulated(h_ref[...], mod_ref, 3).astype(BF16)
    ct = ct_ref[...]
    st = st_ref[...]
    dh = ATT_HEAD_DIM

    ones_w = jnp.ones((2 * dh, dh), BF16)

    def norm_rope(x, gain):
        sq = x * x
        hi = sq.astype(BF16)
        lo = (sq - hi.astype(F32)).astype(BF16)
        ssq = jnp.dot(jnp.concatenate([hi, lo], axis=1), ones_w, preferred_element_type=F32)
        xn = x * lax.rsqrt(ssq * (1.0 / dh) + RMS_EPS) * gain
        return xn * ct + pltpu.roll(xn, dh // 2, 1) * st

    qg = qg_ref[...] * (dh ** -0.5 * LOG2_E)
    kg = kg_ref[...]
    n_pairs = (ATT_HEADS + ATT_KV_HEADS) // 2
    pair_dot = lambda i: jnp.dot(xm, wqk_ref[:, 2 * i * dh:2 * (i + 1) * dh], preferred_element_type=F32)
    pending = pair_dot(0)
    for pair in range(n_pairs):
        p = pending
        if pair + 1 < n_pairs:
            pending = pair_dot(pair + 1)
        for half in range(2):
            hd = 2 * pair + half
            x = p[:, half * dh:(half + 1) * dh]
            if hd < ATT_HEADS:
                q_ref[:, hd * dh:(hd + 1) * dh] = norm_rope(x, qg).astype(BF16)
            else:
                hk = hd - ATT_HEADS
                k_ref[:, hk * dh:(hk + 1) * dh] = norm_rope(x, kg).astype(BF16)
    vt = lax.dot_general(wvt_ref[...], xm, (((1,), (1,)), ((), ())), preferred_element_type=F32)
    vt_ref[...] = vt.astype(BF16)


def _qkv(tk, h, mods, wqk, wvt, qg, kg, ctab, stab):
    d = h.shape[1]
    tm = tk.tile
    _tok_spec = tk.tok
    dh = ATT_HEAD_DIM
    nq, nk = ATT_HEADS * dh, ATT_KV_HEADS * dh
    tab_spec = pl.BlockSpec((tm, dh), lambda t: (jnp.where(tk.is_lat(t), t % tk.tiles_per_seq, tk.tiles_per_seq), 0))
    return pl.pallas_call(
        _qkv_kernel,
        grid=(tk.tiles,),
        in_specs=[_tok_spec(d), tk.mod_spec(d), _resident(wqk.shape), _resident(wvt.shape),
                  _resident((1, dh)), _resident((1, dh)), tab_spec, tab_spec],
        out_specs=[_tok_spec(nq), _tok_spec(nk), pl.BlockSpec((nk, tm), lambda t: (0, t))],
        out_shape=[jax.ShapeDtypeStruct((tk.rows, nq), BF16),
                   jax.ShapeDtypeStruct((tk.rows, nk), BF16),
                   jax.ShapeDtypeStruct((nk, tk.rows), BF16)],
        compiler_params=_params("parallel"),
        name="att_qkv",
    )(h, mods, wqk, wvt, qg, kg, ctab, stab)


def _attn_kernel(q_ref, kl_ref, kc_ref, vlt_ref, vct_ref, o_ref, s_scr, *, lat_q_tiles):
    dh = ATT_HEAD_DIM
    kb = ATT_KEY_BLOCK
    tq = q_ref.shape[0]
    n_lat, n_ctx = kl_ref.shape[0], kc_ref.shape[0]
    nt = (((1,), (1,)), ((), ()))

    def key_blocks(with_lat):
        blocks = []
        if with_lat:
            blocks += [(kl_ref, vlt_ref, b * kb, b * kb, kb) for b in range(n_lat // kb)]
        base = n_lat if with_lat else 0
        kc_blk = min(kb, n_ctx)
        blocks += [(kc_ref, vct_ref, b * kc_blk, base + b * kc_blk, kc_blk) for b in range(n_ctx // kc_blk)]
        return blocks

    def scores(hd, blocks):
        kv = (hd // ATT_GROUP) * dh
        q = q_ref[:, hd * dh:(hd + 1) * dh]
        m8 = None
        for k_ref, _, r0, s0, n in blocks:
            s = lax.dot_general(k_ref[r0:r0 + n, kv:kv + dh], q, nt, preferred_element_type=F32)
            s_scr[hd % 2, s0:s0 + n, :] = s
            bm = jnp.max(s.reshape(n // 8, 8, tq), axis=0)
            m8 = bm if m8 is None else jnp.maximum(m8, bm)
        return jnp.max(m8, axis=0, keepdims=True)

    def weighted_values(hd, blocks, m):
        kv = (hd // ATT_GROUP) * dh
        l8 = None
        ot = None
        for _, vt_ref, r0, s0, n in blocks:
            p = jnp.exp2(s_scr[hd % 2, s0:s0 + n, :] - m)
            ps = jnp.sum(p.reshape(n // 8, 8, tq), axis=0)
            l8 = ps if l8 is None else l8 + ps
            o_blk = jnp.dot(vt_ref[kv:kv + dh, r0:r0 + n], p.astype(BF16), preferred_element_type=F32)
            ot = o_blk if ot is None else ot + o_blk
        denom = jnp.sum(l8, axis=0, keepdims=True)
        o_ref[:, hd * dh:(hd + 1) * dh] = jnp.transpose(ot / denom).astype(BF16)

    def run(blocks):
        m = scores(0, blocks)
        for hd in range(ATT_HEADS):
            m_next = scores(hd + 1, blocks) if hd + 1 < ATT_HEADS else None
            weighted_values(hd, blocks, m)
            m = m_next

    t = pl.program_id(1)

    @pl.when(t < lat_q_tiles)
    def _():
        run(key_blocks(True))

    @pl.when(t >= lat_q_tiles)
    def _():
        run(key_blocks(False))


def _attention(tk, q, k, vt):
    tq = ATT_Q_TILE
    dh = ATT_HEAD_DIM
    nq, nk = ATT_HEADS * dh, ATT_KV_HEADS * dh
    n_lat, n_ctx = tk.n_lat, tk.n_ctx
    assert n_lat % tq == 0 and n_ctx % tq == 0
    lat_q_tiles = n_lat // tq
    q_tiles = lat_q_tiles + n_ctx // tq
    ctx_block0 = tk.lat_rows // n_ctx

    def q_row(i, t):
        return jnp.where(t < lat_q_tiles, i * lat_q_tiles + t,
                         tk.lat_rows // tq + i * (n_ctx // tq) + (t - lat_q_tiles))

    q_spec = pl.BlockSpec((tq, nq), lambda i, t: (q_row(i, t), 0))
    return pl.pallas_call(
        functools.partial(_attn_kernel, lat_q_tiles=lat_q_tiles),
        grid=(tk.bsz, q_tiles),
        in_specs=[q_spec,
                  pl.BlockSpec((n_lat, nk), lambda i, t: (i, 0)),
                  pl.BlockSpec((n_ctx, nk), lambda i, t: (ctx_block0 + i, 0)),
                  pl.BlockSpec((nk, n_lat), lambda i, t: (0, i)),
                  pl.BlockSpec((nk, n_ctx), lambda i, t: (0, ctx_block0 + i))],
        out_specs=q_spec,
        out_shape=jax.ShapeDtypeStruct(q.shape, BF16),
        scratch_shapes=[pltpu.VMEM((2, n_lat + n_ctx, tq), F32)],
        compiler_params=_params("parallel", "parallel"),
        name="att_core",
    )(q, k, k, vt, vt)


def _mlproj_kernel(h_ref, hp_ref, hn_ref, mod_ref, wqk_ref, wv_ref, wo_ref, wg_ref, gb_ref, cw_ref, cb_ref,
                   q_ref, kt_ref, v_ref, sig_ref, gc_ref, at_ref, xm_scr, p_scr, *, tk):
    tm = TOKEN_TILE
    lc = SCAN_CHUNK
    t = pl.program_id(0)
    xm = _modulated(h_ref[...], mod_ref, 3).astype(BF16)
    xm_scr[0:HALO, :] = _modulated(hp_ref[...], mod_ref, 3).astype(BF16)
    xm_scr[HALO:HALO + tm, :] = xm
    xm_scr[HALO + tm:tm + 2 * HALO, :] = _modulated(hn_ref[...], mod_ref, 3).astype(BF16)

    pos_in_seq = t % tk.tiles_per_seq
    is_ctx = jnp.logical_not(tk.is_lat(t))
    has_prev = jnp.logical_and(tk.is_lat(t), pos_in_seq != 0).astype(F32)
    has_next = jnp.logical_and(tk.is_lat(t), pos_in_seq != tk.tiles_per_seq - 1).astype(F32)
    row8 = lax.broadcasted_iota(jnp.int32, (8, FF_CHUNK), 0)
    inner_ends = range(tk.n_ctx, tm, tk.n_ctx)

    def patched(x, fixes):
        pieces, pos = [], 0
        for slab0, r, value, cond in sorted(fixes, key=lambda f: f[0]):
            mask = row8 == r if cond is None else jnp.logical_and(row8 == r, cond)
            pieces += [x[pos:slab0], jnp.where(mask, value, x[slab0:slab0 + 8])]
            pos = slab0 + 8
        return jnp.concatenate([p for p in pieces + [x[pos:]] if p.shape[0]], axis=0)

    hq = ML_HEADS * ML_QK
    n_rounds = 2 * hq // FF_CHUNK

    def round_matmuls(c):
        cols = slice(c * FF_CHUNK, (c + 1) * FF_CHUNK)
        return (jnp.dot(xm_scr[...], wqk_ref[:, cols], preferred_element_type=F32),
                jnp.dot(xm, wv_ref[:, cols], preferred_element_type=F32),
                jnp.dot(xm, wo_ref[:, cols], preferred_element_type=F32))

    pending = round_matmuls(0)
    for c in range(n_rounds):
        cols = slice(c * FF_CHUNK, (c + 1) * FF_CHUNK)
        p, v_c, o_c = pending
        if c + 1 < n_rounds:
            pending = round_matmuls(c + 1)
        p_scr[...] = p
        main = p[HALO:HALO + tm]
        prev_row = p[HALO - 1:HALO] * has_prev
        next_row = p[HALO + tm:HALO + tm + 1] * has_next
        down = patched(p_scr[HALO - 1:HALO - 1 + tm, :],
                       [(0, 0, prev_row, None)] + [(e, 0, 0.0, is_ctx) for e in inner_ends])
        up = patched(p_scr[HALO + 1:HALO + 1 + tm, :],
                     [(tm - 8, 7, next_row, None)] + [(e - 8, 7, 0.0, is_ctx) for e in inner_ends])
        conv = down * cw_ref[0:1, cols] + main * cw_ref[1:2, cols] + up * cw_ref[2:3, cols] + cb_ref[:, cols]
        act = _silu(conv)
        if (c + 1) * FF_CHUNK <= hq:
            q_ref[:, cols] = act.astype(BF16)
        else:
            kact = act * (ML_QK ** -0.5)
            for rc in range(tm // lc):
                kt_ref[rc, c * FF_CHUNK - hq:(c + 1) * FF_CHUNK - hq, :] = (
                    jnp.transpose(kact[rc * lc:(rc + 1) * lc, :]).astype(BF16))
        v_ref[:, cols] = v_c.astype(BF16)
        sig_ref[:, cols] = _sigmoid(o_c).astype(BF16)

    g = jnp.dot(xm, wg_ref[...], preferred_element_type=F32) + gb_ref[...]
    logf = jax.nn.log_sigmoid(g)
    r_i = lax.broadcasted_iota(jnp.int32, (lc, lc), 0)
    c_i = lax.broadcasted_iota(jnp.int32, (lc, lc), 1)
    tri_lo = (c_i <= r_i).astype(BF16)
    lane = lax.broadcasted_iota(jnp.int32, (lc, 128), 1)
    trow = lax.broadcasted_iota(jnp.int32, (lc, 128), 0)
    nh = ML_HEADS
    for c in range(tm // lc):
        rows = slice(c * lc, (c + 1) * lc)
        lf = logf[rows]
        hi = lf.astype(BF16)
        r1 = lf - hi.astype(F32)
        mid = r1.astype(BF16)
        lo = (r1 - mid.astype(F32)).astype(BF16)
        pre3 = jnp.dot(tri_lo, jnp.concatenate([hi, mid, lo], axis=1), preferred_element_type=F32)
        pre = pre3[:, 0:128] + pre3[:, 128:256] + pre3[:, 256:384]
        suf = pre[lc - 1:lc, :] - pre + lf
        bc = pltpu.roll(jnp.where(lane < 3 * nh, pre, suf), 128 - 2 * nh, 1)
        a = g[rows] - bc
        pf = a
        pb = a
        k = 1
        while k < lc:
            pf = jnp.maximum(pf, jnp.where(trow >= k, pltpu.roll(pf, k, 0), -jnp.inf))
            pb = jnp.maximum(pb, jnp.where(trow < lc - k, pltpu.roll(pb, lc - k, 0), -jnp.inf))
            k *= 2
        pm = jnp.where(lane < nh, pf, pb)
        gc_ref[rows, :] = jnp.where(lane < 2 * nh, bc,
                                    jnp.where(lane < 4 * nh, pltpu.roll(a, 2 * nh, 1), pltpu.roll(pm, 4 * nh, 1)))
        at_ref[c] = jnp.transpose(a)[0:ML_UNITS, :]


def _mlproj(tk, h, mods, wqk, wv, wo, wg, gb, cw, cb):
    d = h.shape[1]
    tm = TOKEN_TILE
    lc = SCAN_CHUNK
    hq = ML_HEADS * ML_QK
    blocks_per_tile = tm // HALO
    last_block = tk.rows // HALO - 1
    return pl.pallas_call(
        functools.partial(_mlproj_kernel, tk=tk),
        grid=(tk.tiles,),
        in_specs=[
            _tok_spec(d),
            pl.BlockSpec((HALO, d), lambda t: (jnp.maximum(t * blocks_per_tile - 1, 0), 0)),
            pl.BlockSpec((HALO, d), lambda t: (jnp.minimum((t + 1) * blocks_per_tile, last_block), 0)),
            tk.mod_spec(d),
            _resident(wqk.shape), _resident(wv.shape), _resident(wo.shape), _resident(wg.shape),
            _resident(gb.shape), _resident(cw.shape), _resident(cb.shape),
        ],
        out_specs=[
            _tok_spec(hq), pl.BlockSpec((tm // lc, hq, lc), lambda t: (t, 0, 0)),
            _tok_spec(d), _tok_spec(d), _tok_spec(128),
            pl.BlockSpec((tm // lc, ML_UNITS, lc), lambda t: (t, 0, 0)),
        ],
        out_shape=[
            jax.ShapeDtypeStruct((tk.rows, hq), BF16),
            jax.ShapeDtypeStruct((tk.rows // lc, hq, lc), BF16),
            jax.ShapeDtypeStruct((tk.rows, d), BF16),
            jax.ShapeDtypeStruct((tk.rows, d), BF16),
            jax.ShapeDtypeStruct((tk.rows, 128), F32),
            jax.ShapeDtypeStruct((tk.rows // lc, ML_UNITS, lc), F32),
        ],
        scratch_shapes=[pltpu.VMEM((tm + 2 * HALO, d), BF16), pltpu.VMEM((tm + 2 * HALO, FF_CHUNK), F32)],
        compiler_params=_params("parallel"),
        name="ml_proj",
    )(h, h, h, mods, wqk, wv, wo, wg, gb, cw, cb)


def _scan_kernel(ql_ref, qc_ref, ktl_ref, ktc_ref, vl_ref, vc_ref, gl_ref, gcx_ref, al_ref, ac_ref, o_ref,
                 ct_scr, m_scr, *, n_lat_chunks, n_ctx_chunks):
    lc = SCAN_CHUNK
    dk, dv = ML_QK, ML_V
    assert lc == 128 and dk == 128 and dv == 2 * 128
    ct_scr[...] = jnp.zeros_like(ct_scr)
    m_scr[...] = jnp.zeros_like(m_scr)
    t_i = lax.broadcasted_iota(jnp.int32, (lc, lc), 0)
    s_i = lax.broadcasted_iota(jnp.int32, (lc, lc), 1)
    masks = (s_i <= t_i, s_i >= t_i)
    ones_blk = jnp.ones((lc, 128), BF16)

    def step(j, mode):
        if mode == "ctx":
            q_ref, kt_ref, v_ref, g_ref, a_ref, n_seq = qc_ref, ktc_ref, vc_ref, gcx_ref, ac_ref, n_ctx_chunks
        else:
            q_ref, kt_ref, v_ref, g_ref, a_ref, n_seq = ql_ref, ktl_ref, vl_ref, gl_ref, al_ref, n_lat_chunks
        chunk = (j, n_seq - 1 - j)
        units = []
        for direction in range(2):
            c = chunk[direction]
            r0 = pl.multiple_of(c * lc, lc)
            gc = g_ref[pl.ds(r0, lc), :]
            at = a_ref[c]
            for hd in range(ML_HEADS):
                units.append((direction, hd, direction * ML_HEADS + hd, c, r0, gc, at))

        st = {}
        for direction, hd, u, c, r0, gc, at in units:
            last = lc - 1 if direction == 0 else 0
            kt = kt_ref[c, hd * dk:(hd + 1) * dk, :]
            v_aug = jnp.concatenate([v_ref[pl.ds(r0, lc), hd * dv:(hd + 1) * dv], ones_blk], axis=1)
            bcum = jnp.broadcast_to(gc[:, u:u + 1], (lc, 128))
            pmax = jnp.broadcast_to(gc[:, 2 * ML_UNITS + u:2 * ML_UNITS + u + 1], (lc, 128))
            a_row = at[u:u + 1, :]
            m = m_scr[u, 0:1, :]
            mu = jnp.maximum(m, pmax)
            mu_l = mu[last:last + 1, :]
            st[u] = dict(kt=kt, v_aug=v_aug, bcum=bcum, a_row=a_row, m=m, mu=mu, mu_l=mu_l, ct=ct_scr[u],
                         b_last=bcum[last:last + 1, :])
        if mode != "ctx":
            for direction, hd, u, c, r0, gc, at in units:
                x = st[u]
                x["qc"] = q_ref[pl.ds(r0, lc), hd * dk:(hd + 1) * dk]
                x["s"] = jnp.dot(x["qc"], x["kt"], preferred_element_type=F32)
            for direction, hd, u, c, r0, gc, at in units:
                x = st[u]
                w = jnp.where(masks[direction], jnp.exp(x["a_row"] - x["mu"]), 0.0)
                eq = jnp.exp(x["m"] - x["mu"]) * x["qc"].astype(F32)
                lhs = jnp.concatenate([(x["s"] * w).astype(BF16), eq.astype(BF16)], axis=1)
                rhs = jnp.concatenate([x["v_aug"], x["ct"].astype(BF16)], axis=0)
                x["num"] = jnp.dot(lhs, rhs, preferred_element_type=F32)
        for direction, hd, u, c, r0, gc, at in units:
            x = st[u]
            ktw = (x["kt"].astype(F32) * jnp.exp(x["a_row"] - x["mu_l"])).astype(BF16)
            x["upd"] = jnp.dot(ktw, x["v_aug"], preferred_element_type=F32)
        if mode != "ctx":
            for direction, hd, u, c, r0, gc, at in units:
                x = st[u]
                num = x["num"]
                inv = 1.0 / jnp.maximum(jnp.abs(num[:, dv:]), jnp.exp(-(x["bcum"] + x["mu"])))
                hout = num[:, :dv] * jnp.concatenate([inv, inv], axis=1)
                if mode == "store":
                    o_ref[pl.ds(r0, lc), hd * dv:(hd + 1) * dv] = hout
                else:
                    o_ref[pl.ds(r0, lc), hd * dv:(hd + 1) * dv] += hout
        for direction, hd, u, c, r0, gc, at in units:
            x = st[u]
            decay = jnp.exp(x["m"] - x["mu_l"])
            ct_scr[u] = x["ct"] * jnp.concatenate([decay, decay, decay], axis=1) + x["upd"]
            m_scr[u, 0:1, :] = x["b_last"] + x["mu_l"]

    half = n_lat_chunks // 2
    lax.fori_loop(0, n_ctx_chunks, lambda j, _: step(j, "ctx"), None)
    lax.fori_loop(0, half, lambda j, _: step(j, "store"), None)
    lax.fori_loop(half, n_lat_chunks, lambda j, _: step(j, "add"), None)


def _scan(tk, q, kt, v, gc, at):
    d = v.shape[1]
    lc = SCAN_CHUNK
    hq = ML_HEADS * ML_QK
    n_lat, n_ctx = tk.n_lat, tk.n_ctx
    n_lat_chunks, n_ctx_chunks = n_lat // lc, n_ctx // lc
    assert n_lat_chunks % 2 == 0
    ctx0 = tk.lat_rows // n_ctx
    ctx_chunk0 = tk.lat_rows // lc // n_ctx_chunks
    lat = lambda width: pl.BlockSpec((n_lat, width), lambda i: (i, 0))
    ctx = lambda width: pl.BlockSpec((n_ctx, width), lambda i: (ctx0 + i, 0))
    lat_chunks = lambda rows: pl.BlockSpec((n_lat_chunks, rows, lc), lambda i: (i, 0, 0))
    ctx_chunks = lambda rows: pl.BlockSpec((n_ctx_chunks, rows, lc), lambda i: (ctx_chunk0 + i, 0, 0))
    kern = functools.partial(_scan_kernel, n_lat_chunks=n_lat_chunks, n_ctx_chunks=n_ctx_chunks)
    return pl.pallas_call(
        kern,
        grid=(tk.bsz,),
        in_specs=[
            lat(hq), ctx(hq), lat_chunks(hq), ctx_chunks(hq), lat(d), ctx(d), lat(128), ctx(128),
            lat_chunks(ML_UNITS), ctx_chunks(ML_UNITS),
        ],
        out_specs=lat(d),
        out_shape=jax.ShapeDtypeStruct((tk.lat_rows, d), F32),
        scratch_shapes=[
            pltpu.VMEM((ML_UNITS, ML_QK, ML_V + 128), F32),
            pltpu.VMEM((ML_UNITS, 8, 128), F32),
        ],
        compiler_params=_params("parallel"),
        name="ml_scan",
    )(q, q, kt, kt, v, v, gc, gc, at, at)


def _rope_tables(n_lat, tile):
    rows = n_lat // GRID_W
    row = jnp.repeat(jnp.arange(rows, dtype=jnp.int32), GRID_W).astype(F32)
    col = jnp.tile(jnp.arange(GRID_W, dtype=jnp.int32), rows).astype(F32)
    inv = ROPE_THETA ** (-jnp.arange(ROPE_PAIRS, dtype=F32) / ROPE_PAIRS)
    ar, ac = row[:, None] * inv, col[:, None] * inv
    cos = jnp.concatenate([jnp.cos(ar), jnp.cos(ac), jnp.cos(ar), jnp.cos(ac)], axis=1)
    sin = jnp.concatenate([-jnp.sin(ar), -jnp.sin(ac), jnp.sin(ar), jnp.sin(ac)], axis=1)
    cos = jnp.concatenate([cos, jnp.ones((tile, ATT_HEAD_DIM), F32)], axis=0)
    sin = jnp.concatenate([sin, jnp.zeros((tile, ATT_HEAD_DIM), F32)], axis=0)
    return cos, sin


def _rope_column_order():
    p = ROPE_PAIRS
    return jnp.concatenate([jnp.arange(0, p), jnp.arange(2 * p, 3 * p), jnp.arange(p, 2 * p), jnp.arange(3 * p, 4 * p)])


def kernel(x, c, ctx, c_ctx, ada_w, ada_b, ln_g, ln_b, ffn_w_in, ffn_w_out, att_w_in, att_q_gain, att_k_gain,
           att_w_out, ml_w_in, ml_gate_b, ml_conv_w, ml_conv_b, ml_norm_g, ml_w_out):
    bsz, n_lat, d = x.shape
    n_ctx = ctx.shape[1]
    depth = ada_w.shape[0]
    dff = ffn_w_out.shape[2]
    assert depth == 2, "layer 0 attention, layer 1 (last) mLSTM"
    assert dff % FF_CHUNK == 0 and TOKEN_TILE % SCAN_CHUNK == 0
    tk = _Tokens(bsz, n_lat, n_ctx)
    alpha = (2.0 * depth) ** 0.25

    mod_rows = -(-(bsz + 1) // 8) * 8
    cc = jnp.concatenate([c, c_ctx[None, :], jnp.zeros((mod_rows - bsz - 1, d), F32)], axis=0)
    mods = _ada(cc, ada_w, ada_b).reshape(depth, mod_rows, N_MOD, d)

    win = ffn_w_in.astype(BF16)
    wout = ffn_w_out.astype(BF16)
    row2 = lambda a: a.reshape(1, -1)
    tk_wide = _Tokens(bsz, n_lat, n_ctx, WIDE_TOKEN_TILE)
    cos, sin = _rope_tables(n_lat, WIDE_TOKEN_TILE)

    m_i = mods[0]
    h = _ffn(tk, (x.reshape(bsz * n_lat, d), ctx.reshape(bsz * n_ctx, d)), m_i, tk.tiles, win, wout, (0, 0),
             row2(ln_g[0, 0]), row2(ln_b[0, 0]), 0, alpha)
    dh = ATT_HEAD_DIM
    nqk = (ATT_HEADS + ATT_KV_HEADS) * dh
    order = _rope_column_order()
    wqk = att_w_in[0][:, :nqk].reshape(d, ATT_HEADS + ATT_KV_HEADS, dh)[:, :, order].reshape(d, nqk).astype(BF16)
    wvt = att_w_in[0][:, nqk:].T.astype(BF16)
    q, k, vt = _qkv(tk_wide, h, m_i, wqk, wvt, row2(att_q_gain[0][order]), row2(att_k_gain[0][order]), cos, sin)
    o = _attention(tk, q, k, vt)
    h = _ffn(tk, h, m_i, tk.tiles, win, wout, (0, 1), row2(ln_g[0, 2]), row2(ln_b[0, 2]), 6, alpha,
             mixer=(o, att_w_out[0].astype(BF16), row2(ln_g[0, 1]), row2(ln_b[0, 1])))

    m_i = mods[1]
    h = _ffn(tk, h, m_i, tk.tiles, win, wout, (1, 0), row2(ln_g[1, 0]), row2(ln_b[1, 0]), 0, alpha)
    hq = ML_HEADS * ML_QK
    w = ml_w_in[0]
    wqk = w[:, :2 * hq].astype(BF16)
    wv = w[:, 2 * hq:2 * hq + d].astype(BF16)
    wo = w[:, 2 * hq + d:2 * hq + 2 * d].astype(BF16)
    perm = jnp.array([0, 2, 1, 3])
    wg = w[:, 2 * hq + 2 * d:].reshape(d, 4, ML_HEADS)[:, perm].reshape(d, 4 * ML_HEADS)
    wg = jnp.pad(wg, ((0, 0), (0, 128 - 4 * ML_HEADS))).astype(BF16)
    gb = ml_gate_b[0].reshape(4, ML_HEADS)[perm].reshape(1, 4 * ML_HEADS)
    gb = jnp.pad(gb, ((0, 0), (0, 128 - 4 * ML_HEADS)))
    q, kt, v, sig, gc, at = _mlproj(tk, h, m_i, wqk, wv, wo, wg, gb, ml_conv_w[0], row2(ml_conv_b[0]))
    hs = _scan(tk, q, kt, v, gc, at)
    h = _ffn(tk, h, m_i, tk.lat_tiles, win, wout, (1, 1), row2(ln_g[1, 2]), row2(ln_b[1, 2]), 6, alpha,
             mixer=(hs, sig, row2(ml_norm_g[0]), ml_w_out[0].astype(BF16), row2(ln_g[1, 1]), row2(ln_b[1, 1])))
    return h.reshape(bsz, n_lat, d)
```
